```python
import math, functools
import jax, jax.numpy as jnp
from jax import lax
import numpy as np

D_MODEL = 2048
BATCH = 2
SEQ = 4096
DEPTH = 4

GRID_W = 64
CTX_LEN = 256
HEAD_DIM = 128
NORM_EPS = 1e-6
N_MOD = 9
D_FF = 5632
SGU_GROUPS = 4
SGU_WIDTH = SGU_GROUPS * HEAD_DIM
CHUNK = 128
N_Q_HEADS = 8
N_KV_HEADS = 2
Q_GROUP = N_Q_HEADS // N_KV_HEADS
ATTN_WIDTH = N_Q_HEADS * HEAD_DIM
KV_WIDTH = N_KV_HEADS * HEAD_DIM
WINDOW = 128
BLOCK = 128
ROPE_BASE = 10000.0
NEG_INF = -1e30
HY_GROUPS = 4
HY_WIDTH = HY_GROUPS * HEAD_DIM
HY_ORDER = 2
HY_SHORT = 3
HY_BANDS = 16
HY_EMB = 1 + 2 * HY_BANDS
HY_HIDDEN = 64
HY_TARGET = 1e-2
HY_FAST_DECAY = 0.3
HY_SLOW_DECAY = 1.5
D_MIX = SGU_WIDTH + ATTN_WIDTH + HY_WIDTH
A_END = 2 * SGU_WIDTH
Q_END = A_END + ATTN_WIDTH
K_END = Q_END + KV_WIDTH
V_END = K_END + KV_WIDTH
IN_COLS = V_END + (HY_ORDER + 1) * HY_WIDTH

kernel_name = 'hymba_style_sgu_swa_hyena_dit'


def rms_norm(x, g):
    xf = x.astype(jnp.float32)
    y = xf * lax.rsqrt(jnp.mean(xf * xf, axis=-1, keepdims=True) + NORM_EPS)
    return (y * g.astype(jnp.float32)).astype(x.dtype)


def layer_norm(x, g):
    xf = x.astype(jnp.float32)
    mu = jnp.mean(xf, axis=-1, keepdims=True)
    var = jnp.mean(jnp.square(xf - mu), axis=-1, keepdims=True)
    return ((xf - mu) * lax.rsqrt(var + NORM_EPS) * g.astype(jnp.float32)).astype(x.dtype)


def swiglu(h, w_gate, w_up, w_down):
    return (jax.nn.silu(h @ w_gate) * (h @ w_up)) @ w_down


def sub_pre(x, mod, k, g_pre):
    return rms_norm(x, g_pre) * (1.0 + mod[:, 3 * k][:, None]) + mod[:, 3 * k + 1][:, None]


def sub_post(x, y, mod, k, g_post, coef):
    return x + coef * mod[:, 3 * k + 2][:, None] * rms_norm(y, g_post)


def axial_rope(L):
    rows = L // GRID_W
    row = jnp.broadcast_to(jnp.arange(rows)[:, None], (rows, GRID_W)).reshape(L).astype(jnp.float32)
    col = jnp.broadcast_to(jnp.arange(GRID_W)[None, :], (rows, GRID_W)).reshape(L).astype(jnp.float32)
    half = HEAD_DIM // 2
    inv = ROPE_BASE ** (-jnp.arange(0, half, 2, dtype=jnp.float32) / half)
    ang_r = row[:, None] * inv
    ang_c = col[:, None] * inv
    ang = jnp.concatenate([ang_r, ang_r, ang_c, ang_c], axis=-1)
    return jnp.cos(ang), jnp.sin(ang)


def apply_rope(t, cos, sin):
    t1, t2, t3, t4 = jnp.split(t, 4, axis=-1)
    rot = jnp.concatenate([-t2, t1, -t4, t3], axis=-1)
    return (t * cos[:, None] + rot * sin[:, None]).astype(t.dtype)


def spatial_gating(a, g, w_s, b_s):
    B, L, _ = a.shape
    a = jax.nn.gelu(a, approximate=False)
    u, v = jnp.split(a, 2, axis=-1)
    v = layer_norm(v, g)
    vc = v.reshape(B, L // CHUNK, CHUNK, SGU_GROUPS, HEAD_DIM)
    mixed = jnp.einsum('bnpgc,gqp->bnqgc', vc, w_s) + b_s.T[None, None, :, :, None]
    return u * mixed.reshape(B, L, SGU_WIDTH)


def latent_attention(q, k, v, k_c, v_c, sink):
    B, L = q.shape[:2]
    Lc = k_c.shape[1]
    nb = L // BLOCK
    scale = HEAD_DIM ** -0.5
    qb = q.reshape(B, nb, BLOCK, N_KV_HEADS, Q_GROUP, HEAD_DIM)

    def band(t):
        tp = jnp.pad(t, ((0, 0), (BLOCK, BLOCK), (0, 0), (0, 0)))
        tp = tp.reshape(B, nb + 2, BLOCK, N_KV_HEADS, HEAD_DIM)
        return jnp.concatenate([tp[:, :-2], tp[:, 1:-1], tp[:, 2:]], axis=2)

    kb, vb = band(k), band(v)
    s_band = jnp.einsum('bnqkgd,bnrkd->bnkgqr', qb, kb).astype(jnp.float32) * scale
    qpos = jnp.arange(nb)[:, None, None] * BLOCK + jnp.arange(BLOCK)[None, :, None]
    kpos = (jnp.arange(nb)[:, None, None] - 1) * BLOCK + jnp.arange(3 * BLOCK)[None, None, :]
    valid = (jnp.abs(kpos - qpos) <= WINDOW) & (kpos >= 0) & (kpos < L)
    s_band = jnp.where(valid[None, :, None, None], s_band, NEG_INF)
    s_ctx = jnp.einsum('bnqkgd,bckd->bnkgqc', qb, k_c).astype(jnp.float32) * scale
    s_sink = jnp.broadcast_to(
        sink.astype(jnp.float32).reshape(N_KV_HEADS, Q_GROUP)[None, None, :, :, None, None],
        s_ctx.shape[:-1] + (1,))
    probs = jax.nn.softmax(jnp.concatenate([s_ctx, s_band, s_sink], axis=-1), axis=-1)
    p_ctx = probs[..., :Lc].astype(v.dtype)
    p_band = probs[..., Lc:Lc + 3 * BLOCK].astype(v.dtype)
    out = (jnp.einsum('bnkgqc,bckd->bnqkgd', p_ctx, v_c)
           + jnp.einsum('bnkgqr,bnrkd->bnqkgd', p_band, vb))
    return out.reshape(B, L, ATTN_WIDTH)


def context_attention(q_c, k_c, v_c, sink):
    B, Lc = q_c.shape[:2]
    qg = q_c.reshape(B, Lc, N_KV_HEADS, Q_GROUP, HEAD_DIM)
    s = jnp.einsum('bqkgd,bckd->bkgqc', qg, k_c).astype(jnp.float32) * HEAD_DIM ** -0.5
    s_sink = jnp.broadcast_to(
        sink.astype(jnp.float32).reshape(N_KV_HEADS, Q_GROUP)[None, :, :, None, None], s.shape[:-1] + (1,))
    probs = jax.nn.softmax(jnp.concatenate([s, s_sink], axis=-1), axis=-1)[..., :Lc].astype(v_c.dtype)
    return jnp.einsum('bkgqc,bckd->bqkgd', probs, v_c).reshape(B, Lc, ATTN_WIDTH)


def hyena_filter_spectrum(L, w1, b1, w2, b2, w3, freq):
    f32 = jnp.float32
    t = jnp.linspace(0.0, 1.0, L, dtype=f32)[:, None]
    w = (2.0 * math.pi / L) * jnp.arange(L, dtype=f32)[:, None]
    bands = jnp.linspace(1e-4, HY_BANDS - 1, HY_BANDS, dtype=f32)[None, :]
    feat = jnp.concatenate([t, jnp.cos(bands * w), -jnp.sin(bands * w)], axis=-1)
    fr = freq.astype(f32)
    hd = jnp.sin(fr * (feat @ w1.astype(f32) + b1.astype(f32)))
    hd = jnp.sin(fr * (hd @ w2.astype(f32) + b2.astype(f32)))
    h = (hd @ w3.astype(f32)).reshape(L, HY_ORDER, 2, HY_WIDTH)
    deltas = jnp.abs(jnp.linspace(math.log(HY_TARGET) / HY_FAST_DECAY,
                                  math.log(HY_TARGET) / HY_SLOW_DECAY, HY_WIDTH, dtype=f32))
    h = h * jnp.exp(-t * deltas)[:, None, None, :]
    h = h / (jnp.sum(jnp.abs(h), axis=(0, 2), keepdims=True) + 1e-6)
    kf, kb = h[:, :, 0], h[:, :, 1]
    k_circ = jnp.concatenate([kf[:1] + kb[:1], kf[1:], jnp.zeros_like(kf[:1]), kb[:0:-1]], axis=0)
    return jnp.fft.rfft(k_circ, axis=0)


def hyena(z, conv_w, conv_b, w1, b1, w2, b2, w3, freq, skip):
    B, L, _ = z.shape
    pad = HY_SHORT // 2
    zp = jnp.pad(z, ((0, 0), (pad, pad), (0, 0)))
    z = sum(zp[:, j:j + L] * conv_w[j] for j in range(HY_SHORT)) + conv_b
    v, x1, x2 = jnp.split(z, 3, axis=-1)
    spec = hyena_filter_spectrum(L, w1, b1, w2, b2, w3, freq)

    def long_conv(u, o):
        uf = u.astype(jnp.float32)
        U = jnp.fft.rfft(uf, n=2 * L, axis=1)
        y = jnp.fft.irfft(U * spec[None, :, o, :], n=2 * L, axis=1)[:, :L]
        return (y + uf * skip[o].astype(jnp.float32)).astype(u.dtype)

    s = x1 * long_conv(v, 0)
    return x2 * long_conv(s, 1)


def head_groups(z, k_c, v_c, rope, sgu_g, sgu_w, sgu_b, sink,
                hy_conv_w, hy_conv_b, hy_f_w1, hy_f_b1, hy_f_w2, hy_f_b2, hy_f_w3, hy_freq, hy_skip):
    B, L, _ = z.shape
    a_out = spatial_gating(z[..., :A_END], sgu_g, sgu_w, sgu_b)
    q = z[..., A_END:Q_END].reshape(B, L, N_Q_HEADS, HEAD_DIM)
    if rope is None:
        b_out = context_attention(q, k_c, v_c, sink)
    else:
        cos, sin = rope
        k = z[..., Q_END:K_END].reshape(B, L, N_KV_HEADS, HEAD_DIM)
        v = z[..., K_END:V_END].reshape(B, L, N_KV_HEADS, HEAD_DIM)
        b_out = latent_attention(apply_rope(q, cos, sin), apply_rope(k, cos, sin), v, k_c, v_c, sink)
    c_out = hyena(z[..., V_END:], hy_conv_w, hy_conv_b, hy_f_w1, hy_f_b1, hy_f_w2, hy_f_b2,
                  hy_f_w3, hy_freq, hy_skip)
    return jnp.concatenate([a_out, b_out, c_out], axis=-1)


def setup_inputs(seed: int = 0) -> dict:
    key = jax.random.key(seed)
    ks = iter(jax.random.split(key, 40))
    D = D_MODEL

    def nrm(shape, scale):
        return jax.random.normal(next(ks), shape, jnp.float32) * scale

    return {
        'x': nrm((BATCH, SEQ, D), 1.0),
        'c': nrm((BATCH, D), 1.0),
        'ctx': nrm((BATCH, CTX_LEN, D), 1.0),
        'c_ctx': nrm((D,), 1.0),
        'w_ada': nrm((DEPTH, D, N_MOD * D), D ** -0.5),
        'b_ada': nrm((DEPTH, N_MOD * D), 0.02),
        'norm_g': 1.0 + nrm((DEPTH, 6, D), 0.05),
        'ffn1_gate': nrm((DEPTH, D, D_FF), D ** -0.5),
        'ffn1_up': nrm((DEPTH, D, D_FF), D ** -0.5),
        'ffn1_down': nrm((DEPTH, D_FF, D), D_FF ** -0.5),
        'w_in': nrm((DEPTH, D, IN_COLS), D ** -0.5),
        'sgu_g': 1.0 + nrm((DEPTH, SGU_WIDTH), 0.05),
        'sgu_w': nrm((DEPTH, SGU_GROUPS, CHUNK, CHUNK), CHUNK ** -0.5),
        'sgu_b': 1.0 + nrm((DEPTH, SGU_GROUPS, CHUNK), 0.05),
        'attn_sink': nrm((DEPTH, N_Q_HEADS), 0.5),
        'hy_conv_w': nrm((DEPTH, HY_SHORT, (HY_ORDER + 1) * HY_WIDTH), HY_SHORT ** -0.5),
        'hy_conv_b': nrm((DEPTH, (HY_ORDER + 1) * HY_WIDTH), 0.02),
        'hy_f_w1': nrm((DEPTH, HY_EMB, HY_HIDDEN), HY_EMB ** -0.5),
        'hy_f_b1': nrm((DEPTH, HY_HIDDEN), 0.1),
        'hy_f_w2': nrm((DEPTH, HY_HIDDEN, HY_HIDDEN), HY_HIDDEN ** -0.5),
        'hy_f_b2': nrm((DEPTH, HY_HIDDEN), 0.1),
        'hy_f_w3': nrm((DEPTH, HY_HIDDEN, HY_ORDER * 2 * HY_WIDTH), HY_HIDDEN ** -0.5),
        'hy_freq': 1.0 + nrm((DEPTH, HY_HIDDEN), 0.05),
        'hy_skip': nrm((DEPTH, HY_ORDER, HY_WIDTH), 0.5),
        'w_out': nrm((DEPTH, D_MIX, D), D_MIX ** -0.5),
        'ffn2_gate': nrm((DEPTH, D, D_FF), D ** -0.5),
        'ffn2_up': nrm((DEPTH, D, D_FF), D ** -0.5),
        'ffn2_down': nrm((DEPTH, D_FF, D), D_FF ** -0.5),
    }


def reference(x, c, ctx, c_ctx, w_ada, b_ada, norm_g, ffn1_gate, ffn1_up, ffn1_down, w_in,
              sgu_g, sgu_w, sgu_b, attn_sink, hy_conv_w, hy_conv_b, hy_f_w1, hy_f_b1, hy_f_w2,
              hy_f_b2, hy_f_w3, hy_freq, hy_skip, w_out, ffn2_gate, ffn2_up, ffn2_down):
    B, L, D = x.shape
    rope = axial_rope(L)
    cx = ctx
    for i in range(DEPTH):
        last = i == DEPTH - 1
        g = norm_g[i]
        m_lat = (jax.nn.silu(c) @ w_ada[i] + b_ada[i]).reshape(B, N_MOD, D)
        m_ctx = (jax.nn.silu(c_ctx)[None] @ w_ada[i] + b_ada[i]).reshape(1, N_MOD, D)
        ffn1 = functools.partial(swiglu, w_gate=ffn1_gate[i], w_up=ffn1_up[i], w_down=ffn1_down[i])
        ffn2 = functools.partial(swiglu, w_gate=ffn2_gate[i], w_up=ffn2_up[i], w_down=ffn2_down[i])
        mixer_params = (sgu_g[i], sgu_w[i], sgu_b[i], attn_sink[i], hy_conv_w[i], hy_conv_b[i],
                        hy_f_w1[i], hy_f_b1[i], hy_f_w2[i], hy_f_b2[i], hy_f_w3[i], hy_freq[i], hy_skip[i])

        x = sub_post(x, ffn1(sub_pre(x, m_lat, 0, g[0])), m_lat, 0, g[1], 0.5)
        cx = sub_post(cx, ffn1(sub_pre(cx, m_ctx, 0, g[0])), m_ctx, 0, g[1], 0.5)

        h = sub_pre(x, m_lat, 1, g[2])
        hc = sub_pre(cx, m_ctx, 1, g[2])
        if last:
            z_c = None
            kv_c = hc @ w_in[i][:, Q_END:V_END]
        else:
            z_c = hc @ w_in[i]
            kv_c = z_c[..., Q_END:V_END]
        Lc = kv_c.shape[1]
        k_c = kv_c[..., :KV_WIDTH].reshape(B, Lc, N_KV_HEADS, HEAD_DIM)
        v_c = kv_c[..., KV_WIDTH:].reshape(B, Lc, N_KV_HEADS, HEAD_DIM)
        y = head_groups(h @ w_in[i], k_c, v_c, rope, *mixer_params) @ w_out[i]
        x = sub_post(x, y, m_lat, 1, g[3], 1.0)

        x = sub_post(x, ffn2(sub_pre(x, m_lat, 2, g[4])), m_lat, 2, g[5], 0.5)

        if not last:
            yc = head_groups(z_c, k_c, v_c, None, *mixer_params) @ w_out[i]
            cx = sub_post(cx, yc, m_ctx, 1, g[3], 1.0)
            cx = sub_post(cx, ffn2(sub_pre(cx, m_ctx, 2, g[4])), m_ctx, 2, g[5], 0.5)
    return x
```

```python
import functools
import math

import jax
import jax.numpy as jnp
from jax import lax
from jax.experimental import pallas as pl
from jax.experimental.pallas import tpu as pltpu

D_MODEL = 2048
BATCH = 2
SEQ = 4096
DEPTH = 4
GRID_W = 64
CTX_LEN = 256
HEAD_DIM = 128
NORM_EPS = 1e-6
N_MOD = 9
D_FF = 5632
SGU_GROUPS = 4
SGU_WIDTH = SGU_GROUPS * HEAD_DIM
CHUNK = 128
N_Q_HEADS = 8
N_KV_HEADS = 2
Q_GROUP = N_Q_HEADS // N_KV_HEADS
ATTN_WIDTH = N_Q_HEADS * HEAD_DIM
KV_WIDTH = N_KV_HEADS * HEAD_DIM
WINDOW = 128
BLOCK = 128
ROPE_BASE = 10000.0
NEG_INF = -1e30
HY_GROUPS = 4
HY_WIDTH = HY_GROUPS * HEAD_DIM
HY_ORDER = 2
HY_SHORT = 3
HY_BANDS = 16
HY_EMB = 1 + 2 * HY_BANDS
HY_HIDDEN = 64
HY_TARGET = 1e-2
HY_FAST_DECAY = 0.3
HY_SLOW_DECAY = 1.5
D_MIX = SGU_WIDTH + ATTN_WIDTH + HY_WIDTH
A_END = 2 * SGU_WIDTH
Q_END = A_END + ATTN_WIDTH
K_END = Q_END + KV_WIDTH
V_END = K_END + KV_WIDTH
IN_COLS = V_END + (HY_ORDER + 1) * HY_WIDTH

N_LAT = BATCH * SEQ
N_CTX = BATCH * CTX_LEN
N_TOK = N_LAT + N_CTX
MOD_ROWS = 8

TM = 512
TF = 512
TN_IN = 1024
TN_ADA = 1024
HY_COLS = 256
HY_KF = 256
VMEM_LIMIT = 56 * 1024 * 1024

F32 = jnp.float32
BF16 = jnp.bfloat16


def _params(*sem):
    return pltpu.CompilerParams(dimension_semantics=sem, vmem_limit_bytes=VMEM_LIMIT)


def _mod_row(i):
    return jnp.minimum(i // (SEQ // TM), BATCH)


def _silu(a):
    return a / (1.0 + jnp.exp(-a))


def _rms(x, g):
    return x * lax.rsqrt(jnp.mean(x * x, axis=-1, keepdims=True) + NORM_EPS) * g


def _ada_kernel(c_ref, w_ref, b_ref, o_ref):
    s = _silu(c_ref[...]).astype(BF16)
    o_ref[...] = jnp.dot(s, w_ref[...].astype(BF16), preferred_element_type=F32) + b_ref[...]


def _ada_mods(cc, w_ada, b_ada):
    n_out = N_MOD * D_MODEL
    out = pl.pallas_call(
        _ada_kernel,
        grid=(DEPTH, n_out // TN_ADA),
        in_specs=[
            pl.BlockSpec((MOD_ROWS, D_MODEL), lambda l, j: (0, 0)),
            pl.BlockSpec((None, D_MODEL, TN_ADA), lambda l, j: (l, 0, j)),
            pl.BlockSpec((None, 1, TN_ADA), lambda l, j: (l, 0, j)),
        ],
        out_specs=pl.BlockSpec((None, MOD_ROWS, TN_ADA), lambda l, j: (l, 0, j)),
        out_shape=jax.ShapeDtypeStruct((DEPTH, MOD_ROWS, n_out), F32),
        compiler_params=_params("arbitrary", "arbitrary"),
        name="ada_mods",
    )(cc, w_ada, b_ada.reshape(DEPTH, 1, n_out))
    return out.reshape(DEPTH, MOD_ROWS, N_MOD, D_MODEL)


def _ffn_kernel(x_ref, mod_ref, g_ref, wg_ref, wu_ref, wd_ref, o_ref, h_ref, acc_ref, *, k, nf):
    f = pl.program_id(1)

    @pl.when(f == 0)
    def _():
        y = _rms(x_ref[...], g_ref[2 * k:2 * k + 1, :])
        h = y * (1.0 + mod_ref[3 * k:3 * k + 1, :]) + mod_ref[3 * k + 1:3 * k + 2, :]
        h_ref[...] = h.astype(BF16)
        acc_ref[...] = jnp.zeros_like(acc_ref)

    h = h_ref[...]
    a = jnp.dot(h, wg_ref[...], preferred_element_type=F32)
    b = jnp.dot(h, wu_ref[...], preferred_element_type=F32)
    p = (_silu(a) * b).astype(BF16)
    acc_ref[...] += jnp.dot(p, wd_ref[...], preferred_element_type=F32)

    @pl.when(f == nf - 1)
    def _():
        yn = _rms(acc_ref[...], g_ref[2 * k + 1:2 * k + 2, :])
        o_ref[...] = x_ref[...] + (0.5 * mod_ref[3 * k + 2:3 * k + 3, :]) * yn


def _ffn(x, mods, norm_g, wg, wu, wd, *, layer, k, n_tiles):
    nf = D_FF // TF
    return pl.pallas_call(
        functools.partial(_ffn_kernel, k=k, nf=nf),
        grid=(n_tiles, nf),
        in_specs=[
            pl.BlockSpec((TM, D_MODEL), lambda i, f: (i, 0)),
            pl.BlockSpec((None, None, N_MOD, D_MODEL), lambda i, f: (layer, _mod_row(i), 0, 0)),
            pl.BlockSpec((None, 6, D_MODEL), lambda i, f: (layer, 0, 0)),
            pl.BlockSpec((None, D_MODEL, TF), lambda i, f: (layer, 0, f)),
            pl.BlockSpec((None, D_MODEL, TF), lambda i, f: (layer, 0, f)),
            pl.BlockSpec((None, TF, D_MODEL), lambda i, f: (layer, f, 0)),
        ],
        out_specs=pl.BlockSpec((TM, D_MODEL), lambda i, f: (i, 0)),
        out_shape=jax.ShapeDtypeStruct((n_tiles * TM, D_MODEL), F32),
        scratch_shapes=[pltpu.VMEM((TM, D_MODEL), BF16), pltpu.VMEM((TM, D_MODEL), F32)],
        compiler_params=_params("arbitrary", "arbitrary"),
        name=f"ffn{k}",
    )(x, mods, norm_g, wg, wu, wd)


def _rope(t, cos, sa, sb):
    return t * cos + pltpu.roll(t, 3 * HEAD_DIM // 4, 1) * sa + pltpu.roll(t, HEAD_DIM // 4, 1) * sb


def _inproj_kernel(x_ref, mod_ref, g_ref, w_ref, cos_ref, sa_ref, sb_ref, o_ref, h_ref):
    n = pl.program_id(1)

    @pl.when(n == 0)
    def _():
        y = _rms(x_ref[...], g_ref[2:3, :])
        h = y * (1.0 + mod_ref[3:4, :]) + mod_ref[4:5, :]
        h_ref[...] = h.astype(BF16)

    z = jnp.dot(h_ref[...], w_ref[...], preferred_element_type=F32)
    q_chunk = A_END // TN_IN
    k_chunk = Q_END // TN_IN

    def rope_heads(n_heads):
        cos, sa, sb = cos_ref[...], sa_ref[...], sb_ref[...]
        for hd in range(n_heads):
            sl = slice(hd * HEAD_DIM, (hd + 1) * HEAD_DIM)
            o_ref[:, sl] = _rope(z[:, sl], cos, sa, sb)

    @pl.when(n == q_chunk)
    def _():
        rope_heads(N_Q_HEADS)

    @pl.when(n == k_chunk)
    def _():
        rope_heads(N_KV_HEADS)
        o_ref[:, KV_WIDTH:] = z[:, KV_WIDTH:]

    @pl.when((n != q_chunk) & (n != k_chunk))
    def _():
        o_ref[...] = z


def _inproj(x, mods, norm_g, w_in, cos, sa, sb, *, layer):
    n_tiles = N_TOK // TM
    tab = pl.BlockSpec((TM, HEAD_DIM), lambda i, n: (i, 0))
    return pl.pallas_call(
        _inproj_kernel,
        grid=(n_tiles, IN_COLS // TN_IN),
        in_specs=[
            pl.BlockSpec((TM, D_MODEL), lambda i, n: (i, 0)),
            pl.BlockSpec((None, None, N_MOD, D_MODEL), lambda i, n: (layer, _mod_row(i), 0, 0)),
            pl.BlockSpec((None, 6, D_MODEL), lambda i, n: (layer, 0, 0)),
            pl.BlockSpec((None, D_MODEL, TN_IN), lambda i, n: (layer, 0, n)),
            tab, tab, tab,
        ],
        out_specs=pl.BlockSpec((TM, TN_IN), lambda i, n: (i, n)),
        out_shape=jax.ShapeDtypeStruct((N_TOK, IN_COLS), F32),
        scratch_shapes=[pltpu.VMEM((TM, D_MODEL), BF16)],
        compiler_params=_params("arbitrary", "arbitrary"),
        name="inproj",
    )(x, mods, norm_g, w_in, cos, sa, sb)


def _sgu_kernel(z_ref, g_ref, w_ref, b_ref, o_ref, *, rows):
    inv_sqrt2 = 1.0 / math.sqrt(2.0)
    for c in range(rows // CHUNK):
        rs = slice(c * CHUNK, (c + 1) * CHUNK)
        a = z_ref[rs, :]
        a = a * (lax.erf(a * inv_sqrt2) + 1.0) * 0.5
        u, v = a[:, :SGU_WIDTH], a[:, SGU_WIDTH:]
        mu = jnp.mean(v, axis=-1, keepdims=True)
        vc = v - mu
        var = jnp.mean(vc * vc, axis=-1, keepdims=True)
        v = (vc * lax.rsqrt(var + NORM_EPS) * g_ref[...]).astype(BF16)
        for g in range(SGU_GROUPS):
            cs = slice(g * HEAD_DIM, (g + 1) * HEAD_DIM)
            mixed = jnp.dot(w_ref[g].astype(BF16), v[:, cs], preferred_element_type=F32) + b_ref[:, g:g + 1]
            o_ref[rs, cs] = (u[:, cs] * mixed).astype(o_ref.dtype)


def _sgu(z, sgu_g, sgu_w, sgu_b, *, layer, n_rows):
    rows = TM
    return pl.pallas_call(
        functools.partial(_sgu_kernel, rows=rows),
        grid=(n_rows // rows,),
        in_specs=[
            pl.BlockSpec((rows, A_END), lambda i: (i, 0)),
            pl.BlockSpec((None, 1, SGU_WIDTH), lambda i: (layer, 0, 0)),
            pl.BlockSpec((None, SGU_GROUPS, CHUNK, CHUNK), lambda i: (layer, 0, 0, 0)),
            pl.BlockSpec((None, CHUNK, SGU_GROUPS), lambda i: (layer, 0, 0)),
        ],
        out_specs=pl.BlockSpec((rows, SGU_WIDTH), lambda i: (i, 0)),
        out_shape=jax.ShapeDtypeStruct((n_rows, SGU_WIDTH), BF16),
        compiler_params=_params("arbitrary"),
        name="sgu",
    )(z, sgu_g.reshape(DEPTH, 1, SGU_WIDTH), sgu_w, jnp.swapaxes(sgu_b, 1, 2))


def _attn_kernel(sink_ref, q_ref, kp_ref, kc_ref, kn_ref, vp_ref, vc_ref, vn_ref, kx_ref, vx_ref,
                 o_ref, *, layer, nb, n_lat_blocks):
    s_id = pl.program_id(0)
    is_lat = s_id < n_lat_blocks
    n = s_id % nb
    n_keys = CTX_LEN + 3 * BLOCK
    row = lax.broadcasted_iota(jnp.int32, (BLOCK, n_keys), 0)
    col = lax.broadcasted_iota(jnp.int32, (BLOCK, n_keys), 1)
    rel = col - (CTX_LEN + BLOCK)
    lo = jnp.where(is_lat, -n * BLOCK, 4 * BLOCK)
    hi = jnp.where(is_lat, (nb - n) * BLOCK, -4 * BLOCK)
    dist = rel - row
    valid = (col < CTX_LEN) | ((dist >= -WINDOW) & (dist <= WINDOW) & (rel >= lo) & (rel < hi))
    scale = HEAD_DIM ** -0.5
    nt = (((1,), (1,)), ((), ()))
    for kh in range(N_KV_HEADS):
        ks = slice(kh * HEAD_DIM, (kh + 1) * HEAD_DIM)
        keys = jnp.concatenate([kx_ref[:, ks], kp_ref[:, ks], kc_ref[:, ks], kn_ref[:, ks]], axis=0).astype(BF16)
        vals = jnp.concatenate([vx_ref[:, ks], vp_ref[:, ks], vc_ref[:, ks], vn_ref[:, ks]], axis=0).astype(BF16)
        for g in range(Q_GROUP):
            hd = kh * Q_GROUP + g
            hs = slice(hd * HEAD_DIM, (hd + 1) * HEAD_DIM)
            s = lax.dot_general(q_ref[:, hs].astype(BF16), keys, nt, preferred_element_type=F32) * scale
            s = jnp.where(valid, s, NEG_INF)
            sink = sink_ref[layer, hd]
            m = jnp.maximum(jnp.max(s, axis=-1, keepdims=True), sink)
            p = jnp.exp(s - m)
            denom = jnp.sum(p, axis=-1, keepdims=True) + jnp.exp(sink - m)
            probs = (p / denom).astype(BF16)
            o_ref[:, hs] = jnp.dot(probs, vals, preferred_element_type=F32).astype(o_ref.dtype)


def _attention(z, attn_sink, *, layer, n_rows):
    nb = SEQ // BLOCK
    n_lat_blocks = N_LAT // BLOCK
    n_blocks = n_rows // BLOCK
    ctx_blocks = CTX_LEN // BLOCK
    q_col = A_END // ATTN_WIDTH
    k_col = Q_END // KV_WIDTH
    v_col = K_END // KV_WIDTH

    def lat(s):
        return s < n_lat_blocks

    def band(off, col):
        def index(s):
            b = s // nb
            nbr = jnp.clip(s % nb + off, 0, nb - 1) + b * nb
            return (jnp.where(lat(s), nbr, s), col)
        return pl.BlockSpec((BLOCK, KV_WIDTH), index)

    def ctx(col):
        def index(s):
            b = jnp.where(lat(s), s // nb, (s - n_lat_blocks) // ctx_blocks)
            return (N_LAT // CTX_LEN + b, col)
        return pl.BlockSpec((CTX_LEN, KV_WIDTH), index)

    return pl.pallas_call(
        functools.partial(_attn_kernel, layer=layer, nb=nb, n_lat_blocks=n_lat_blocks),
        grid=(n_blocks,),
        in_specs=[
            pl.BlockSpec(memory_space=pltpu.SMEM),
            pl.BlockSpec((BLOCK, ATTN_WIDTH), lambda s: (s, q_col)),
            band(-1, k_col), band(0, k_col), band(1, k_col),
            band(-1, v_col), band(0, v_col), band(1, v_col),
            ctx(k_col), ctx(v_col),
        ],
        out_specs=pl.BlockSpec((BLOCK, ATTN_WIDTH), lambda s: (s, 0)),
        out_shape=jax.ShapeDtypeStruct((n_rows, ATTN_WIDTH), BF16),
        compiler_params=_params("arbitrary"),
        name="attn",
    )(attn_sink, z, z, z, z, z, z, z, z, z)


def _short_conv(ref, w_ref, b_ref, r0, rows, seq):
    halo = 8
    lo = max(r0 - halo, 0)
    hi = min(r0 + rows + halo, seq)
    ext = ref[lo:hi, :]
    n_ext = hi - lo
    off = r0 - lo
    up = pltpu.roll(ext, 1, 0)[off:off + rows]
    dn = pltpu.roll(ext, n_ext - 1, 0)[off:off + rows]
    t = lax.broadcasted_iota(jnp.int32, (rows, 1), 0) + r0
    if r0 == 0:
        up = jnp.where(t == 0, 0.0, up)
    if r0 + rows == seq:
        dn = jnp.where(t == seq - 1, 0.0, dn)
    return up * w_ref[0:1, :] + ext[off:off + rows] * w_ref[1:2, :] + dn * w_ref[2:3, :] + b_ref[...]


def _hyena_kernel(u_ref, m_ref, wu_ref, bu_ref, wm_ref, bm_ref, skip_ref, f_ref, g_ref,
                  sa_ref, sb_ref, sd_ref, o_ref, ub_ref, acc_ref, *, seq, kf, conv_u, row_chunk):
    c = pl.program_id(2)
    n_rc = seq // row_chunk

    def u_rows(r0):
        if conv_u:
            return _short_conv(u_ref, wu_ref, bu_ref, r0, row_chunk, seq)
        return u_ref[r0:r0 + row_chunk, :]

    @pl.when(c == 0)
    def _():
        for r in range(n_rc):
            ub_ref[r * row_chunk:(r + 1) * row_chunk, :] = u_rows(r * row_chunk).astype(BF16)

    spec_u = jnp.dot(f_ref[...], ub_ref[...], preferred_element_type=F32)
    ur, ui = spec_u[:kf], spec_u[kf:]
    sa, sb, sd = sa_ref[...], sb_ref[...], sd_ref[...]
    prod = jnp.concatenate([ur * sa - ui * sb, ur * sb + ui * sd], axis=0).astype(BF16)
    part = jnp.dot(g_ref[...], prod, preferred_element_type=F32)

    @pl.when(c == 0)
    def _():
        acc_ref[...] = part

    @pl.when(c != 0)
    def _():
        acc_ref[...] += part

    @pl.when(c == pl.num_programs(2) - 1)
    def _():
        for r in range(n_rc):
            r0 = r * row_chunk
            rs = slice(r0, r0 + row_chunk)
            y = acc_ref[rs, :] + u_rows(r0) * skip_ref[...]
            mult = _short_conv(m_ref, wm_ref, bm_ref, r0, row_chunk, seq)
            o_ref[rs, :] = (mult * y).astype(o_ref.dtype)


def _hyena_conv(u_src, u_rb0, u_cb, m_src, m_rb0, m_cb, conv_w, conv_b, skip, fmat, gmat, tabs,
                *, layer, order, seq, conv_u, out_rows, out_rb0, out_dtype, alias_out=None):
    kf = min(HY_KF, seq)
    n_ch = seq // kf
    n_cols = HY_WIDTH // HY_COLS
    sa, sb, sd = tabs
    hy0 = V_END // HY_COLS
    wcol_u = (u_cb - hy0) if conv_u else 0
    wcol_m = m_cb - hy0
    in_specs = [
        pl.BlockSpec((seq, HY_COLS), lambda b, j, c: (u_rb0 + b, u_cb + j)),
        pl.BlockSpec((seq, HY_COLS), lambda b, j, c: (m_rb0 + b, m_cb + j)),
        pl.BlockSpec((None, HY_SHORT, HY_COLS), lambda b, j, c: (layer, 0, wcol_u + j)),
        pl.BlockSpec((None, 1, HY_COLS), lambda b, j, c: (layer, 0, wcol_u + j)),
        pl.BlockSpec((None, HY_SHORT, HY_COLS), lambda b, j, c: (layer, 0, wcol_m + j)),
        pl.BlockSpec((None, 1, HY_COLS), lambda b, j, c: (layer, 0, wcol_m + j)),
        pl.BlockSpec((None, 1, HY_COLS), lambda b, j, c: (layer * HY_ORDER + order, 0, j)),
        pl.BlockSpec((None, 2 * kf, seq), lambda b, j, c: (c, 0, 0)),
        pl.BlockSpec((None, seq, 2 * kf), lambda b, j, c: (c, 0, 0)),
        pl.BlockSpec((kf, HY_COLS), lambda b, j, c: (c, j)),
        pl.BlockSpec((kf, HY_COLS), lambda b, j, c: (c, j)),
        pl.BlockSpec((kf, HY_COLS), lambda b, j, c: (c, j)),
    ]
    args = [u_src, m_src, conv_w, conv_b, conv_w, conv_b, skip, fmat, gmat, sa, sb, sd]
    aliases = {}
    if alias_out is not None:
        in_specs.append(pl.BlockSpec(memory_space=pl.ANY))
        args.append(alias_out)
        aliases = {len(args) - 1: 0}

    def body(*refs):
        if alias_out is not None:
            refs = refs[:12] + refs[13:]
        _hyena_kernel(*refs, seq=seq, kf=kf, conv_u=conv_u, row_chunk=min(512, seq))

    return pl.pallas_call(
        body,
        grid=(BATCH, n_cols, n_ch),
        in_specs=in_specs,
        out_specs=pl.BlockSpec((seq, HY_COLS), lambda b, j, c: (out_rb0 + b, j)),
        out_shape=jax.ShapeDtypeStruct((out_rows, HY_WIDTH), out_dtype),
        scratch_shapes=[pltpu.VMEM((seq, HY_COLS), BF16), pltpu.VMEM((seq, HY_COLS), F32)],
        input_output_aliases=aliases,
        compiler_params=_params("arbitrary", "arbitrary", "arbitrary"),
        name=f"hyena_o{order}_l{seq}",
    )(*args)


def _dft_matrices(seq, kf):
    n_fft = 2 * seq
    r = 64
    k = jnp.arange(seq, dtype=jnp.int32)[:, None]
    nh = jnp.arange(seq // r, dtype=jnp.int32)[None, :]
    nl = jnp.arange(r, dtype=jnp.int32)[None, :]
    a1 = (2.0 * math.pi / n_fft) * ((k * nh * r) % n_fft).astype(F32)
    a2 = (2.0 * math.pi / n_fft) * ((k * nl) % n_fft).astype(F32)
    c1, s1 = jnp.cos(a1)[:, :, None], jnp.sin(a1)[:, :, None]
    c2, s2 = jnp.cos(a2)[:, None, :], jnp.sin(a2)[:, None, :]
    cosm = (c1 * c2 - s1 * s2).reshape(seq, seq)
    sinm = (s1 * c2 + c1 * s2).reshape(seq, seq)
    nyq = jnp.where(jnp.arange(seq) % 2 == 0, 1.0, -1.0).astype(F32)[None, :]
    msin = jnp.where(k == 0, nyq, -sinm)
    n_ch = seq // kf
    fmat = jnp.concatenate([cosm.reshape(n_ch, kf, seq), msin.reshape(n_ch, kf, seq)], axis=1).astype(BF16)
    gmat = jnp.swapaxes(fmat, 1, 2)
    return fmat, gmat


def _filter_tables(seq, w1, b1, w2, b2, w3, freq):
    t = jnp.linspace(0.0, 1.0, seq, dtype=F32)[:, None]
    w = (2.0 * math.pi / seq) * jnp.arange(seq, dtype=F32)[:, None]
    bands = jnp.linspace(1e-4, HY_BANDS - 1, HY_BANDS, dtype=F32)[None, :]
    feat = jnp.concatenate([t, jnp.cos(bands * w), -jnp.sin(bands * w)], axis=-1)
    hp = lax.Precision.HIGHEST
    hd = jnp.sin(freq * (jnp.dot(feat, w1, precision=hp) + b1))
    hd = jnp.sin(freq * (jnp.dot(hd, w2, precision=hp) + b2))
    h = jnp.dot(hd, w3, precision=hp).reshape(seq, HY_ORDER, 2, HY_WIDTH)
    deltas = jnp.abs(jnp.linspace(math.log(HY_TARGET) / HY_FAST_DECAY,
                                  math.log(HY_TARGET) / HY_SLOW_DECAY, HY_WIDTH, dtype=F32))
    h = h * jnp.exp(-t * deltas)[:, None, None, :]
    h = h / (jnp.sum(jnp.abs(h), axis=(0, 2), keepdims=True) + 1e-6)
    kf_, kb_ = h[:, :, 0], h[:, :, 1]
    k_circ = jnp.concatenate([kf_[:1] + kb_[:1], kf_[1:], jnp.zeros_like(kf_[:1]), kb_[:0:-1]], axis=0)
    spec = jnp.fft.rfft(k_circ, axis=0)
    n_fft = 2 * seq
    scale = jnp.where(jnp.arange(seq) == 0, 1.0 / n_fft, 2.0 / n_fft).astype(F32)[:, None, None]
    sr = jnp.real(spec)
    a = sr[:seq] * scale
    b = jnp.imag(spec)[:seq] * scale
    dt = jnp.concatenate([sr[seq:seq + 1], sr[1:seq]], axis=0) * scale
    return [(a[:, o], b[:, o], dt[:, o]) for o in range(HY_ORDER)]


def _outproj_kernel(a_ref, b_ref, c_ref, w_ref, x_ref, mod_ref, g_ref, o_ref):
    y = jnp.dot(a_ref[...], w_ref[:SGU_WIDTH, :], preferred_element_type=F32)
    y += jnp.dot(b_ref[...], w_ref[SGU_WIDTH:SGU_WIDTH + ATTN_WIDTH, :], preferred_element_type=F32)
    y += jnp.dot(c_ref[...], w_ref[SGU_WIDTH + ATTN_WIDTH:, :], preferred_element_type=F32)
    o_ref[...] = x_ref[...] + mod_ref[5:6, :] * _rms(y, g_ref[3:4, :])


def _outproj(a, b, c, w_out, x, mods, norm_g, *, layer, n_tiles):
    return pl.pallas_call(
        _outproj_kernel,
        grid=(n_tiles,),
        in_specs=[
            pl.BlockSpec((TM, SGU_WIDTH), lambda i: (i, 0)),
            pl.BlockSpec((TM, ATTN_WIDTH), lambda i: (i, 0)),
            pl.BlockSpec((TM, HY_WIDTH), lambda i: (i, 0)),
            pl.BlockSpec((None, D_MIX, D_MODEL), lambda i: (layer, 0, 0)),
            pl.BlockSpec((TM, D_MODEL), lambda i: (i, 0)),
            pl.BlockSpec((None, None, N_MOD, D_MODEL), lambda i: (layer, _mod_row(i), 0, 0)),
            pl.BlockSpec((None, 6, D_MODEL), lambda i: (layer, 0, 0)),
        ],
        out_specs=pl.BlockSpec((TM, D_MODEL), lambda i: (i, 0)),
        out_shape=jax.ShapeDtypeStruct((n_tiles * TM, D_MODEL), F32),
        compiler_params=_params("arbitrary"),
        name="outproj",
    )(a, b, c, w_out, x, mods, norm_g)


def _rope_tables():
    rows = SEQ // GRID_W
    row = jnp.broadcast_to(jnp.arange(rows)[:, None], (rows, GRID_W)).reshape(SEQ).astype(F32)
    col = jnp.broadcast_to(jnp.arange(GRID_W)[None, :], (rows, GRID_W)).reshape(SEQ).astype(F32)
    half = HEAD_DIM // 2
    inv = ROPE_BASE ** (-jnp.arange(0, half, 2, dtype=F32) / half)
    ang_r = row[:, None] * inv
    ang_c = col[:, None] * inv
    ang = jnp.concatenate([ang_r, ang_r, ang_c, ang_c], axis=-1)
    cos, sin = jnp.cos(ang), jnp.sin(ang)
    first = (jnp.arange(HEAD_DIM) % half) < (half // 2)
    sa = jnp.where(first, -sin, 0.0)
    sb = jnp.where(first, 0.0, sin)

    def stack(lat, ctx_value):
        return jnp.concatenate([lat] * BATCH + [jnp.full((N_CTX, HEAD_DIM), ctx_value, F32)], axis=0)

    return stack(cos, 1.0), stack(sa, 0.0), stack(sb, 0.0)


def kernel(x, c, ctx, c_ctx, w_ada, b_ada, norm_g, ffn1_gate, ffn1_up, ffn1_down, w_in,
           sgu_g, sgu_w, sgu_b, attn_sink, hy_conv_w, hy_conv_b, hy_f_w1, hy_f_b1, hy_f_w2,
           hy_f_b2, hy_f_w3, hy_freq, hy_skip, w_out, ffn2_gate, ffn2_up, ffn2_down):
    xs = jnp.concatenate([x.reshape(N_LAT, D_MODEL), ctx.reshape(N_CTX, D_MODEL)], axis=0)
    cc = jnp.concatenate([c, c_ctx[None], jnp.zeros((MOD_ROWS - BATCH - 1, D_MODEL), F32)], axis=0)
    mods = _ada_mods(cc, w_ada, b_ada)

    w1g, w1u, w1d = ffn1_gate.astype(BF16), ffn1_up.astype(BF16), ffn1_down.astype(BF16)
    w2g, w2u, w2d = ffn2_gate.astype(BF16), ffn2_up.astype(BF16), ffn2_down.astype(BF16)
    w_in_b, w_out_b = w_in.astype(BF16), w_out.astype(BF16)
    cos, sa, sb = _rope_tables()
    conv_b = hy_conv_b.reshape(DEPTH, 1, (HY_ORDER + 1) * HY_WIDTH)
    skip = hy_skip.reshape(DEPTH * HY_ORDER, 1, HY_WIDTH)
    f_lat, g_lat = _dft_matrices(SEQ, min(HY_KF, SEQ))
    f_ctx, g_ctx = _dft_matrices(CTX_LEN, min(HY_KF, CTX_LEN))

    all_tiles = N_TOK // TM
    lat_tiles = N_LAT // TM
    hy_v, hy_x1, hy_x2 = [(V_END + o * HY_WIDTH) // HY_COLS for o in range(HY_ORDER + 1)]

    for i in range(DEPTH):
        last = i == DEPTH - 1
        n_tiles = lat_tiles if last else all_tiles
        n_rows = n_tiles * TM

        xs = _ffn(xs, mods, norm_g, w1g, w1u, w1d, layer=i, k=0, n_tiles=all_tiles)
        z = _inproj(xs, mods, norm_g, w_in_b, cos, sa, sb, layer=i)

        a_out = _sgu(z, sgu_g, sgu_w, sgu_b, layer=i, n_rows=n_rows)
        b_out = _attention(z, attn_sink, layer=i, n_rows=n_rows)

        filt = (hy_f_w1[i], hy_f_b1[i], hy_f_w2[i], hy_f_b2[i], hy_f_w3[i], hy_freq[i])
        tabs = _filter_tables(SEQ, *filt)
        s_lat = _hyena_conv(z, 0, hy_v, z, 0, hy_x1, hy_conv_w, conv_b, skip, f_lat, g_lat, tabs[0],
                            layer=i, order=0, seq=SEQ, conv_u=True, out_rows=N_LAT, out_rb0=0,
                            out_dtype=F32)
        c_out = _hyena_conv(s_lat, 0, 0, z, 0, hy_x2, hy_conv_w, conv_b, skip, f_lat, g_lat, tabs[1],
                            layer=i, order=1, seq=SEQ, conv_u=False, out_rows=n_rows, out_rb0=0,
                            out_dtype=BF16)
        if not last:
            tabs_c = _filter_tables(CTX_LEN, *filt)
            rb = N_LAT // CTX_LEN
            s_ctx = _hyena_conv(z, rb, hy_v, z, rb, hy_x1, hy_conv_w, conv_b, skip, f_ctx, g_ctx,
                                tabs_c[0], layer=i, order=0, seq=CTX_LEN, conv_u=True,
                                out_rows=N_CTX, out_rb0=0, out_dtype=F32)
            c_out = _hyena_conv(s_ctx, 0, 0, z, rb, hy_x2, hy_conv_w, conv_b, skip, f_ctx, g_ctx,
                                tabs_c[1], layer=i, order=1, seq=CTX_LEN, conv_u=False,
                                out_rows=n_rows, out_rb0=rb, out_dtype=BF16, alias_out=c_out)

        xs = _outproj(a_out, b_out, c_out, w_out_b, xs, mods, norm_g, layer=i, n_tiles=n_tiles)
        xs = _ffn(xs, mods, norm_g, w2g, w2u, w2d, layer=i, k=2, n_tiles=n_tiles)

    return xs[:N_LAT].reshape(BATCH, SEQ, D_MODEL)
```

```python
import functools
import math

import jax
import jax.numpy as jnp
from jax import lax
from jax.experimental import pallas as pl
from jax.experimental.pallas import tpu as pltpu

D_MODEL = 2048
BATCH = 2
SEQ = 4096
DEPTH = 4
GRID_W = 64
CTX_LEN = 256
HEAD_DIM = 128
NORM_EPS = 1e-6
N_MOD = 9
D_FF = 5632
SGU_GROUPS = 4
SGU_WIDTH = SGU_GROUPS * HEAD_DIM
CHUNK = 128
N_Q_HEADS = 8
N_KV_HEADS = 2
Q_GROUP = N_Q_HEADS // N_KV_HEADS
ATTN_WIDTH = N_Q_HEADS * HEAD_DIM
KV_WIDTH = N_KV_HEADS * HEAD_DIM
WINDOW = 128
BLOCK = 128
ROPE_BASE = 10000.0
NEG_INF = -1e30
HY_GROUPS = 4
HY_WIDTH = HY_GROUPS * HEAD_DIM
HY_ORDER = 2
HY_SHORT = 3
HY_BANDS = 16
HY_EMB = 1 + 2 * HY_BANDS
HY_HIDDEN = 64
HY_TARGET = 1e-2
HY_FAST_DECAY = 0.3
HY_SLOW_DECAY = 1.5
D_MIX = SGU_WIDTH + ATTN_WIDTH + HY_WIDTH
A_END = 2 * SGU_WIDTH
Q_END = A_END + ATTN_WIDTH
K_END = Q_END + KV_WIDTH
V_END = K_END + KV_WIDTH
IN_COLS = V_END + (HY_ORDER + 1) * HY_WIDTH

N_LAT = BATCH * SEQ
N_CTX = BATCH * CTX_LEN
N_TOK = N_LAT + N_CTX
MOD_ROWS = 8

TM = 512
TF = 512
TN_IN = 1024
TN_ADA = 1024
HY_COLS = 256
HY_KF = 256
HY_ROWS = 512
HY_EMB_PAD = 128
VMEM_LIMIT = 56 * 1024 * 1024

F32 = jnp.float32
BF16 = jnp.bfloat16


def _params(*sem):
    return pltpu.CompilerParams(dimension_semantics=sem, vmem_limit_bytes=VMEM_LIMIT)


def _mod_row(i):
    return jnp.minimum(i // (SEQ // TM), BATCH)


def _silu(a):
    return a / (1.0 + jnp.exp(-a))


def _rms(x, g):
    return x * lax.rsqrt(jnp.mean(x * x, axis=-1, keepdims=True) + NORM_EPS) * g


def _ada_kernel(c_ref, w_ref, b_ref, o_ref):
    s = _silu(c_ref[...]).astype(BF16)
    o_ref[...] = jnp.dot(s, w_ref[...].astype(BF16), preferred_element_type=F32) + b_ref[...]


def _ada_mods(cc, w_ada, b_ada):
    n_out = N_MOD * D_MODEL
    out = pl.pallas_call(
        _ada_kernel,
        grid=(DEPTH, n_out // TN_ADA),
        in_specs=[
            pl.BlockSpec((MOD_ROWS, D_MODEL), lambda l, j: (0, 0)),
            pl.BlockSpec((None, D_MODEL, TN_ADA), lambda l, j: (l, 0, j)),
            pl.BlockSpec((None, 1, TN_ADA), lambda l, j: (l, 0, j)),
        ],
        out_specs=pl.BlockSpec((None, MOD_ROWS, TN_ADA), lambda l, j: (l, 0, j)),
        out_shape=jax.ShapeDtypeStruct((DEPTH, MOD_ROWS, n_out), F32),
        compiler_params=_params("arbitrary", "arbitrary"),
        name="ada_mods",
    )(cc, w_ada, b_ada.reshape(DEPTH, 1, n_out))
    return out.reshape(DEPTH, MOD_ROWS, N_MOD, D_MODEL)


def _ffn_kernel(x_ref, mod_ref, g_ref, wg_ref, wu_ref, wd_ref, o_ref, h_ref, acc_ref, *, k, nf):
    f = pl.program_id(1)

    @pl.when(f == 0)
    def _():
        y = _rms(x_ref[...], g_ref[2 * k:2 * k + 1, :])
        h = y * (1.0 + mod_ref[3 * k:3 * k + 1, :]) + mod_ref[3 * k + 1:3 * k + 2, :]
        h_ref[...] = h.astype(BF16)
        acc_ref[...] = jnp.zeros_like(acc_ref)

    h = h_ref[...]
    a = jnp.dot(h, wg_ref[...], preferred_element_type=F32)
    b = jnp.dot(h, wu_ref[...], preferred_element_type=F32)
    p = (_silu(a) * b).astype(BF16)
    acc_ref[...] += jnp.dot(p, wd_ref[...], preferred_element_type=F32)

    @pl.when(f == nf - 1)
    def _():
        yn = _rms(acc_ref[...], g_ref[2 * k + 1:2 * k + 2, :])
        o_ref[...] = x_ref[...] + (0.5 * mod_ref[3 * k + 2:3 * k + 3, :]) * yn


def _ffn(x, mods, norm_g, wg, wu, wd, *, layer, k, n_tiles):
    nf = D_FF // TF
    return pl.pallas_call(
        functools.partial(_ffn_kernel, k=k, nf=nf),
        grid=(n_tiles, nf),
        in_specs=[
            pl.BlockSpec((TM, D_MODEL), lambda i, f: (i, 0)),
            pl.BlockSpec((None, None, N_MOD, D_MODEL), lambda i, f: (layer, _mod_row(i), 0, 0)),
            pl.BlockSpec((None, 6, D_MODEL), lambda i, f: (layer, 0, 0)),
            pl.BlockSpec((None, D_MODEL, TF), lambda i, f: (layer, 0, f)),
            pl.BlockSpec((None, D_MODEL, TF), lambda i, f: (layer, 0, f)),
            pl.BlockSpec((None, TF, D_MODEL), lambda i, f: (layer, f, 0)),
        ],
        out_specs=pl.BlockSpec((TM, D_MODEL), lambda i, f: (i, 0)),
        out_shape=jax.ShapeDtypeStruct((n_tiles * TM, D_MODEL), F32),
        scratch_shapes=[pltpu.VMEM((TM, D_MODEL), BF16), pltpu.VMEM((TM, D_MODEL), F32)],
        compiler_params=_params("arbitrary", "arbitrary"),
        name=f"ffn{k}",
    )(x, mods, norm_g, wg, wu, wd)


def _rope(t, cos, sa, sb):
    return t * cos + pltpu.roll(t, 3 * HEAD_DIM // 4, 1) * sa + pltpu.roll(t, HEAD_DIM // 4, 1) * sb


def _inproj_kernel(x_ref, mod_ref, g_ref, w_ref, cos_ref, sa_ref, sb_ref, o_ref):
    y = _rms(x_ref[...], g_ref[2:3, :])
    h = (y * (1.0 + mod_ref[3:4, :]) + mod_ref[4:5, :]).astype(BF16)
    cos, sa, sb = cos_ref[...], sa_ref[...], sb_ref[...]
    n_rope = (K_END - A_END) // HEAD_DIM
    for n in range(IN_COLS // TN_IN):
        c0 = n * TN_IN
        z = jnp.dot(h, w_ref[:, c0:c0 + TN_IN], preferred_element_type=F32)
        for hd in range(TN_IN // HEAD_DIM):
            col = c0 + hd * HEAD_DIM
            zs = z[:, hd * HEAD_DIM:(hd + 1) * HEAD_DIM]
            if A_END <= col < A_END + n_rope * HEAD_DIM:
                zs = _rope(zs, cos, sa, sb)
            o_ref[:, col:col + HEAD_DIM] = zs


def _inproj(x, mods, norm_g, w_in, cos, sa, sb, *, layer):
    n_tiles = N_TOK // TM
    tab = pl.BlockSpec((TM, HEAD_DIM), lambda i: (i, 0))
    return pl.pallas_call(
        _inproj_kernel,
        grid=(n_tiles,),
        in_specs=[
            pl.BlockSpec((TM, D_MODEL), lambda i: (i, 0)),
            pl.BlockSpec((None, None, N_MOD, D_MODEL), lambda i: (layer, _mod_row(i), 0, 0)),
            pl.BlockSpec((None, 6, D_MODEL), lambda i: (layer, 0, 0)),
            pl.BlockSpec((None, D_MODEL, IN_COLS), lambda i: (layer, 0, 0), pipeline_mode=pl.Buffered(1)),
            tab, tab, tab,
        ],
        out_specs=pl.BlockSpec((TM, IN_COLS), lambda i: (i, 0)),
        out_shape=jax.ShapeDtypeStruct((N_TOK, IN_COLS), F32),
        compiler_params=_params("arbitrary"),
        name="inproj",
    )(x, mods, norm_g, w_in, cos, sa, sb)


def _sgu_kernel(z_ref, g_ref, w_ref, b_ref, o_ref, *, rows):
    inv_sqrt2 = 1.0 / math.sqrt(2.0)
    for c in range(rows // CHUNK):
        rs = slice(c * CHUNK, (c + 1) * CHUNK)
        a = z_ref[rs, :]
        a = a * (lax.erf(a * inv_sqrt2) + 1.0) * 0.5
        u, v = a[:, :SGU_WIDTH], a[:, SGU_WIDTH:]
        mu = jnp.mean(v, axis=-1, keepdims=True)
        vc = v - mu
        var = jnp.mean(vc * vc, axis=-1, keepdims=True)
        v = (vc * lax.rsqrt(var + NORM_EPS) * g_ref[...]).astype(BF16)
        for g in range(SGU_GROUPS):
            cs = slice(g * HEAD_DIM, (g + 1) * HEAD_DIM)
            mixed = jnp.dot(w_ref[g].astype(BF16), v[:, cs], preferred_element_type=F32) + b_ref[:, g:g + 1]
            o_ref[rs, cs] = (u[:, cs] * mixed).astype(o_ref.dtype)


def _sgu(z, sgu_g, sgu_w, sgu_b, *, layer, n_rows):
    rows = TM
    return pl.pallas_call(
        functools.partial(_sgu_kernel, rows=rows),
        grid=(n_rows // rows,),
        in_specs=[
            pl.BlockSpec((rows, A_END), lambda i: (i, 0)),
            pl.BlockSpec((None, 1, SGU_WIDTH), lambda i: (layer, 0, 0)),
            pl.BlockSpec((None, SGU_GROUPS, CHUNK, CHUNK), lambda i: (layer, 0, 0, 0)),
            pl.BlockSpec((None, CHUNK, SGU_GROUPS), lambda i: (layer, 0, 0)),
        ],
        out_specs=pl.BlockSpec((rows, SGU_WIDTH), lambda i: (i, 0)),
        out_shape=jax.ShapeDtypeStruct((n_rows, SGU_WIDTH), BF16),
        compiler_params=_params("arbitrary"),
        name="sgu",
    )(z, sgu_g.reshape(DEPTH, 1, SGU_WIDTH), sgu_w, jnp.swapaxes(sgu_b, 1, 2))


def _attn_kernel(sink_ref, q_ref, kp_ref, kc_ref, kn_ref, vp_ref, vc_ref, vn_ref, kx_ref, vx_ref,
                 o_ref, *, layer, nb, n_lat_blocks):
    s_id = pl.program_id(0)
    is_lat = s_id < n_lat_blocks
    n = s_id % nb
    n_keys = CTX_LEN + 3 * BLOCK
    row = lax.broadcasted_iota(jnp.int32, (BLOCK, n_keys), 0)
    col = lax.broadcasted_iota(jnp.int32, (BLOCK, n_keys), 1)
    rel = col - (CTX_LEN + BLOCK)
    lo = jnp.where(is_lat, -n * BLOCK, 4 * BLOCK)
    hi = jnp.where(is_lat, (nb - n) * BLOCK, -4 * BLOCK)
    dist = rel - row
    valid = (col < CTX_LEN) | ((dist >= -WINDOW) & (dist <= WINDOW) & (rel >= lo) & (rel < hi))
    scale = HEAD_DIM ** -0.5
    nt = (((1,), (1,)), ((), ()))
    for kh in range(N_KV_HEADS):
        ks = slice(kh * HEAD_DIM, (kh + 1) * HEAD_DIM)
        keys = jnp.concatenate([kx_ref[:, ks], kp_ref[:, ks], kc_ref[:, ks], kn_ref[:, ks]], axis=0).astype(BF16)
        vals = jnp.concatenate([vx_ref[:, ks], vp_ref[:, ks], vc_ref[:, ks], vn_ref[:, ks]], axis=0).astype(BF16)
        for g in range(Q_GROUP):
            hd = kh * Q_GROUP + g
            hs = slice(hd * HEAD_DIM, (hd + 1) * HEAD_DIM)
            s = lax.dot_general(q_ref[:, hs].astype(BF16), keys, nt, preferred_element_type=F32) * scale
            s = jnp.where(valid, s, NEG_INF)
            sink = sink_ref[layer, hd]
            m = jnp.maximum(jnp.max(s, axis=-1, keepdims=True), sink)
            p = jnp.exp(s - m)
            denom = jnp.sum(p, axis=-1, keepdims=True) + jnp.exp(sink - m)
            probs = (p / denom).astype(BF16)
            o_ref[:, hs] = jnp.dot(probs, vals, preferred_element_type=F32).astype(o_ref.dtype)


def _attention(z, attn_sink, *, layer, n_rows):
    nb = SEQ // BLOCK
    n_lat_blocks = N_LAT // BLOCK
    n_blocks = n_rows // BLOCK
    ctx_blocks = CTX_LEN // BLOCK
    q_col = A_END // ATTN_WIDTH
    k_col = Q_END // KV_WIDTH
    v_col = K_END // KV_WIDTH

    def lat(s):
        return s < n_lat_blocks

    def band(off, col):
        def index(s):
            b = s // nb
            nbr = jnp.clip(s % nb + off, 0, nb - 1) + b * nb
            return (jnp.where(lat(s), nbr, s), col)
        return pl.BlockSpec((BLOCK, KV_WIDTH), index)

    def ctx(col):
        def index(s):
            b = jnp.where(lat(s), s // nb, (s - n_lat_blocks) // ctx_blocks)
            return (N_LAT // CTX_LEN + b, col)
        return pl.BlockSpec((CTX_LEN, KV_WIDTH), index)

    return pl.pallas_call(
        functools.partial(_attn_kernel, layer=layer, nb=nb, n_lat_blocks=n_lat_blocks),
        grid=(n_blocks,),
        in_specs=[
            pl.BlockSpec(memory_space=pltpu.SMEM),
            pl.BlockSpec((BLOCK, ATTN_WIDTH), lambda s: (s, q_col)),
            band(-1, k_col), band(0, k_col), band(1, k_col),
            band(-1, v_col), band(0, v_col), band(1, v_col),
            ctx(k_col), ctx(v_col),
        ],
        out_specs=pl.BlockSpec((BLOCK, ATTN_WIDTH), lambda s: (s, 0)),
        out_shape=jax.ShapeDtypeStruct((n_rows, ATTN_WIDTH), BF16),
        compiler_params=_params("arbitrary"),
        name="attn",
    )(attn_sink, z, z, z, z, z, z, z, z, z)


def _short_conv(ref, w_ref, b_ref, r0, rows, seq):
    halo = 8
    lo = max(r0 - halo, 0)
    hi = min(r0 + rows + halo, seq)
    ext = ref[lo:hi, :]
    n_ext = hi - lo
    off = r0 - lo
    up = pltpu.roll(ext, 1, 0)[off:off + rows]
    dn = pltpu.roll(ext, n_ext - 1, 0)[off:off + rows]
    t = lax.broadcasted_iota(jnp.int32, (rows, 1), 0) + r0
    if r0 == 0:
        up = jnp.where(t == 0, 0.0, up)
    if r0 + rows == seq:
        dn = jnp.where(t == seq - 1, 0.0, dn)
    return up * w_ref[0:1, :] + ext[off:off + rows] * w_ref[1:2, :] + dn * w_ref[2:3, :] + b_ref[...]


def _alt_sign(r0, rows):
    t = lax.broadcasted_iota(jnp.int32, (rows, 1), 0) + r0
    return jnp.where((t & 1) == 0, 1.0, -1.0)


def _hy_chunks(seq):
    rows = min(HY_ROWS, seq)
    return [(r * rows, rows) for r in range(seq // rows)]


def _filt_kernel(feat_ref, w1_ref, b1_ref, w2_ref, b2_ref, w3_ref, fr_ref, dl_ref,
                 hb_ref, asum_ref, hnyq_ref, hd_ref, *, seq):
    hp = lax.Precision.HIGHEST

    @pl.when(pl.program_id(1) == 0)
    def _():
        fr = fr_ref[...]
        hd = jnp.sin(fr * (jnp.dot(feat_ref[...], w1_ref[...], precision=hp, preferred_element_type=F32)
                           + b1_ref[...]))
        hd_ref[...] = jnp.sin(fr * (jnp.dot(hd, w2_ref[...], precision=hp, preferred_element_type=F32)
                                    + b2_ref[...]))

    asum = jnp.zeros((1, HY_WIDTH), F32)
    hnyq = jnp.zeros((1, HY_WIDTH), F32)
    for r0, rows in _hy_chunks(seq):
        rs = slice(r0, r0 + rows)
        h = jnp.dot(hd_ref[rs, :], w3_ref[...], precision=hp, preferred_element_type=F32)
        h = h * jnp.exp(-feat_ref[rs, 0:1] * dl_ref[...])
        asum = asum + jnp.sum(jnp.abs(h), axis=0, keepdims=True)
        hnyq = hnyq + jnp.sum(h * _alt_sign(r0, rows), axis=0, keepdims=True)
        hb_ref[rs, :] = h.astype(BF16)
    asum_ref[...] = asum
    hnyq_ref[...] = hnyq


def _hyena_filters(seq, w1, b1, w2, b2, w3, freq):
    t = jnp.linspace(0.0, 1.0, seq, dtype=F32)[:, None]
    w = (2.0 * math.pi / seq) * jnp.arange(seq, dtype=F32)[:, None]
    bands = jnp.linspace(1e-4, HY_BANDS - 1, HY_BANDS, dtype=F32)[None, :]
    feat = jnp.concatenate([t, jnp.cos(bands * w), -jnp.sin(bands * w)], axis=-1)
    feat = jnp.pad(feat, ((0, 0), (0, HY_EMB_PAD - HY_EMB)))
    w1p = jnp.pad(w1, ((0, 0), (0, HY_EMB_PAD - HY_EMB), (0, 0)))
    deltas = jnp.abs(jnp.linspace(math.log(HY_TARGET) / HY_FAST_DECAY,
                                  math.log(HY_TARGET) / HY_SLOW_DECAY, HY_WIDTH, dtype=F32))[None, :]
    n_cols = HY_ORDER * 2 * HY_WIDTH

    def row(a):
        return a.reshape(DEPTH, 1, HY_HIDDEN)

    vec = pl.BlockSpec((None, 1, HY_HIDDEN), lambda l, j: (l, 0, 0))
    stat = pl.BlockSpec((None, 1, HY_WIDTH), lambda l, j: (l, 0, j))
    return pl.pallas_call(
        functools.partial(_filt_kernel, seq=seq),
        grid=(DEPTH, n_cols // HY_WIDTH),
        in_specs=[
            pl.BlockSpec((seq, HY_EMB_PAD), lambda l, j: (0, 0)),
            pl.BlockSpec((None, HY_EMB_PAD, HY_HIDDEN), lambda l, j: (l, 0, 0)),
            vec,
            pl.BlockSpec((None, HY_HIDDEN, HY_HIDDEN), lambda l, j: (l, 0, 0)),
            vec,
            pl.BlockSpec((None, HY_HIDDEN, HY_WIDTH), lambda l, j: (l, 0, j)),
            vec,
            pl.BlockSpec((1, HY_WIDTH), lambda l, j: (0, 0)),
        ],
        out_specs=[pl.BlockSpec((None, seq, HY_WIDTH), lambda l, j: (l, 0, j)), stat, stat],
        out_shape=[jax.ShapeDtypeStruct((DEPTH, seq, n_cols), BF16),
                   jax.ShapeDtypeStruct((DEPTH, 1, n_cols), F32),
                   jax.ShapeDtypeStruct((DEPTH, 1, n_cols), F32)],
        scratch_shapes=[pltpu.VMEM((seq, HY_HIDDEN), F32)],
        compiler_params=_params("arbitrary", "arbitrary"),
        name=f"hy_filt_l{seq}",
    )(feat, w1p, row(b1), w2, row(b2), w3, row(freq), deltas)


def _dft_matrices(seq, kf):
    n_fft = 2 * seq
    n_ch = seq // kf
    th = 2.0 * math.pi / n_fft
    c = jnp.arange(n_ch, dtype=jnp.int32)[:, None]
    kl = jnp.arange(kf, dtype=jnp.int32)[:, None]
    n = jnp.arange(seq, dtype=jnp.int32)[None, :]
    a1 = th * ((c * kf * n) % n_fft).astype(F32)
    a2 = th * ((kl * n) % n_fft).astype(F32)
    c1, s1, c2, s2 = jnp.cos(a1), jnp.sin(a1), jnp.cos(a2), jnp.sin(a2)
    f_cos = c1[:, None, :] * c2[None, :, :] - s1[:, None, :] * s2[None, :, :]
    f_sin = s1[:, None, :] * c2[None, :, :] + c1[:, None, :] * s2[None, :, :]
    fmat = jnp.concatenate([f_cos, -f_sin], axis=1).astype(BF16)
    c2t, s2t = c2.T, s2.T
    g_cos = c1[:, :, None] * c2t[None, :, :] - s1[:, :, None] * s2t[None, :, :]
    g_sin = s1[:, :, None] * c2t[None, :, :] + c1[:, :, None] * s2t[None, :, :]
    gmat = jnp.concatenate([g_cos, -g_sin], axis=2).astype(BF16)
    return fmat, gmat


def _hy_pre_kernel(z_ref, w_ref, b_ref, ub_ref, nyq_ref, *, seq):
    nyq = jnp.zeros((1, HY_COLS), F32)
    for r0, rows in _hy_chunks(seq):
        v = _short_conv(z_ref, w_ref, b_ref, r0, rows, seq)
        ub_ref[r0:r0 + rows, :] = v.astype(BF16)
        nyq = nyq + jnp.sum(v * _alt_sign(r0, rows), axis=0, keepdims=True)
    nyq_ref[...] = nyq


def _hy_spec_kernel(u_ref, hf_ref, hb_ref, af_ref, ab_ref, f_ref, g_ref, y_ref, *, kf, n_fft):
    c = pl.program_id(1)
    f = f_ref[...]
    hf = jnp.dot(f, hf_ref[...], preferred_element_type=F32)
    hb = jnp.dot(f, hb_ref[...], preferred_element_type=F32)
    row = lax.broadcasted_iota(jnp.int32, (kf, 1), 0)
    wgt = jnp.where((row == 0) & (c == 0), 1.0 / n_fft, 2.0 / n_fft) / (af_ref[...] + ab_ref[...] + 1e-6)
    sr = (hf[:kf] + hb[:kf]) * wgt
    si = (hf[kf:] - hb[kf:]) * wgt
    g = g_ref[...]
    for b in range(BATCH):
        uu = jnp.dot(f, u_ref[b], preferred_element_type=F32)
        ur, ui = uu[:kf], uu[kf:]
        prod = jnp.concatenate([ur * sr - ui * si, ur * si + ui * sr], axis=0).astype(BF16)
        part = jnp.dot(g, prod, preferred_element_type=F32)

        @pl.when(c == 0)
        def _():
            y_ref[b] = part

        @pl.when(c != 0)
        def _():
            y_ref[b] += part


def _nyq_term(unyq_ref, hnf_ref, hnb_ref, af_ref, ab_ref, n_fft):
    return unyq_ref[...] * (hnf_ref[...] + hnb_ref[...]) / (af_ref[...] + ab_ref[...] + 1e-6) * (1.0 / n_fft)


def _hy_mid_kernel(zv_ref, zx_ref, y_ref, unyq_ref, hnf_ref, hnb_ref, af_ref, ab_ref, wv_ref, bv_ref,
                   wx_ref, bx_ref, skip_ref, s_ref, sb_ref, snyq_ref, *, seq):
    pn = _nyq_term(unyq_ref, hnf_ref, hnb_ref, af_ref, ab_ref, 2 * seq)
    snyq = jnp.zeros((1, HY_COLS), F32)
    for r0, rows in _hy_chunks(seq):
        rs = slice(r0, r0 + rows)
        sign = _alt_sign(r0, rows)
        v = _short_conv(zv_ref, wv_ref, bv_ref, r0, rows, seq)
        x1 = _short_conv(zx_ref, wx_ref, bx_ref, r0, rows, seq)
        s = x1 * (y_ref[rs, :] + sign * pn + v * skip_ref[...])
        s_ref[rs, :] = s
        sb_ref[rs, :] = s.astype(BF16)
        snyq = snyq + jnp.sum(s * sign, axis=0, keepdims=True)
    snyq_ref[...] = snyq


def _hy_post_kernel(zx_ref, y_ref, s_ref, snyq_ref, hnf_ref, hnb_ref, af_ref, ab_ref, wx_ref, bx_ref,
                    skip_ref, *rest, seq):
    o_ref = rest[-1]
    pn = _nyq_term(snyq_ref, hnf_ref, hnb_ref, af_ref, ab_ref, 2 * seq)
    for r0, rows in _hy_chunks(seq):
        rs = slice(r0, r0 + rows)
        x2 = _short_conv(zx_ref, wx_ref, bx_ref, r0, rows, seq)
        y = y_ref[rs, :] + _alt_sign(r0, rows) * pn + s_ref[rs, :] * skip_ref[...]
        o_ref[rs, :] = (x2 * y).astype(o_ref.dtype)


def _hyena(z, z_rb0, conv_w, conv_b, skip, filt, fmat, gmat, *, layer, seq, out_rows, out_rb0,
           alias_out=None):
    hb, asum, hnyq = filt
    kf = min(HY_KF, seq)
    n_ch = seq // kf
    n_cols = HY_WIDTH // HY_COLS
    hy0 = V_END // HY_COLS
    cpo = 2 * n_cols

    def zcols(which):
        return pl.BlockSpec((seq, HY_COLS), lambda b, j: (z_rb0 + b, hy0 + which * n_cols + j))

    def cw(which):
        return pl.BlockSpec((None, HY_SHORT, HY_COLS), lambda b, j: (layer, 0, which * n_cols + j))

    def cb(which):
        return pl.BlockSpec((None, 1, HY_COLS), lambda b, j: (layer, 0, which * n_cols + j))

    def fstat(order, direction):
        return pl.BlockSpec((None, 1, HY_COLS), lambda b, j: (layer, 0, order * cpo + direction * n_cols + j))

    def skip_spec(order):
        return pl.BlockSpec((None, 1, HY_COLS), lambda b, j: (layer * HY_ORDER + order, 0, j))

    seq_blk = pl.BlockSpec((None, seq, HY_COLS), lambda b, j: (b, 0, j))
    nyq_spec = pl.BlockSpec((None, 1, HY_COLS), lambda b, j: (b, 0, j))
    ew_params = _params("arbitrary", "arbitrary")
    act = jax.ShapeDtypeStruct((BATCH, seq, HY_WIDTH), F32)
    act_b = jax.ShapeDtypeStruct((BATCH, seq, HY_WIDTH), BF16)
    nyq_shape = jax.ShapeDtypeStruct((BATCH, 1, HY_WIDTH), F32)

    def spectral(ub, order):
        def fcol(direction):
            return pl.BlockSpec((None, seq, HY_COLS), lambda j, c: (layer, 0, order * cpo + direction * n_cols + j),
                                pipeline_mode=pl.Buffered(1))

        def fsum(direction):
            return pl.BlockSpec((None, 1, HY_COLS), lambda j, c: (layer, 0, order * cpo + direction * n_cols + j))

        return pl.pallas_call(
            functools.partial(_hy_spec_kernel, kf=kf, n_fft=2 * seq),
            grid=(n_cols, n_ch),
            in_specs=[
                pl.BlockSpec((BATCH, seq, HY_COLS), lambda j, c: (0, 0, j), pipeline_mode=pl.Buffered(1)),
                fcol(0), fcol(1), fsum(0), fsum(1),
                pl.BlockSpec((None, 2 * kf, seq), lambda j, c: (c, 0, 0)),
                pl.BlockSpec((None, seq, 2 * kf), lambda j, c: (c, 0, 0)),
            ],
            out_specs=pl.BlockSpec((BATCH, seq, HY_COLS), lambda j, c: (0, 0, j)),
            out_shape=act,
            compiler_params=_params("arbitrary", "arbitrary"),
            name=f"hy_spec{order}_l{seq}",
        )(ub, hb, hb, asum, asum, fmat, gmat)

    ub, unyq = pl.pallas_call(
        functools.partial(_hy_pre_kernel, seq=seq),
        grid=(BATCH, n_cols),
        in_specs=[zcols(0), cw(0), cb(0)],
        out_specs=[seq_blk, nyq_spec],
        out_shape=[act_b, nyq_shape],
        compiler_params=ew_params,
        name=f"hy_pre_l{seq}",
    )(z, conv_w, conv_b)
    y0 = spectral(ub, 0)
    s, sb, snyq = pl.pallas_call(
        functools.partial(_hy_mid_kernel, seq=seq),
        grid=(BATCH, n_cols),
        in_specs=[zcols(0), zcols(1), seq_blk, nyq_spec, fstat(0, 0), fstat(0, 1), fstat(0, 0), fstat(0, 1),
                  cw(0), cb(0), cw(1), cb(1), skip_spec(0)],
        out_specs=[seq_blk, seq_blk, nyq_spec],
        out_shape=[act, act_b, nyq_shape],
        compiler_params=ew_params,
        name=f"hy_mid_l{seq}",
    )(z, z, y0, unyq, hnyq, hnyq, asum, asum, conv_w, conv_b, conv_w, conv_b, skip)
    y1 = spectral(sb, 1)
    in_specs = [zcols(2), seq_blk, seq_blk, nyq_spec, fstat(1, 0), fstat(1, 1), fstat(1, 0), fstat(1, 1),
                cw(2), cb(2), skip_spec(1)]
    args = [z, y1, s, snyq, hnyq, hnyq, asum, asum, conv_w, conv_b, skip]
    aliases = {}
    if alias_out is not None:
        in_specs.append(pl.BlockSpec(memory_space=pl.ANY))
        args.append(alias_out)
        aliases = {len(args) - 1: 0}
    return pl.pallas_call(
        functools.partial(_hy_post_kernel, seq=seq),
        grid=(BATCH, n_cols),
        in_specs=in_specs,
        out_specs=pl.BlockSpec((seq, HY_COLS), lambda b, j: (out_rb0 + b, j)),
        out_shape=jax.ShapeDtypeStruct((out_rows, HY_WIDTH), BF16),
        input_output_aliases=aliases,
        compiler_params=ew_params,
        name=f"hy_post_l{seq}",
    )(*args)


def _outproj_kernel(a_ref, b_ref, c_ref, w_ref, x_ref, mod_ref, g_ref, o_ref):
    y = jnp.dot(a_ref[...], w_ref[:SGU_WIDTH, :], preferred_element_type=F32)
    y += jnp.dot(b_ref[...], w_ref[SGU_WIDTH:SGU_WIDTH + ATTN_WIDTH, :], preferred_element_type=F32)
    y += jnp.dot(c_ref[...], w_ref[SGU_WIDTH + ATTN_WIDTH:, :], preferred_element_type=F32)
    o_ref[...] = x_ref[...] + mod_ref[5:6, :] * _rms(y, g_ref[3:4, :])


def _outproj(a, b, c, w_out, x, mods, norm_g, *, layer, n_tiles):
    return pl.pallas_call(
        _outproj_kernel,
        grid=(n_tiles,),
        in_specs=[
            pl.BlockSpec((TM, SGU_WIDTH), lambda i: (i, 0)),
            pl.BlockSpec((TM, ATTN_WIDTH), lambda i: (i, 0)),
            pl.BlockSpec((TM, HY_WIDTH), lambda i: (i, 0)),
            pl.BlockSpec((None, D_MIX, D_MODEL), lambda i: (layer, 0, 0)),
            pl.BlockSpec((TM, D_MODEL), lambda i: (i, 0)),
            pl.BlockSpec((None, None, N_MOD, D_MODEL), lambda i: (layer, _mod_row(i), 0, 0)),
            pl.BlockSpec((None, 6, D_MODEL), lambda i: (layer, 0, 0)),
        ],
        out_specs=pl.BlockSpec((TM, D_MODEL), lambda i: (i, 0)),
        out_shape=jax.ShapeDtypeStruct((n_tiles * TM, D_MODEL), F32),
        compiler_params=_params("arbitrary"),
        name="outproj",
    )(a, b, c, w_out, x, mods, norm_g)


def _rope_tables():
    rows = SEQ // GRID_W
    row = jnp.broadcast_to(jnp.arange(rows)[:, None], (rows, GRID_W)).reshape(SEQ).astype(F32)
    col = jnp.broadcast_to(jnp.arange(GRID_W)[None, :], (rows, GRID_W)).reshape(SEQ).astype(F32)
    half = HEAD_DIM // 2
    inv = ROPE_BASE ** (-jnp.arange(0, half, 2, dtype=F32) / half)
    ang_r = row[:, None] * inv
    ang_c = col[:, None] * inv
    ang = jnp.concatenate([ang_r, ang_r, ang_c, ang_c], axis=-1)
    cos, sin = jnp.cos(ang), jnp.sin(ang)
    first = (jnp.arange(HEAD_DIM) % half) < (half // 2)
    sa = jnp.where(first, -sin, 0.0)
    sb = jnp.where(first, 0.0, sin)

    def stack(lat, ctx_value):
        return jnp.concatenate([lat] * BATCH + [jnp.full((N_CTX, HEAD_DIM), ctx_value, F32)], axis=0)

    return stack(cos, 1.0), stack(sa, 0.0), stack(sb, 0.0)


def kernel(x, c, ctx, c_ctx, w_ada, b_ada, norm_g, ffn1_gate, ffn1_up, ffn1_down, w_in,
           sgu_g, sgu_w, sgu_b, attn_sink, hy_conv_w, hy_conv_b, hy_f_w1, hy_f_b1, hy_f_w2,
           hy_f_b2, hy_f_w3, hy_freq, hy_skip, w_out, ffn2_gate, ffn2_up, ffn2_down):
    xs = jnp.concatenate([x.reshape(N_LAT, D_MODEL), ctx.reshape(N_CTX, D_MODEL)], axis=0)
    cc = jnp.concatenate([c, c_ctx[None], jnp.zeros((MOD_ROWS - BATCH - 1, D_MODEL), F32)], axis=0)
    mods = _ada_mods(cc, w_ada, b_ada)

    w1g, w1u, w1d = ffn1_gate.astype(BF16), ffn1_up.astype(BF16), ffn1_down.astype(BF16)
    w2g, w2u, w2d = ffn2_gate.astype(BF16), ffn2_up.astype(BF16), ffn2_down.astype(BF16)
    w_in_b, w_out_b = w_in.astype(BF16), w_out.astype(BF16)
    cos, sa, sb = _rope_tables()
    conv_b = hy_conv_b.reshape(DEPTH, 1, (HY_ORDER + 1) * HY_WIDTH)
    skip = hy_skip.reshape(DEPTH * HY_ORDER, 1, HY_WIDTH)
    filt_w = (hy_f_w1, hy_f_b1, hy_f_w2, hy_f_b2, hy_f_w3, hy_freq)
    filt_lat = _hyena_filters(SEQ, *filt_w)
    filt_ctx = _hyena_filters(CTX_LEN, *filt_w)
    f_lat, g_lat = _dft_matrices(SEQ, min(HY_KF, SEQ))
    f_ctx, g_ctx = _dft_matrices(CTX_LEN, min(HY_KF, CTX_LEN))

    all_tiles = N_TOK // TM
    lat_tiles = N_LAT // TM
    ctx_rb = N_LAT // CTX_LEN

    for i in range(DEPTH):
        last = i == DEPTH - 1
        n_tiles = lat_tiles if last else all_tiles
        n_rows = n_tiles * TM

        xs = _ffn(xs, mods, norm_g, w1g, w1u, w1d, layer=i, k=0, n_tiles=all_tiles)
        z = _inproj(xs, mods, norm_g, w_in_b, cos, sa, sb, layer=i)

        a_out = _sgu(z, sgu_g, sgu_w, sgu_b, layer=i, n_rows=n_rows)
        b_out = _attention(z, attn_sink, layer=i, n_rows=n_rows)
        c_out = _hyena(z, 0, hy_conv_w, conv_b, skip, filt_lat, f_lat, g_lat, layer=i, seq=SEQ,
                       out_rows=n_rows, out_rb0=0)
        if not last:
            c_out = _hyena(z, ctx_rb, hy_conv_w, conv_b, skip, filt_ctx, f_ctx, g_ctx, layer=i, seq=CTX_LEN,
                           out_rows=n_rows, out_rb0=ctx_rb, alias_out=c_out)

        xs = _outproj(a_out, b_out, c_out, w_out_b, xs, mods, norm_g, layer=i, n_tiles=n_tiles)
        xs = _ffn(xs, mods, norm_g, w2g, w2u, w2d, layer=i, k=2, n_tiles=n_tiles)

    return xs[:N_LAT].reshape(BATCH, SEQ, D_MODEL)
```

```python
import functools
import math

import jax
import jax.numpy as jnp
from jax import lax
from jax.experimental import pallas as pl
from jax.experimental.pallas import tpu as pltpu

D_MODEL = 2048
BATCH = 2
SEQ = 4096
DEPTH = 4
GRID_W = 64
CTX_LEN = 256
HEAD_DIM = 128
NORM_EPS = 1e-6
N_MOD = 9
D_FF = 5632
SGU_GROUPS = 4
SGU_WIDTH = SGU_GROUPS * HEAD_DIM
CHUNK = 128
N_Q_HEADS = 8
N_KV_HEADS = 2
Q_GROUP = N_Q_HEADS // N_KV_HEADS
ATTN_WIDTH = N_Q_HEADS * HEAD_DIM
KV_WIDTH = N_KV_HEADS * HEAD_DIM
WINDOW = 128
BLOCK = 128
ROPE_BASE = 10000.0
NEG_INF = -1e30
HY_GROUPS = 4
HY_WIDTH = HY_GROUPS * HEAD_DIM
HY_ORDER = 2
HY_SHORT = 3
HY_BANDS = 16
HY_EMB = 1 + 2 * HY_BANDS
HY_HIDDEN = 64
HY_TARGET = 1e-2
HY_FAST_DECAY = 0.3
HY_SLOW_DECAY = 1.5
D_MIX = SGU_WIDTH + ATTN_WIDTH + HY_WIDTH
A_END = 2 * SGU_WIDTH
Q_END = A_END + ATTN_WIDTH
K_END = Q_END + KV_WIDTH
V_END = K_END + KV_WIDTH
IN_COLS = V_END + (HY_ORDER + 1) * HY_WIDTH

N_LAT = BATCH * SEQ
N_CTX = BATCH * CTX_LEN
N_TOK = N_LAT + N_CTX
MOD_ROWS = 8

TM = 512
TF = 512
TN_IN = 1024
TN_ADA = 1024
HY_COLS = 256
HY_FILT_COLS = 2 * HY_COLS
HY_KF = 256
HY_ROWS = 512
HY_INV_ROWS = 1024
HY_EMB_PAD = 128
VMEM_LIMIT = 56 * 1024 * 1024

F32 = jnp.float32
BF16 = jnp.bfloat16


def _params(*sem):
    return pltpu.CompilerParams(dimension_semantics=sem, vmem_limit_bytes=VMEM_LIMIT)


def _mod_row(i):
    return jnp.minimum(i // (SEQ // TM), BATCH)


def _silu(a):
    return a / (1.0 + jnp.exp(-a))


def _rms(x, g):
    return x * lax.rsqrt(jnp.mean(x * x, axis=-1, keepdims=True) + NORM_EPS) * g


def _ada_kernel(c_ref, w_ref, b_ref, o_ref):
    s = _silu(c_ref[...]).astype(BF16)
    o_ref[...] = jnp.dot(s, w_ref[...].astype(BF16), preferred_element_type=F32) + b_ref[...]


def _ada_mods(cc, w_ada, b_ada):
    n_out = N_MOD * D_MODEL
    out = pl.pallas_call(
        _ada_kernel,
        grid=(DEPTH, n_out // TN_ADA),
        in_specs=[
            pl.BlockSpec((MOD_ROWS, D_MODEL), lambda l, j: (0, 0)),
            pl.BlockSpec((None, D_MODEL, TN_ADA), lambda l, j: (l, 0, j)),
            pl.BlockSpec((None, 1, TN_ADA), lambda l, j: (l, 0, j)),
        ],
        out_specs=pl.BlockSpec((None, MOD_ROWS, TN_ADA), lambda l, j: (l, 0, j)),
        out_shape=jax.ShapeDtypeStruct((DEPTH, MOD_ROWS, n_out), F32),
        compiler_params=_params("arbitrary", "arbitrary"),
        name="ada_mods",
    )(cc, w_ada, b_ada.reshape(DEPTH, 1, n_out))
    return out.reshape(DEPTH, MOD_ROWS, N_MOD, D_MODEL)


def _ffn_kernel(x_ref, mod_ref, g_ref, wg_ref, wu_ref, wd_ref, o_ref, h_ref, acc_ref, *, k, nf):
    f = pl.program_id(1)

    @pl.when(f == 0)
    def _():
        y = _rms(x_ref[...], g_ref[2 * k:2 * k + 1, :])
        h = y * (1.0 + mod_ref[3 * k:3 * k + 1, :]) + mod_ref[3 * k + 1:3 * k + 2, :]
        h_ref[...] = h.astype(BF16)
        acc_ref[...] = jnp.zeros_like(acc_ref)

    h = h_ref[...]
    a = jnp.dot(h, wg_ref[...], preferred_element_type=F32)
    b = jnp.dot(h, wu_ref[...], preferred_element_type=F32)
    p = (_silu(a) * b).astype(BF16)
    acc_ref[...] += jnp.dot(p, wd_ref[...], preferred_element_type=F32)

    @pl.when(f == nf - 1)
    def _():
        yn = _rms(acc_ref[...], g_ref[2 * k + 1:2 * k + 2, :])
        o_ref[...] = x_ref[...] + (0.5 * mod_ref[3 * k + 2:3 * k + 3, :]) * yn


def _ffn(x, mods, norm_g, wg, wu, wd, *, layer, k, n_tiles):
    nf = D_FF // TF
    return pl.pallas_call(
        functools.partial(_ffn_kernel, k=k, nf=nf),
        grid=(n_tiles, nf),
        in_specs=[
            pl.BlockSpec((TM, D_MODEL), lambda i, f: (i, 0)),
            pl.BlockSpec((None, None, N_MOD, D_MODEL), lambda i, f: (layer, _mod_row(i), 0, 0)),
            pl.BlockSpec((None, 6, D_MODEL), lambda i, f: (layer, 0, 0)),
            pl.BlockSpec((None, D_MODEL, TF), lambda i, f: (layer, 0, f)),
            pl.BlockSpec((None, D_MODEL, TF), lambda i, f: (layer, 0, f)),
            pl.BlockSpec((None, TF, D_MODEL), lambda i, f: (layer, f, 0)),
        ],
        out_specs=pl.BlockSpec((TM, D_MODEL), lambda i, f: (i, 0)),
        out_shape=jax.ShapeDtypeStruct((n_tiles * TM, D_MODEL), F32),
        scratch_shapes=[pltpu.VMEM((TM, D_MODEL), BF16), pltpu.VMEM((TM, D_MODEL), F32)],
        compiler_params=_params("arbitrary", "arbitrary"),
        name=f"ffn{k}",
    )(x, mods, norm_g, wg, wu, wd)


def _rope(t, cos, sa, sb):
    return t * cos + pltpu.roll(t, 3 * HEAD_DIM // 4, 1) * sa + pltpu.roll(t, HEAD_DIM // 4, 1) * sb


def _inproj_kernel(x_ref, mod_ref, g_ref, w_ref, cos_ref, sa_ref, sb_ref, o_ref):
    y = _rms(x_ref[...], g_ref[2:3, :])
    h = (y * (1.0 + mod_ref[3:4, :]) + mod_ref[4:5, :]).astype(BF16)
    cos, sa, sb = cos_ref[...], sa_ref[...], sb_ref[...]
    n_rope = (K_END - A_END) // HEAD_DIM
    for n in range(IN_COLS // TN_IN):
        c0 = n * TN_IN
        z = jnp.dot(h, w_ref[:, c0:c0 + TN_IN], preferred_element_type=F32)
        for hd in range(TN_IN // HEAD_DIM):
            col = c0 + hd * HEAD_DIM
            zs = z[:, hd * HEAD_DIM:(hd + 1) * HEAD_DIM]
            if A_END <= col < A_END + n_rope * HEAD_DIM:
                zs = _rope(zs, cos, sa, sb)
            o_ref[:, col:col + HEAD_DIM] = zs


def _inproj(x, mods, norm_g, w_in, cos, sa, sb, *, layer):
    n_tiles = N_TOK // TM
    tab = pl.BlockSpec((TM, HEAD_DIM), lambda i: (i, 0))
    return pl.pallas_call(
        _inproj_kernel,
        grid=(n_tiles,),
        in_specs=[
            pl.BlockSpec((TM, D_MODEL), lambda i: (i, 0)),
            pl.BlockSpec((None, None, N_MOD, D_MODEL), lambda i: (layer, _mod_row(i), 0, 0)),
            pl.BlockSpec((None, 6, D_MODEL), lambda i: (layer, 0, 0)),
            pl.BlockSpec((None, D_MODEL, IN_COLS), lambda i: (layer, 0, 0), pipeline_mode=pl.Buffered(1)),
            tab, tab, tab,
        ],
        out_specs=pl.BlockSpec((TM, IN_COLS), lambda i: (i, 0)),
        out_shape=jax.ShapeDtypeStruct((N_TOK, IN_COLS), F32),
        compiler_params=_params("arbitrary"),
        name="inproj",
    )(x, mods, norm_g, w_in, cos, sa, sb)


def _sgu_kernel(z_ref, g_ref, w_ref, b_ref, o_ref, *, rows):
    inv_sqrt2 = 1.0 / math.sqrt(2.0)
    for c in range(rows // CHUNK):
        rs = slice(c * CHUNK, (c + 1) * CHUNK)
        a = z_ref[rs, :]
        a = a * (lax.erf(a * inv_sqrt2) + 1.0) * 0.5
        u, v = a[:, :SGU_WIDTH], a[:, SGU_WIDTH:]
        mu = jnp.mean(v, axis=-1, keepdims=True)
        vc = v - mu
        var = jnp.mean(vc * vc, axis=-1, keepdims=True)
        v = (vc * lax.rsqrt(var + NORM_EPS) * g_ref[...]).astype(BF16)
        for g in range(SGU_GROUPS):
            cs = slice(g * HEAD_DIM, (g + 1) * HEAD_DIM)
            mixed = jnp.dot(w_ref[g].astype(BF16), v[:, cs], preferred_element_type=F32) + b_ref[:, g:g + 1]
            o_ref[rs, cs] = (u[:, cs] * mixed).astype(o_ref.dtype)


def _sgu(z, sgu_g, sgu_w, sgu_b, *, layer, n_rows):
    rows = TM
    return pl.pallas_call(
        functools.partial(_sgu_kernel, rows=rows),
        grid=(n_rows // rows,),
        in_specs=[
            pl.BlockSpec((rows, A_END), lambda i: (i, 0)),
            pl.BlockSpec((None, 1, SGU_WIDTH), lambda i: (layer, 0, 0)),
            pl.BlockSpec((None, SGU_GROUPS, CHUNK, CHUNK), lambda i: (layer, 0, 0, 0)),
            pl.BlockSpec((None, CHUNK, SGU_GROUPS), lambda i: (layer, 0, 0)),
        ],
        out_specs=pl.BlockSpec((rows, SGU_WIDTH), lambda i: (i, 0)),
        out_shape=jax.ShapeDtypeStruct((n_rows, SGU_WIDTH), BF16),
        compiler_params=_params("arbitrary"),
        name="sgu",
    )(z, sgu_g.reshape(DEPTH, 1, SGU_WIDTH), sgu_w, jnp.swapaxes(sgu_b, 1, 2))


def _attn_kernel(sink_ref, q_ref, kp_ref, kc_ref, kn_ref, vp_ref, vc_ref, vn_ref, kx_ref, vx_ref,
                 o_ref, *, layer, nb, n_lat_blocks):
    s_id = pl.program_id(0)
    is_lat = s_id < n_lat_blocks
    n = s_id % nb
    n_keys = CTX_LEN + 3 * BLOCK
    row = lax.broadcasted_iota(jnp.int32, (BLOCK, n_keys), 0)
    col = lax.broadcasted_iota(jnp.int32, (BLOCK, n_keys), 1)
    rel = col - (CTX_LEN + BLOCK)
    lo = jnp.where(is_lat, -n * BLOCK, 4 * BLOCK)
    hi = jnp.where(is_lat, (nb - n) * BLOCK, -4 * BLOCK)
    dist = rel - row
    valid = (col < CTX_LEN) | ((dist >= -WINDOW) & (dist <= WINDOW) & (rel >= lo) & (rel < hi))
    scale = HEAD_DIM ** -0.5
    nt = (((1,), (1,)), ((), ()))
    for kh in range(N_KV_HEADS):
        ks = slice(kh * HEAD_DIM, (kh + 1) * HEAD_DIM)
        keys = jnp.concatenate([kx_ref[:, ks], kp_ref[:, ks], kc_ref[:, ks], kn_ref[:, ks]], axis=0).astype(BF16)
        vals = jnp.concatenate([vx_ref[:, ks], vp_ref[:, ks], vc_ref[:, ks], vn_ref[:, ks]], axis=0).astype(BF16)
        for g in range(Q_GROUP):
            hd = kh * Q_GROUP + g
            hs = slice(hd * HEAD_DIM, (hd + 1) * HEAD_DIM)
            s = lax.dot_general(q_ref[:, hs].astype(BF16), keys, nt, preferred_element_type=F32) * scale
            s = jnp.where(valid, s, NEG_INF)
            sink = sink_ref[layer, hd]
            m = jnp.maximum(jnp.max(s, axis=-1, keepdims=True), sink)
            p = jnp.exp(s - m)
            denom = jnp.sum(p, axis=-1, keepdims=True) + jnp.exp(sink - m)
            probs = (p / denom).astype(BF16)
            o_ref[:, hs] = jnp.dot(probs, vals, preferred_element_type=F32).astype(o_ref.dtype)


def _attention(z, attn_sink, *, layer, n_rows):
    nb = SEQ // BLOCK
    n_lat_blocks = N_LAT // BLOCK
    n_blocks = n_rows // BLOCK
    ctx_blocks = CTX_LEN // BLOCK
    q_col = A_END // ATTN_WIDTH
    k_col = Q_END // KV_WIDTH
    v_col = K_END // KV_WIDTH

    def lat(s):
        return s < n_lat_blocks

    def band(off, col):
        def index(s):
            b = s // nb
            nbr = jnp.clip(s % nb + off, 0, nb - 1) + b * nb
            return (jnp.where(lat(s), nbr, s), col)
        return pl.BlockSpec((BLOCK, KV_WIDTH), index)

    def ctx(col):
        def index(s):
            b = jnp.where(lat(s), s // nb, (s - n_lat_blocks) // ctx_blocks)
            return (N_LAT // CTX_LEN + b, col)
        return pl.BlockSpec((CTX_LEN, KV_WIDTH), index)

    return pl.pallas_call(
        functools.partial(_attn_kernel, layer=layer, nb=nb, n_lat_blocks=n_lat_blocks),
        grid=(n_blocks,),
        in_specs=[
            pl.BlockSpec(memory_space=pltpu.SMEM),
            pl.BlockSpec((BLOCK, ATTN_WIDTH), lambda s: (s, q_col)),
            band(-1, k_col), band(0, k_col), band(1, k_col),
            band(-1, v_col), band(0, v_col), band(1, v_col),
            ctx(k_col), ctx(v_col),
        ],
        out_specs=pl.BlockSpec((BLOCK, ATTN_WIDTH), lambda s: (s, 0)),
        out_shape=jax.ShapeDtypeStruct((n_rows, ATTN_WIDTH), BF16),
        compiler_params=_params("arbitrary"),
        name="attn",
    )(attn_sink, z, z, z, z, z, z, z, z, z)


def _short_conv(ref, w_ref, b_ref, r0, rows, seq):
    halo = 8
    lo = max(r0 - halo, 0)
    hi = min(r0 + rows + halo, seq)
    ext = ref[lo:hi, :]
    n_ext = hi - lo
    off = r0 - lo
    up = pltpu.roll(ext, 1, 0)[off:off + rows]
    dn = pltpu.roll(ext, n_ext - 1, 0)[off:off + rows]
    t = lax.broadcasted_iota(jnp.int32, (rows, 1), 0) + r0
    if r0 == 0:
        up = jnp.where(t == 0, 0.0, up)
    if r0 + rows == seq:
        dn = jnp.where(t == seq - 1, 0.0, dn)
    return up * w_ref[0:1, :] + ext[off:off + rows] * w_ref[1:2, :] + dn * w_ref[2:3, :] + b_ref[...]


def _alt_sign(r0, rows):
    t = lax.broadcasted_iota(jnp.int32, (rows, 1), 0) + r0
    return jnp.where((t & 1) == 0, 1.0, -1.0)


def _hy_chunks(seq, chunk=None):
    rows = min(chunk or HY_ROWS, seq)
    return [(r * rows, rows) for r in range(seq // rows)]


def _filt_kernel(feat_ref, w1_ref, b1_ref, w2_ref, b2_ref, w3_ref, fr_ref, dl_ref,
                 hb_ref, asum_ref, hnyq_ref, hd_ref, *, seq):
    hp = lax.Precision.HIGHEST

    @pl.when(pl.program_id(1) == 0)
    def _():
        fr = fr_ref[...]
        hd = jnp.sin(fr * (jnp.dot(feat_ref[...], w1_ref[...], precision=hp, preferred_element_type=F32)
                           + b1_ref[...]))
        hd_ref[...] = jnp.sin(fr * (jnp.dot(hd, w2_ref[...], precision=hp, preferred_element_type=F32)
                                    + b2_ref[...]))

    asum = jnp.zeros((1, HY_FILT_COLS), F32)
    hnyq = jnp.zeros((1, HY_FILT_COLS), F32)
    for r0, rows in _hy_chunks(seq):
        rs = slice(r0, r0 + rows)
        h = jnp.dot(hd_ref[rs, :], w3_ref[...], precision=hp, preferred_element_type=F32)
        h = h * jnp.exp(-feat_ref[rs, 0:1] * dl_ref[...])
        asum = asum + jnp.sum(jnp.abs(h), axis=0, keepdims=True)
        hnyq = hnyq + jnp.sum(h * _alt_sign(r0, rows), axis=0, keepdims=True)
        hb_ref[rs, :] = h.astype(BF16)
    asum_ref[...] = asum
    hnyq_ref[...] = hnyq


def _hyena_filters(seq, w1, b1, w2, b2, w3, freq):
    n_tiles = HY_WIDTH // HY_COLS
    w3 = w3.reshape(DEPTH, HY_HIDDEN, HY_ORDER, 2, n_tiles, HY_COLS)
    w3 = jnp.swapaxes(w3, 3, 4).reshape(DEPTH, HY_HIDDEN, HY_ORDER * 2 * HY_WIDTH)
    t = jnp.linspace(0.0, 1.0, seq, dtype=F32)[:, None]
    w = (2.0 * math.pi / seq) * jnp.arange(seq, dtype=F32)[:, None]
    bands = jnp.linspace(1e-4, HY_BANDS - 1, HY_BANDS, dtype=F32)[None, :]
    feat = jnp.concatenate([t, jnp.cos(bands * w), -jnp.sin(bands * w)], axis=-1)
    feat = jnp.pad(feat, ((0, 0), (0, HY_EMB_PAD - HY_EMB)))
    w1p = jnp.pad(w1, ((0, 0), (0, HY_EMB_PAD - HY_EMB), (0, 0)))
    deltas = jnp.abs(jnp.linspace(math.log(HY_TARGET) / HY_FAST_DECAY,
                                  math.log(HY_TARGET) / HY_SLOW_DECAY, HY_WIDTH, dtype=F32))
    deltas = jnp.broadcast_to(deltas.reshape(1, n_tiles, 1, HY_COLS), (HY_ORDER, n_tiles, 2, HY_COLS))
    deltas = deltas.reshape(1, HY_ORDER * 2 * HY_WIDTH)
    n_cols = HY_ORDER * 2 * HY_WIDTH

    def row(a):
        return a.reshape(DEPTH, 1, HY_HIDDEN)

    vec = pl.BlockSpec((None, 1, HY_HIDDEN), lambda l, j: (l, 0, 0))
    stat = pl.BlockSpec((None, 1, HY_FILT_COLS), lambda l, j: (l, 0, j))
    return pl.pallas_call(
        functools.partial(_filt_kernel, seq=seq),
        grid=(DEPTH, n_cols // HY_FILT_COLS),
        in_specs=[
            pl.BlockSpec((seq, HY_EMB_PAD), lambda l, j: (0, 0)),
            pl.BlockSpec((None, HY_EMB_PAD, HY_HIDDEN), lambda l, j: (l, 0, 0)),
            vec,
            pl.BlockSpec((None, HY_HIDDEN, HY_HIDDEN), lambda l, j: (l, 0, 0)),
            vec,
            pl.BlockSpec((None, HY_HIDDEN, HY_FILT_COLS), lambda l, j: (l, 0, j)),
            vec,
            pl.BlockSpec((1, HY_FILT_COLS), lambda l, j: (0, j)),
        ],
        out_specs=[pl.BlockSpec((None, seq, HY_FILT_COLS), lambda l, j: (l, 0, j)), stat, stat],
        out_shape=[jax.ShapeDtypeStruct((DEPTH, seq, n_cols), BF16),
                   jax.ShapeDtypeStruct((DEPTH, 1, n_cols), F32),
                   jax.ShapeDtypeStruct((DEPTH, 1, n_cols), F32)],
        scratch_shapes=[pltpu.VMEM((seq, HY_HIDDEN), F32)],
        compiler_params=_params("arbitrary", "arbitrary"),
        name=f"hy_filt_l{seq}",
    )(feat, w1p, row(b1), w2, row(b2), w3, row(freq), deltas)


def _dft_matrices(seq, kf):
    n_fft = 2 * seq
    n_ch = seq // kf
    th = 2.0 * math.pi / n_fft
    c = jnp.arange(n_ch, dtype=jnp.int32)[:, None]
    kl = jnp.arange(kf, dtype=jnp.int32)[:, None]
    n = jnp.arange(seq, dtype=jnp.int32)[None, :]
    a1 = th * ((c * kf * n) % n_fft).astype(F32)
    a2 = th * ((kl * n) % n_fft).astype(F32)
    c1, s1, c2, s2 = jnp.cos(a1), jnp.sin(a1), jnp.cos(a2), jnp.sin(a2)
    f_cos = c1[:, None, :] * c2[None, :, :] - s1[:, None, :] * s2[None, :, :]
    f_sin = s1[:, None, :] * c2[None, :, :] + c1[:, None, :] * s2[None, :, :]
    fmat = jnp.concatenate([f_cos, -f_sin], axis=1).astype(BF16)
    c2t, s2t = c2.T, s2.T
    g_cos = c1[:, :, None] * c2t[None, :, :] - s1[:, :, None] * s2t[None, :, :]
    g_sin = s1[:, :, None] * c2t[None, :, :] + c1[:, :, None] * s2t[None, :, :]
    gmat = jnp.concatenate([g_cos, -g_sin], axis=2).astype(BF16)
    return fmat, gmat


def _hy_pre_kernel(z_ref, w_ref, b_ref, ub_ref, nyq_ref, *, seq):
    nyq = jnp.zeros((1, HY_COLS), F32)
    for r0, rows in _hy_chunks(seq):
        v = _short_conv(z_ref, w_ref, b_ref, r0, rows, seq)
        ub_ref[r0:r0 + rows, :] = v.astype(BF16)
        nyq = nyq + jnp.sum(v * _alt_sign(r0, rows), axis=0, keepdims=True)
    nyq_ref[...] = nyq


def _hy_spec_kernel(u_ref, h_ref, a_ref, f_ref, g_ref, y_ref, *, seq, kf, n_fft):
    c = pl.program_id(1)
    cw = HY_COLS

    @pl.when(c == 0)
    def _():
        y_ref[...] = jnp.zeros_like(y_ref)

    f = f_ref[...]
    hh = jnp.dot(f, h_ref[...], preferred_element_type=F32)
    uu = jnp.dot(f, u_ref[...], preferred_element_type=F32)
    row = lax.broadcasted_iota(jnp.int32, (kf, 1), 0)
    a = a_ref[...]
    wgt = jnp.where((row == 0) & (c == 0), 1.0 / n_fft, 2.0 / n_fft) / (a[:, :cw] + a[:, cw:] + 1e-6)
    sr = (hh[:kf, :cw] + hh[:kf, cw:]) * wgt
    si = (hh[kf:, :cw] - hh[kf:, cw:]) * wgt
    p_re, p_im = [], []
    for b in range(BATCH):
        ur, ui = uu[:kf, b * cw:(b + 1) * cw], uu[kf:, b * cw:(b + 1) * cw]
        p_re.append(ur * sr - ui * si)
        p_im.append(ur * si + ui * sr)
    prod = jnp.concatenate([jnp.concatenate(p_re, axis=1), jnp.concatenate(p_im, axis=1)], axis=0).astype(BF16)
    for r0, rows in _hy_chunks(seq, HY_INV_ROWS):
        rs = slice(r0, r0 + rows)
        y_ref[rs, :] += jnp.dot(g_ref[rs, :], prod, preferred_element_type=F32)


def _nyq_term(unyq_ref, hnf_ref, hnb_ref, af_ref, ab_ref, n_fft):
    return unyq_ref[...] * (hnf_ref[...] + hnb_ref[...]) / (af_ref[...] + ab_ref[...] + 1e-6) * (1.0 / n_fft)


def _hy_mid_kernel(zv_ref, zx_ref, y_ref, unyq_ref, hnf_ref, hnb_ref, af_ref, ab_ref, wv_ref, bv_ref,
                   wx_ref, bx_ref, skip_ref, s_ref, sb_ref, snyq_ref, *, seq):
    pn = _nyq_term(unyq_ref, hnf_ref, hnb_ref, af_ref, ab_ref, 2 * seq)
    snyq = jnp.zeros((1, HY_COLS), F32)
    for r0, rows in _hy_chunks(seq):
        rs = slice(r0, r0 + rows)
        sign = _alt_sign(r0, rows)
        v = _short_conv(zv_ref, wv_ref, bv_ref, r0, rows, seq)
        x1 = _short_conv(zx_ref, wx_ref, bx_ref, r0, rows, seq)
        s = x1 * (y_ref[rs, :] + sign * pn + v * skip_ref[...])
        s_ref[rs, :] = s
        sb_ref[rs, :] = s.astype(BF16)
        snyq = snyq + jnp.sum(s * sign, axis=0, keepdims=True)
    snyq_ref[...] = snyq


def _hy_post_kernel(zx_ref, y_ref, s_ref, snyq_ref, hnf_ref, hnb_ref, af_ref, ab_ref, wx_ref, bx_ref,
                    skip_ref, *rest, seq):
    o_ref = rest[-1]
    pn = _nyq_term(snyq_ref, hnf_ref, hnb_ref, af_ref, ab_ref, 2 * seq)
    for r0, rows in _hy_chunks(seq):
        rs = slice(r0, r0 + rows)
        x2 = _short_conv(zx_ref, wx_ref, bx_ref, r0, rows, seq)
        y = y_ref[rs, :] + _alt_sign(r0, rows) * pn + s_ref[rs, :] * skip_ref[...]
        o_ref[rs, :] = (x2 * y).astype(o_ref.dtype)


def _hyena(z, z_rb0, conv_w, conv_b, skip, filt, fmat, gmat, *, layer, seq, out_rows, out_rb0,
           alias_out=None):
    hb, asum, hnyq = filt
    kf = min(HY_KF, seq)
    n_ch = seq // kf
    n_cols = HY_WIDTH // HY_COLS
    hy0 = V_END // HY_COLS

    def zcols(which):
        return pl.BlockSpec((seq, HY_COLS), lambda b, j: (z_rb0 + b, hy0 + which * n_cols + j))

    def cw(which):
        return pl.BlockSpec((None, HY_SHORT, HY_COLS), lambda b, j: (layer, 0, which * n_cols + j))

    def cb(which):
        return pl.BlockSpec((None, 1, HY_COLS), lambda b, j: (layer, 0, which * n_cols + j))

    def fstat(order, direction):
        return pl.BlockSpec((None, 1, HY_COLS), lambda b, j: (layer, 0, (order * n_cols + j) * 2 + direction))

    def skip_spec(order):
        return pl.BlockSpec((None, 1, HY_COLS), lambda b, j: (layer * HY_ORDER + order, 0, j))

    seq_blk = pl.BlockSpec((seq, HY_COLS), lambda b, j: (0, j * BATCH + b))
    nyq_spec = pl.BlockSpec((1, HY_COLS), lambda b, j: (0, j * BATCH + b))
    ew_params = _params("arbitrary", "arbitrary")
    act = jax.ShapeDtypeStruct((seq, BATCH * HY_WIDTH), F32)
    act_b = jax.ShapeDtypeStruct((seq, BATCH * HY_WIDTH), BF16)
    nyq_shape = jax.ShapeDtypeStruct((1, BATCH * HY_WIDTH), F32)

    def spectral(ub, order):
        return pl.pallas_call(
            functools.partial(_hy_spec_kernel, seq=seq, kf=kf, n_fft=2 * seq),
            grid=(n_cols, n_ch),
            in_specs=[
                pl.BlockSpec((seq, BATCH * HY_COLS), lambda j, c: (0, j), pipeline_mode=pl.Buffered(1)),
                pl.BlockSpec((None, seq, 2 * HY_COLS), lambda j, c: (layer, 0, order * n_cols + j),
                             pipeline_mode=pl.Buffered(1)),
                pl.BlockSpec((None, 1, 2 * HY_COLS), lambda j, c: (layer, 0, order * n_cols + j)),
                pl.BlockSpec((None, 2 * kf, seq), lambda j, c: (c, 0, 0)),
                pl.BlockSpec((None, seq, 2 * kf), lambda j, c: (c, 0, 0)),
            ],
            out_specs=pl.BlockSpec((seq, BATCH * HY_COLS), lambda j, c: (0, j)),
            out_shape=act,
            compiler_params=_params("arbitrary", "arbitrary"),
            name=f"hy_spec{order}_l{seq}",
        )(ub, hb, asum, fmat, gmat)

    ub, unyq = pl.pallas_call(
        functools.partial(_hy_pre_kernel, seq=seq),
        grid=(BATCH, n_cols),
        in_specs=[zcols(0), cw(0), cb(0)],
        out_specs=[seq_blk, nyq_spec],
        out_shape=[act_b, nyq_shape],
        compiler_params=ew_params,
        name=f"hy_pre_l{seq}",
    )(z, conv_w, conv_b)
    y0 = spectral(ub, 0)
    s, sb, snyq = pl.pallas_call(
        functools.partial(_hy_mid_kernel, seq=seq),
        grid=(BATCH, n_cols),
        in_specs=[zcols(0), zcols(1), seq_blk, nyq_spec, fstat(0, 0), fstat(0, 1), fstat(0, 0), fstat(0, 1),
                  cw(0), cb(0), cw(1), cb(1), skip_spec(0)],
        out_specs=[seq_blk, seq_blk, nyq_spec],
        out_shape=[act, act_b, nyq_shape],
        compiler_params=ew_params,
        name=f"hy_mid_l{seq}",
    )(z, z, y0, unyq, hnyq, hnyq, asum, asum, conv_w, conv_b, conv_w, conv_b, skip)
    y1 = spectral(sb, 1)
    in_specs = [zcols(2), seq_blk, seq_blk, nyq_spec, fstat(1, 0), fstat(1, 1), fstat(1, 0), fstat(1, 1),
                cw(2), cb(2), skip_spec(1)]
    args = [z, y1, s, snyq, hnyq, hnyq, asum, asum, conv_w, conv_b, skip]
    aliases = {}
    if alias_out is not None:
        in_specs.append(pl.BlockSpec(memory_space=pl.ANY))
        args.append(alias_out)
        aliases = {len(args) - 1: 0}
    return pl.pallas_call(
        functools.partial(_hy_post_kernel, seq=seq),
        grid=(BATCH, n_cols),
        in_specs=in_specs,
        out_specs=pl.BlockSpec((seq, HY_COLS), lambda b, j: (out_rb0 + b, j)),
        out_shape=jax.ShapeDtypeStruct((out_rows, HY_WIDTH), BF16),
        input_output_aliases=aliases,
        compiler_params=ew_params,
        name=f"hy_post_l{seq}",
    )(*args)


def _outproj_kernel(a_ref, b_ref, c_ref, w_ref, x_ref, mod_ref, g_ref, o_ref):
    y = jnp.dot(a_ref[...], w_ref[:SGU_WIDTH, :], preferred_element_type=F32)
    y += jnp.dot(b_ref[...], w_ref[SGU_WIDTH:SGU_WIDTH + ATTN_WIDTH, :], preferred_element_type=F32)
    y += jnp.dot(c_ref[...], w_ref[SGU_WIDTH + ATTN_WIDTH:, :], preferred_element_type=F32)
    o_ref[...] = x_ref[...] + mod_ref[5:6, :] * _rms(y, g_ref[3:4, :])


def _outproj(a, b, c, w_out, x, mods, norm_g, *, layer, n_tiles):
    return pl.pallas_call(
        _outproj_kernel,
        grid=(n_tiles,),
        in_specs=[
            pl.BlockSpec((TM, SGU_WIDTH), lambda i: (i, 0)),
            pl.BlockSpec((TM, ATTN_WIDTH), lambda i: (i, 0)),
            pl.BlockSpec((TM, HY_WIDTH), lambda i: (i, 0)),
            pl.BlockSpec((None, D_MIX, D_MODEL), lambda i: (layer, 0, 0)),
            pl.BlockSpec((TM, D_MODEL), lambda i: (i, 0)),
            pl.BlockSpec((None, None, N_MOD, D_MODEL), lambda i: (layer, _mod_row(i), 0, 0)),
            pl.BlockSpec((None, 6, D_MODEL), lambda i: (layer, 0, 0)),
        ],
        out_specs=pl.BlockSpec((TM, D_MODEL), lambda i: (i, 0)),
        out_shape=jax.ShapeDtypeStruct((n_tiles * TM, D_MODEL), F32),
        compiler_params=_params("arbitrary"),
        name="outproj",
    )(a, b, c, w_out, x, mods, norm_g)


def _rope_tables():
    rows = SEQ // GRID_W
    row = jnp.broadcast_to(jnp.arange(rows)[:, None], (rows, GRID_W)).reshape(SEQ).astype(F32)
    col = jnp.broadcast_to(jnp.arange(GRID_W)[None, :], (rows, GRID_W)).reshape(SEQ).astype(F32)
    half = HEAD_DIM // 2
    inv = ROPE_BASE ** (-jnp.arange(0, half, 2, dtype=F32) / half)
    ang_r = row[:, None] * inv
    ang_c = col[:, None] * inv
    ang = jnp.concatenate([ang_r, ang_r, ang_c, ang_c], axis=-1)
    cos, sin = jnp.cos(ang), jnp.sin(ang)
    first = (jnp.arange(HEAD_DIM) % half) < (half // 2)
    sa = jnp.where(first, -sin, 0.0)
    sb = jnp.where(first, 0.0, sin)

    def stack(lat, ctx_value):
        return jnp.concatenate([lat] * BATCH + [jnp.full((N_CTX, HEAD_DIM), ctx_value, F32)], axis=0)

    return stack(cos, 1.0), stack(sa, 0.0), stack(sb, 0.0)


def kernel(x, c, ctx, c_ctx, w_ada, b_ada, norm_g, ffn1_gate, ffn1_up, ffn1_down, w_in,
           sgu_g, sgu_w, sgu_b, attn_sink, hy_conv_w, hy_conv_b, hy_f_w1, hy_f_b1, hy_f_w2,
           hy_f_b2, hy_f_w3, hy_freq, hy_skip, w_out, ffn2_gate, ffn2_up, ffn2_down):
    xs = jnp.concatenate([x.reshape(N_LAT, D_MODEL), ctx.reshape(N_CTX, D_MODEL)], axis=0)
    cc = jnp.concatenate([c, c_ctx[None], jnp.zeros((MOD_ROWS - BATCH - 1, D_MODEL), F32)], axis=0)
    mods = _ada_mods(cc, w_ada, b_ada)

    w1g, w1u, w1d = ffn1_gate.astype(BF16), ffn1_up.astype(BF16), ffn1_down.astype(BF16)
    w2g, w2u, w2d = ffn2_gate.astype(BF16), ffn2_up.astype(BF16), ffn2_down.astype(BF16)
    w_in_b, w_out_b = w_in.astype(BF16), w_out.astype(BF16)
    cos, sa, sb = _rope_tables()
    conv_b = hy_conv_b.reshape(DEPTH, 1, (HY_ORDER + 1) * HY_WIDTH)
    skip = hy_skip.reshape(DEPTH * HY_ORDER, 1, HY_WIDTH)
    filt_w = (hy_f_w1, hy_f_b1, hy_f_w2, hy_f_b2, hy_f_w3, hy_freq)
    filt_lat = _hyena_filters(SEQ, *filt_w)
    filt_ctx = _hyena_filters(CTX_LEN, *filt_w)
    f_lat, g_lat = _dft_matrices(SEQ, min(HY_KF, SEQ))
    f_ctx, g_ctx = _dft_matrices(CTX_LEN, min(HY_KF, CTX_LEN))

    all_tiles = N_TOK // TM
    lat_tiles = N_LAT // TM
    ctx_rb = N_LAT // CTX_LEN

    for i in range(DEPTH):
        last = i == DEPTH - 1
        n_tiles = lat_tiles if last else all_tiles
        n_rows = n_tiles * TM

        xs = _ffn(xs, mods, norm_g, w1g, w1u, w1d, layer=i, k=0, n_tiles=all_tiles)
        z = _inproj(xs, mods, norm_g, w_in_b, cos, sa, sb, layer=i)

        a_out = _sgu(z, sgu_g, sgu_w, sgu_b, layer=i, n_rows=n_rows)
        b_out = _attention(z, attn_sink, layer=i, n_rows=n_rows)
        c_out = _hyena(z, 0, hy_conv_w, conv_b, skip, filt_lat, f_lat, g_lat, layer=i, seq=SEQ,
                       out_rows=n_rows, out_rb0=0)
        if not last:
            c_out = _hyena(z, ctx_rb, hy_conv_w, conv_b, skip, filt_ctx, f_ctx, g_ctx, layer=i, seq=CTX_LEN,
                           out_rows=n_rows, out_rb0=ctx_rb, alias_out=c_out)

        xs = _outproj(a_out, b_out, c_out, w_out_b, xs, mods, norm_g, layer=i, n_tiles=n_tiles)
        xs = _ffn(xs, mods, norm_g, w2g, w2u, w2d, layer=i, k=2, n_tiles=n_tiles)

    return xs[:N_LAT].reshape(BATCH, SEQ, D_MODEL)
```

```python
import functools
import math

import jax
import jax.numpy as jnp
from jax import lax
from jax.experimental import pallas as pl
from jax.experimental.pallas import tpu as pltpu

D_MODEL = 2048
BATCH = 2
SEQ = 4096
DEPTH = 4
GRID_W = 64
CTX_LEN = 256
HEAD_DIM = 128
NORM_EPS = 1e-6
N_MOD = 9
D_FF = 5632
SGU_GROUPS = 4
SGU_WIDTH = SGU_GROUPS * HEAD_DIM
CHUNK = 128
N_Q_HEADS = 8
N_KV_HEADS = 2
Q_GROUP = N_Q_HEADS // N_KV_HEADS
ATTN_WIDTH = N_Q_HEADS * HEAD_DIM
KV_WIDTH = N_KV_HEADS * HEAD_DIM
WINDOW = 128
BLOCK = 128
ROPE_BASE = 10000.0
NEG_INF = -1e30
HY_GROUPS = 4
HY_WIDTH = HY_GROUPS * HEAD_DIM
HY_ORDER = 2
HY_SHORT = 3
HY_BANDS = 16
HY_EMB = 1 + 2 * HY_BANDS
HY_HIDDEN = 64
HY_TARGET = 1e-2
HY_FAST_DECAY = 0.3
HY_SLOW_DECAY = 1.5
D_MIX = SGU_WIDTH + ATTN_WIDTH + HY_WIDTH
A_END = 2 * SGU_WIDTH
Q_END = A_END + ATTN_WIDTH
K_END = Q_END + KV_WIDTH
V_END = K_END + KV_WIDTH
IN_COLS = V_END + (HY_ORDER + 1) * HY_WIDTH

N_LAT = BATCH * SEQ
N_CTX = BATCH * CTX_LEN
N_TOK = N_LAT + N_CTX
MOD_ROWS = 8

TM = 512
TF = 512
FFN_ROWS = 32
BF16_SUBLANES = 16
TN_IN = 1024
TN_ADA = 1024
HY_COLS = 256
HY_FILT_COLS = 2 * HY_COLS
HY_KF = 256
HY_ROWS = 512
HY_INV_ROWS = 1024
HY_EMB_PAD = 128
VMEM_LIMIT = 56 * 1024 * 1024

F32 = jnp.float32
BF16 = jnp.bfloat16


def _params(*sem):
    return pltpu.CompilerParams(dimension_semantics=sem, vmem_limit_bytes=VMEM_LIMIT)


def _mod_row(i):
    return jnp.minimum(i // (SEQ // TM), BATCH)


def _silu(a):
    return a / (1.0 + jnp.exp(-a))


def _rms(x, g):
    return x * lax.rsqrt(jnp.mean(x * x, axis=-1, keepdims=True) + NORM_EPS) * g


def _ada_kernel(c_ref, w_ref, b_ref, o_ref):
    s = _silu(c_ref[...]).astype(BF16)
    o_ref[...] = jnp.dot(s, w_ref[...].astype(BF16), preferred_element_type=F32) + b_ref[...]


def _ada_mods(cc, w_ada, b_ada):
    n_out = N_MOD * D_MODEL
    out = pl.pallas_call(
        _ada_kernel,
        grid=(DEPTH, n_out // TN_ADA),
        in_specs=[
            pl.BlockSpec((MOD_ROWS, D_MODEL), lambda l, j: (0, 0)),
            pl.BlockSpec((None, D_MODEL, TN_ADA), lambda l, j: (l, 0, j)),
            pl.BlockSpec((None, 1, TN_ADA), lambda l, j: (l, 0, j)),
        ],
        out_specs=pl.BlockSpec((None, MOD_ROWS, TN_ADA), lambda l, j: (l, 0, j)),
        out_shape=jax.ShapeDtypeStruct((DEPTH, MOD_ROWS, n_out), F32),
        compiler_params=_params("arbitrary", "arbitrary"),
        name="ada_mods",
    )(cc, w_ada, b_ada.reshape(DEPTH, 1, n_out))
    return out.reshape(DEPTH, MOD_ROWS, N_MOD, D_MODEL)


def _ffn_kernel(*refs, k, nf, n_jobs):
    x_ref, mod_ref, g_ref, wg_ref, wu_ref, wd_ref = refs[:6]
    cast_src = refs[6:6 + n_jobs]
    o_ref = refs[6 + n_jobs]
    cast_dst = refs[7 + n_jobs:7 + 2 * n_jobs]
    h_ref, acc_ref = refs[7 + 2 * n_jobs:]
    f = pl.program_id(1)

    for src, dst in zip(cast_src, cast_dst):
        dst[...] = src[...].astype(BF16)

    @pl.when(f == 0)
    def _():
        g = g_ref[2 * k:2 * k + 1, :]
        m1 = 1.0 + mod_ref[3 * k:3 * k + 1, :]
        m2 = mod_ref[3 * k + 1:3 * k + 2, :]
        for r0 in range(0, TM, FFN_ROWS):
            rs = slice(r0, r0 + FFN_ROWS)
            h_ref[rs, :] = (_rms(x_ref[rs, :], g) * m1 + m2).astype(BF16)
        acc_ref[...] = jnp.zeros_like(acc_ref)

    h = h_ref[...]
    a = jnp.dot(h, wg_ref[...], preferred_element_type=F32)
    b = jnp.dot(h, wu_ref[...], preferred_element_type=F32)
    p = (_silu(a) * b).astype(BF16)
    acc_ref[...] += jnp.dot(p, wd_ref[...], preferred_element_type=F32)

    @pl.when(f == nf - 1)
    def _():
        g = g_ref[2 * k + 1:2 * k + 2, :]
        gate = 0.5 * mod_ref[3 * k + 2:3 * k + 3, :]
        for r0 in range(0, TM, FFN_ROWS):
            rs = slice(r0, r0 + FFN_ROWS)
            o_ref[rs, :] = x_ref[rs, :] + gate * _rms(acc_ref[rs, :], g)


def _ffn(x, mods, norm_g, wg, wu, wd, cast_jobs, *, layer, k, n_tiles):
    nf = D_FF // TF
    n_steps = n_tiles * nf
    in_specs = [
        pl.BlockSpec((TM, D_MODEL), lambda i, f: (i, 0)),
        pl.BlockSpec((None, None, N_MOD, D_MODEL), lambda i, f: (layer, _mod_row(i), 0, 0)),
        pl.BlockSpec((None, 6, D_MODEL), lambda i, f: (layer, 0, 0)),
        pl.BlockSpec((D_MODEL, TF), lambda i, f: (0, f)),
        pl.BlockSpec((D_MODEL, TF), lambda i, f: (0, f)),
        pl.BlockSpec((TF, D_MODEL), lambda i, f: (f, 0)),
    ]
    out_specs = [pl.BlockSpec((TM, D_MODEL), lambda i, f: (i, 0))]
    out_shape = [jax.ShapeDtypeStruct((n_tiles * TM, D_MODEL), F32)]
    for w, src_layer in cast_jobs:
        n_rows, n_cols = w.shape[1:]
        rb = BF16_SUBLANES * pl.cdiv(pl.cdiv(n_rows, n_steps), BF16_SUBLANES)
        assert n_rows % rb == 0
        last_blk = n_rows // rb - 1

        def step_block(i, f, last_blk=last_blk):
            return jnp.minimum(i * nf + f, last_blk)

        in_specs.append(pl.BlockSpec((None, rb, n_cols),
                                     lambda i, f, sl=src_layer, sb=step_block: (sl, sb(i, f), 0)))
        out_specs.append(pl.BlockSpec((rb, n_cols), lambda i, f, sb=step_block: (sb(i, f), 0)))
        out_shape.append(jax.ShapeDtypeStruct((n_rows, n_cols), BF16))
    outs = pl.pallas_call(
        functools.partial(_ffn_kernel, k=k, nf=nf, n_jobs=len(cast_jobs)),
        grid=(n_tiles, nf),
        in_specs=in_specs,
        out_specs=out_specs,
        out_shape=out_shape,
        scratch_shapes=[pltpu.VMEM((TM, D_MODEL), BF16), pltpu.VMEM((TM, D_MODEL), F32)],
        compiler_params=_params("arbitrary", "arbitrary"),
        name=f"ffn{k}",
    )(x, mods, norm_g, wg, wu, wd, *[w for w, _ in cast_jobs])
    return outs[0], list(outs[1:])


def _rope(t, cos, sa, sb):
    return t * cos + pltpu.roll(t, 3 * HEAD_DIM // 4, 1) * sa + pltpu.roll(t, HEAD_DIM // 4, 1) * sb


def _inproj_kernel(x_ref, mod_ref, g_ref, w_ref, cos_ref, sa_ref, sb_ref, o_ref):
    y = _rms(x_ref[...], g_ref[2:3, :])
    h = (y * (1.0 + mod_ref[3:4, :]) + mod_ref[4:5, :]).astype(BF16)
    cos, sa, sb = cos_ref[...], sa_ref[...], sb_ref[...]
    n_rope = (K_END - A_END) // HEAD_DIM
    for n in range(IN_COLS // TN_IN):
        c0 = n * TN_IN
        z = jnp.dot(h, w_ref[:, c0:c0 + TN_IN], preferred_element_type=F32)
        for hd in range(TN_IN // HEAD_DIM):
            col = c0 + hd * HEAD_DIM
            zs = z[:, hd * HEAD_DIM:(hd + 1) * HEAD_DIM]
            if A_END <= col < A_END + n_rope * HEAD_DIM:
                zs = _rope(zs, cos, sa, sb)
            o_ref[:, col:col + HEAD_DIM] = zs


def _inproj(x, mods, norm_g, w_in, cos, sa, sb, *, layer):
    n_tiles = N_TOK // TM
    tab = pl.BlockSpec((TM, HEAD_DIM), lambda i: (i, 0))
    return pl.pallas_call(
        _inproj_kernel,
        grid=(n_tiles,),
        in_specs=[
            pl.BlockSpec((TM, D_MODEL), lambda i: (i, 0)),
            pl.BlockSpec((None, None, N_MOD, D_MODEL), lambda i: (layer, _mod_row(i), 0, 0)),
            pl.BlockSpec((None, 6, D_MODEL), lambda i: (layer, 0, 0)),
            pl.BlockSpec((D_MODEL, IN_COLS), lambda i: (0, 0), pipeline_mode=pl.Buffered(1)),
            tab, tab, tab,
        ],
        out_specs=pl.BlockSpec((TM, IN_COLS), lambda i: (i, 0)),
        out_shape=jax.ShapeDtypeStruct((N_TOK, IN_COLS), F32),
        compiler_params=_params("arbitrary"),
        name="inproj",
    )(x, mods, norm_g, w_in, cos, sa, sb)


def _sgu_kernel(z_ref, g_ref, w_ref, b_ref, o_ref, *, rows):
    inv_sqrt2 = 1.0 / math.sqrt(2.0)
    for c in range(rows // CHUNK):
        rs = slice(c * CHUNK, (c + 1) * CHUNK)
        a = z_ref[rs, :]
        a = a * (lax.erf(a * inv_sqrt2) + 1.0) * 0.5
        u, v = a[:, :SGU_WIDTH], a[:, SGU_WIDTH:]
        mu = jnp.mean(v, axis=-1, keepdims=True)
        vc = v - mu
        var = jnp.mean(vc * vc, axis=-1, keepdims=True)
        v = (vc * lax.rsqrt(var + NORM_EPS) * g_ref[...]).astype(BF16)
        for g in range(SGU_GROUPS):
            cs = slice(g * HEAD_DIM, (g + 1) * HEAD_DIM)
            mixed = jnp.dot(w_ref[g].astype(BF16), v[:, cs], preferred_element_type=F32) + b_ref[:, g:g + 1]
            o_ref[rs, cs] = (u[:, cs] * mixed).astype(o_ref.dtype)


def _sgu(z, sgu_g, sgu_w, sgu_b, *, layer, n_rows):
    rows = TM
    return pl.pallas_call(
        functools.partial(_sgu_kernel, rows=rows),
        grid=(n_rows // rows,),
        in_specs=[
            pl.BlockSpec((rows, A_END), lambda i: (i, 0)),
            pl.BlockSpec((None, 1, SGU_WIDTH), lambda i: (layer, 0, 0)),
            pl.BlockSpec((None, SGU_GROUPS, CHUNK, CHUNK), lambda i: (layer, 0, 0, 0)),
            pl.BlockSpec((None, CHUNK, SGU_GROUPS), lambda i: (layer, 0, 0)),
        ],
        out_specs=pl.BlockSpec((rows, SGU_WIDTH), lambda i: (i, 0)),
        out_shape=jax.ShapeDtypeStruct((n_rows, SGU_WIDTH), BF16),
        compiler_params=_params("arbitrary"),
        name="sgu",
    )(z, sgu_g.reshape(DEPTH, 1, SGU_WIDTH), sgu_w, jnp.swapaxes(sgu_b, 1, 2))


def _attn_kernel(sink_ref, q_ref, kp_ref, kc_ref, kn_ref, vp_ref, vc_ref, vn_ref, kx_ref, vx_ref,
                 o_ref, *, layer, nb, n_lat_blocks):
    s_id = pl.program_id(0)
    is_lat = s_id < n_lat_blocks
    n = s_id % nb
    n_keys = CTX_LEN + 3 * BLOCK
    n_rows = Q_GROUP * BLOCK
    row_id = lax.broadcasted_iota(jnp.int32, (n_rows, 1), 0)
    row = lax.broadcasted_iota(jnp.int32, (n_rows, n_keys), 0) & (BLOCK - 1)
    col = lax.broadcasted_iota(jnp.int32, (n_rows, n_keys), 1)
    rel = col - (CTX_LEN + BLOCK)
    lo = jnp.where(is_lat, -n * BLOCK, 4 * BLOCK)
    hi = jnp.where(is_lat, (nb - n) * BLOCK, -4 * BLOCK)
    dist = rel - row
    valid = (col < CTX_LEN) | ((dist >= -WINDOW) & (dist <= WINDOW) & (rel >= lo) & (rel < hi))
    scale = HEAD_DIM ** -0.5
    nt = (((1,), (1,)), ((), ()))
    for kh in range(N_KV_HEADS):
        ks = slice(kh * HEAD_DIM, (kh + 1) * HEAD_DIM)
        keys = jnp.concatenate([kx_ref[:, ks], kp_ref[:, ks], kc_ref[:, ks], kn_ref[:, ks]], axis=0).astype(BF16)
        vals = jnp.concatenate([vx_ref[:, ks], vp_ref[:, ks], vc_ref[:, ks], vn_ref[:, ks]], axis=0).astype(BF16)
        heads = [kh * Q_GROUP + g for g in range(Q_GROUP)]
        q = jnp.concatenate([q_ref[:, hd * HEAD_DIM:(hd + 1) * HEAD_DIM] for hd in heads], axis=0).astype(BF16)
        sink = jnp.full((n_rows, 1), sink_ref[layer, heads[0]], F32)
        for g, hd in enumerate(heads[1:], start=1):
            sink = jnp.where(row_id >= g * BLOCK, sink_ref[layer, hd], sink)
        s = lax.dot_general(q, keys, nt, preferred_element_type=F32) * scale
        s = jnp.where(valid, s, NEG_INF)
        m = jnp.maximum(jnp.max(s, axis=-1, keepdims=True), sink)
        p = jnp.exp(s - m)
        denom = jnp.sum(p, axis=-1, keepdims=True) + jnp.exp(sink - m)
        probs = (p / denom).astype(BF16)
        out = jnp.dot(probs, vals, preferred_element_type=F32).astype(o_ref.dtype)
        for g, hd in enumerate(heads):
            o_ref[:, hd * HEAD_DIM:(hd + 1) * HEAD_DIM] = out[g * BLOCK:(g + 1) * BLOCK]


def _attention(z, attn_sink, *, layer, n_rows):
    nb = SEQ // BLOCK
    n_lat_blocks = N_LAT // BLOCK
    n_blocks = n_rows // BLOCK
    ctx_blocks = CTX_LEN // BLOCK
    q_col = A_END // ATTN_WIDTH
    k_col = Q_END // KV_WIDTH
    v_col = K_END // KV_WIDTH

    def lat(s):
        return s < n_lat_blocks

    def band(off, col):
        def index(s):
            b = s // nb
            nbr = jnp.clip(s % nb + off, 0, nb - 1) + b * nb
            return (jnp.where(lat(s), nbr, s), col)
        return pl.BlockSpec((BLOCK, KV_WIDTH), index)

    def ctx(col):
        def index(s):
            b = jnp.where(lat(s), s // nb, (s - n_lat_blocks) // ctx_blocks)
            return (N_LAT // CTX_LEN + b, col)
        return pl.BlockSpec((CTX_LEN, KV_WIDTH), index)

    return pl.pallas_call(
        functools.partial(_attn_kernel, layer=layer, nb=nb, n_lat_blocks=n_lat_blocks),
        grid=(n_blocks,),
        in_specs=[
            pl.BlockSpec(memory_space=pltpu.SMEM),
            pl.BlockSpec((BLOCK, ATTN_WIDTH), lambda s: (s, q_col)),
            band(-1, k_col), band(0, k_col), band(1, k_col),
            band(-1, v_col), band(0, v_col), band(1, v_col),
            ctx(k_col), ctx(v_col),
        ],
        out_specs=pl.BlockSpec((BLOCK, ATTN_WIDTH), lambda s: (s, 0)),
        out_shape=jax.ShapeDtypeStruct((n_rows, ATTN_WIDTH), BF16),
        compiler_params=_params("arbitrary"),
        name="attn",
    )(attn_sink, z, z, z, z, z, z, z, z, z)


def _short_conv(ref, w_ref, b_ref, r0, rows, seq):
    halo = 8
    lo = max(r0 - halo, 0)
    hi = min(r0 + rows + halo, seq)
    ext = ref[lo:hi, :]
    n_ext = hi - lo
    off = r0 - lo
    up = pltpu.roll(ext, 1, 0)[off:off + rows]
    dn = pltpu.roll(ext, n_ext - 1, 0)[off:off + rows]
    t = lax.broadcasted_iota(jnp.int32, (rows, 1), 0) + r0
    if r0 == 0:
        up = jnp.where(t == 0, 0.0, up)
    if r0 + rows == seq:
        dn = jnp.where(t == seq - 1, 0.0, dn)
    return up * w_ref[0:1, :] + ext[off:off + rows] * w_ref[1:2, :] + dn * w_ref[2:3, :] + b_ref[...]


def _alt_sign(r0, rows):
    t = lax.broadcasted_iota(jnp.int32, (rows, 1), 0) + r0
    return jnp.where((t & 1) == 0, 1.0, -1.0)


def _hy_chunks(seq, chunk=None):
    rows = min(chunk or HY_ROWS, seq)
    return [(r * rows, rows) for r in range(seq // rows)]


def _filt_kernel(feat_ref, w1_ref, b1_ref, w2_ref, b2_ref, w3_ref, fr_ref, dl_ref,
                 hb_ref, asum_ref, hnyq_ref, hd_ref, *, seq):
    hp = lax.Precision.HIGHEST

    @pl.when(pl.program_id(1) == 0)
    def _():
        fr = fr_ref[...]
        hd = jnp.sin(fr * (jnp.dot(feat_ref[...], w1_ref[...], precision=hp, preferred_element_type=F32)
                           + b1_ref[...]))
        hd_ref[...] = jnp.sin(fr * (jnp.dot(hd, w2_ref[...], precision=hp, preferred_element_type=F32)
                                    + b2_ref[...]))

    asum = jnp.zeros((1, HY_FILT_COLS), F32)
    hnyq = jnp.zeros((1, HY_FILT_COLS), F32)
    for r0, rows in _hy_chunks(seq):
        rs = slice(r0, r0 + rows)
        h = jnp.dot(hd_ref[rs, :], w3_ref[...], precision=hp, preferred_element_type=F32)
        h = h * jnp.exp(-feat_ref[rs, 0:1] * dl_ref[...])
        asum = asum + jnp.sum(jnp.abs(h), axis=0, keepdims=True)
        hnyq = hnyq + jnp.sum(h * _alt_sign(r0, rows), axis=0, keepdims=True)
        hb_ref[rs, :] = h.astype(BF16)
    asum_ref[...] = asum
    hnyq_ref[...] = hnyq


def _hyena_filters(seq, w1, b1, w2, b2, w3, freq):
    n_tiles = HY_WIDTH // HY_COLS
    w3 = w3.reshape(DEPTH, HY_HIDDEN, HY_ORDER, 2, n_tiles, HY_COLS)
    w3 = jnp.swapaxes(w3, 3, 4).reshape(DEPTH, HY_HIDDEN, HY_ORDER * 2 * HY_WIDTH)
    t = jnp.linspace(0.0, 1.0, seq, dtype=F32)[:, None]
    w = (2.0 * math.pi / seq) * jnp.arange(seq, dtype=F32)[:, None]
    bands = jnp.linspace(1e-4, HY_BANDS - 1, HY_BANDS, dtype=F32)[None, :]
    feat = jnp.concatenate([t, jnp.cos(bands * w), -jnp.sin(bands * w)], axis=-1)
    feat = jnp.pad(feat, ((0, 0), (0, HY_EMB_PAD - HY_EMB)))
    w1p = jnp.pad(w1, ((0, 0), (0, HY_EMB_PAD - HY_EMB), (0, 0)))
    deltas = jnp.abs(jnp.linspace(math.log(HY_TARGET) / HY_FAST_DECAY,
                                  math.log(HY_TARGET) / HY_SLOW_DECAY, HY_WIDTH, dtype=F32))
    deltas = jnp.broadcast_to(deltas.reshape(1, n_tiles, 1, HY_COLS), (HY_ORDER, n_tiles, 2, HY_COLS))
    deltas = deltas.reshape(1, HY_ORDER * 2 * HY_WIDTH)
    n_cols = HY_ORDER * 2 * HY_WIDTH

    def row(a):
        return a.reshape(DEPTH, 1, HY_HIDDEN)

    vec = pl.BlockSpec((None, 1, HY_HIDDEN), lambda l, j: (l, 0, 0))
    stat = pl.BlockSpec((None, 1, HY_FILT_COLS), lambda l, j: (l, 0, j))
    return pl.pallas_call(
        functools.partial(_filt_kernel, seq=seq),
        grid=(DEPTH, n_cols // HY_FILT_COLS),
        in_specs=[
            pl.BlockSpec((seq, HY_EMB_PAD), lambda l, j: (0, 0)),
            pl.BlockSpec((None, HY_EMB_PAD, HY_HIDDEN), lambda l, j: (l, 0, 0)),
            vec,
            pl.BlockSpec((None, HY_HIDDEN, HY_HIDDEN), lambda l, j: (l, 0, 0)),
            vec,
            pl.BlockSpec((None, HY_HIDDEN, HY_FILT_COLS), lambda l, j: (l, 0, j)),
            vec,
            pl.BlockSpec((1, HY_FILT_COLS), lambda l, j: (0, j)),
        ],
        out_specs=[pl.BlockSpec((None, seq, HY_FILT_COLS), lambda l, j: (l, 0, j)), stat, stat],
        out_shape=[jax.ShapeDtypeStruct((DEPTH, seq, n_cols), BF16),
                   jax.ShapeDtypeStruct((DEPTH, 1, n_cols), F32),
                   jax.ShapeDtypeStruct((DEPTH, 1, n_cols), F32)],
        scratch_shapes=[pltpu.VMEM((seq, HY_HIDDEN), F32)],
        compiler_params=_params("arbitrary", "arbitrary"),
        name=f"hy_filt_l{seq}",
    )(feat, w1p, row(b1), w2, row(b2), w3, row(freq), deltas)


def _dft_matrices(seq, kf):
    n_fft = 2 * seq
    n_ch = seq // kf
    th = 2.0 * math.pi / n_fft
    c = jnp.arange(n_ch, dtype=jnp.int32)[:, None]
    kl = jnp.arange(kf, dtype=jnp.int32)[:, None]
    n = jnp.arange(seq, dtype=jnp.int32)[None, :]
    a1 = th * ((c * kf * n) % n_fft).astype(F32)
    a2 = th * ((kl * n) % n_fft).astype(F32)
    c1, s1, c2, s2 = jnp.cos(a1), jnp.sin(a1), jnp.cos(a2), jnp.sin(a2)
    f_cos = c1[:, None, :] * c2[None, :, :] - s1[:, None, :] * s2[None, :, :]
    f_sin = s1[:, None, :] * c2[None, :, :] + c1[:, None, :] * s2[None, :, :]
    fmat = jnp.concatenate([f_cos, -f_sin], axis=1).astype(BF16)
    c2t, s2t = c2.T, s2.T
    g_cos = c1[:, :, None] * c2t[None, :, :] - s1[:, :, None] * s2t[None, :, :]
    g_sin = s1[:, :, None] * c2t[None, :, :] + c1[:, :, None] * s2t[None, :, :]
    gmat = jnp.concatenate([g_cos, -g_sin], axis=2).astype(BF16)
    return fmat, gmat


def _hy_pre_kernel(z_ref, w_ref, b_ref, ub_ref, nyq_ref, *, seq):
    nyq = jnp.zeros((1, HY_COLS), F32)
    for r0, rows in _hy_chunks(seq):
        v = _short_conv(z_ref, w_ref, b_ref, r0, rows, seq)
        ub_ref[r0:r0 + rows, :] = v.astype(BF16)
        nyq = nyq + jnp.sum(v * _alt_sign(r0, rows), axis=0, keepdims=True)
    nyq_ref[...] = nyq


def _hy_spec_kernel(u_ref, h_ref, a_ref, f_ref, g_ref, y_ref, *, seq, kf, n_fft):
    c = pl.program_id(1)
    cw = HY_COLS

    @pl.when(c == 0)
    def _():
        y_ref[...] = jnp.zeros_like(y_ref)

    f = f_ref[...]
    hh = jnp.dot(f, h_ref[...], preferred_element_type=F32)
    uu = jnp.dot(f, u_ref[...], preferred_element_type=F32)
    row = lax.broadcasted_iota(jnp.int32, (kf, 1), 0)
    a = a_ref[...]
    wgt = jnp.where((row == 0) & (c == 0), 1.0 / n_fft, 2.0 / n_fft) / (a[:, :cw] + a[:, cw:] + 1e-6)
    sr = (hh[:kf, :cw] + hh[:kf, cw:]) * wgt
    si = (hh[kf:, :cw] - hh[kf:, cw:]) * wgt
    p_re, p_im = [], []
    for b in range(BATCH):
        ur, ui = uu[:kf, b * cw:(b + 1) * cw], uu[kf:, b * cw:(b + 1) * cw]
        p_re.append(ur * sr - ui * si)
        p_im.append(ur * si + ui * sr)
    prod = jnp.concatenate([jnp.concatenate(p_re, axis=1), jnp.concatenate(p_im, axis=1)], axis=0).astype(BF16)
    for r0, rows in _hy_chunks(seq, HY_INV_ROWS):
        rs = slice(r0, r0 + rows)
        y_ref[rs, :] += jnp.dot(g_ref[rs, :], prod, preferred_element_type=F32)


def _nyq_term(unyq_ref, hnf_ref, hnb_ref, af_ref, ab_ref, n_fft):
    return unyq_ref[...] * (hnf_ref[...] + hnb_ref[...]) / (af_ref[...] + ab_ref[...] + 1e-6) * (1.0 / n_fft)


def _hy_mid_kernel(zv_ref, zx_ref, y_ref, unyq_ref, hnf_ref, hnb_ref, af_ref, ab_ref, wv_ref, bv_ref,
                   wx_ref, bx_ref, skip_ref, s_ref, sb_ref, snyq_ref, *, seq):
    pn = _nyq_term(unyq_ref, hnf_ref, hnb_ref, af_ref, ab_ref, 2 * seq)
    snyq = jnp.zeros((1, HY_COLS), F32)
    for r0, rows in _hy_chunks(seq):
        rs = slice(r0, r0 + rows)
        sign = _alt_sign(r0, rows)
        v = _short_conv(zv_ref, wv_ref, bv_ref, r0, rows, seq)
        x1 = _short_conv(zx_ref, wx_ref, bx_ref, r0, rows, seq)
        s = x1 * (y_ref[rs, :] + sign * pn + v * skip_ref[...])
        s_ref[rs, :] = s
        sb_ref[rs, :] = s.astype(BF16)
        snyq = snyq + jnp.sum(s * sign, axis=0, keepdims=True)
    snyq_ref[...] = snyq


def _hy_post_kernel(zx_ref, y_ref, s_ref, snyq_ref, hnf_ref, hnb_ref, af_ref, ab_ref, wx_ref, bx_ref,
                    skip_ref, *rest, seq):
    o_ref = rest[-1]
    pn = _nyq_term(snyq_ref, hnf_ref, hnb_ref, af_ref, ab_ref, 2 * seq)
    for r0, rows in _hy_chunks(seq):
        rs = slice(r0, r0 + rows)
        x2 = _short_conv(zx_ref, wx_ref, bx_ref, r0, rows, seq)
        y = y_ref[rs, :] + _alt_sign(r0, rows) * pn + s_ref[rs, :] * skip_ref[...]
        o_ref[rs, :] = (x2 * y).astype(o_ref.dtype)


def _hyena(z, z_rb0, conv_w, conv_b, skip, filt, fmat, gmat, *, layer, seq, out_rows, out_rb0,
           alias_out=None):
    hb, asum, hnyq = filt
    kf = min(HY_KF, seq)
    n_ch = seq // kf
    n_cols = HY_WIDTH // HY_COLS
    hy0 = V_END // HY_COLS

    def zcols(which):
        return pl.BlockSpec((seq, HY_COLS), lambda b, j: (z_rb0 + b, hy0 + which * n_cols + j))

    def cw(which):
        return pl.BlockSpec((None, HY_SHORT, HY_COLS), lambda b, j: (layer, 0, which * n_cols + j))

    def cb(which):
        return pl.BlockSpec((None, 1, HY_COLS), lambda b, j: (layer, 0, which * n_cols + j))

    def fstat(order, direction):
        return pl.BlockSpec((None, 1, HY_COLS), lambda b, j: (layer, 0, (order * n_cols + j) * 2 + direction))

    def skip_spec(order):
        return pl.BlockSpec((None, 1, HY_COLS), lambda b, j: (layer * HY_ORDER + order, 0, j))

    seq_blk = pl.BlockSpec((seq, HY_COLS), lambda b, j: (0, j * BATCH + b))
    nyq_spec = pl.BlockSpec((1, HY_COLS), lambda b, j: (0, j * BATCH + b))
    ew_params = _params("arbitrary", "arbitrary")
    act = jax.ShapeDtypeStruct((seq, BATCH * HY_WIDTH), F32)
    act_b = jax.ShapeDtypeStruct((seq, BATCH * HY_WIDTH), BF16)
    nyq_shape = jax.ShapeDtypeStruct((1, BATCH * HY_WIDTH), F32)

    def spectral(ub, order):
        return pl.pallas_call(
            functools.partial(_hy_spec_kernel, seq=seq, kf=kf, n_fft=2 * seq),
            grid=(n_cols, n_ch),
            in_specs=[
                pl.BlockSpec((seq, BATCH * HY_COLS), lambda j, c: (0, j), pipeline_mode=pl.Buffered(1)),
                pl.BlockSpec((None, seq, 2 * HY_COLS), lambda j, c: (layer, 0, order * n_cols + j),
                             pipeline_mode=pl.Buffered(1)),
                pl.BlockSpec((None, 1, 2 * HY_COLS), lambda j, c: (layer, 0, order * n_cols + j)),
                pl.BlockSpec((None, 2 * kf, seq), lambda j, c: (c, 0, 0)),
                pl.BlockSpec((None, seq, 2 * kf), lambda j, c: (c, 0, 0)),
            ],
            out_specs=pl.BlockSpec((seq, BATCH * HY_COLS), lambda j, c: (0, j)),
            out_shape=act,
            compiler_params=_params("arbitrary", "arbitrary"),
            name=f"hy_spec{order}_l{seq}",
        )(ub, hb, asum, fmat, gmat)

    ub, unyq = pl.pallas_call(
        functools.partial(_hy_pre_kernel, seq=seq),
        grid=(BATCH, n_cols),
        in_specs=[zcols(0), cw(0), cb(0)],
        out_specs=[seq_blk, nyq_spec],
        out_shape=[act_b, nyq_shape],
        compiler_params=ew_params,
        name=f"hy_pre_l{seq}",
    )(z, conv_w, conv_b)
    y0 = spectral(ub, 0)
    s, sb, snyq = pl.pallas_call(
        functools.partial(_hy_mid_kernel, seq=seq),
        grid=(BATCH, n_cols),
        in_specs=[zcols(0), zcols(1), seq_blk, nyq_spec, fstat(0, 0), fstat(0, 1), fstat(0, 0), fstat(0, 1),
                  cw(0), cb(0), cw(1), cb(1), skip_spec(0)],
        out_specs=[seq_blk, seq_blk, nyq_spec],
        out_shape=[act, act_b, nyq_shape],
        compiler_params=ew_params,
        name=f"hy_mid_l{seq}",
    )(z, z, y0, unyq, hnyq, hnyq, asum, asum, conv_w, conv_b, conv_w, conv_b, skip)
    y1 = spectral(sb, 1)
    in_specs = [zcols(2), seq_blk, seq_blk, nyq_spec, fstat(1, 0), fstat(1, 1), fstat(1, 0), fstat(1, 1),
                cw(2), cb(2), skip_spec(1)]
    args = [z, y1, s, snyq, hnyq, hnyq, asum, asum, conv_w, conv_b, skip]
    aliases = {}
    if alias_out is not None:
        in_specs.append(pl.BlockSpec(memory_space=pl.ANY))
        args.append(alias_out)
        aliases = {len(args) - 1: 0}
    return pl.pallas_call(
        functools.partial(_hy_post_kernel, seq=seq),
        grid=(BATCH, n_cols),
        in_specs=in_specs,
        out_specs=pl.BlockSpec((seq, HY_COLS), lambda b, j: (out_rb0 + b, j)),
        out_shape=jax.ShapeDtypeStruct((out_rows, HY_WIDTH), BF16),
        input_output_aliases=aliases,
        compiler_params=ew_params,
        name=f"hy_post_l{seq}",
    )(*args)


def _outproj_kernel(a_ref, b_ref, c_ref, w_ref, x_ref, mod_ref, g_ref, o_ref):
    y = jnp.dot(a_ref[...], w_ref[:SGU_WIDTH, :], preferred_element_type=F32)
    y += jnp.dot(b_ref[...], w_ref[SGU_WIDTH:SGU_WIDTH + ATTN_WIDTH, :], preferred_element_type=F32)
    y += jnp.dot(c_ref[...], w_ref[SGU_WIDTH + ATTN_WIDTH:, :], preferred_element_type=F32)
    o_ref[...] = x_ref[...] + mod_ref[5:6, :] * _rms(y, g_ref[3:4, :])


def _outproj(a, b, c, w_out, x, mods, norm_g, *, layer, n_tiles):
    return pl.pallas_call(
        _outproj_kernel,
        grid=(n_tiles,),
        in_specs=[
            pl.BlockSpec((TM, SGU_WIDTH), lambda i: (i, 0)),
            pl.BlockSpec((TM, ATTN_WIDTH), lambda i: (i, 0)),
            pl.BlockSpec((TM, HY_WIDTH), lambda i: (i, 0)),
            pl.BlockSpec((D_MIX, D_MODEL), lambda i: (0, 0), pipeline_mode=pl.Buffered(1)),
            pl.BlockSpec((TM, D_MODEL), lambda i: (i, 0)),
            pl.BlockSpec((None, None, N_MOD, D_MODEL), lambda i: (layer, _mod_row(i), 0, 0)),
            pl.BlockSpec((None, 6, D_MODEL), lambda i: (layer, 0, 0)),
        ],
        out_specs=pl.BlockSpec((TM, D_MODEL), lambda i: (i, 0)),
        out_shape=jax.ShapeDtypeStruct((n_tiles * TM, D_MODEL), F32),
        compiler_params=_params("arbitrary"),
        name="outproj",
    )(a, b, c, w_out, x, mods, norm_g)


def _rope_tables():
    rows = SEQ // GRID_W
    row = jnp.broadcast_to(jnp.arange(rows)[:, None], (rows, GRID_W)).reshape(SEQ).astype(F32)
    col = jnp.broadcast_to(jnp.arange(GRID_W)[None, :], (rows, GRID_W)).reshape(SEQ).astype(F32)
    half = HEAD_DIM // 2
    inv = ROPE_BASE ** (-jnp.arange(0, half, 2, dtype=F32) / half)
    ang_r = row[:, None] * inv
    ang_c = col[:, None] * inv
    ang = jnp.concatenate([ang_r, ang_r, ang_c, ang_c], axis=-1)
    cos, sin = jnp.cos(ang), jnp.sin(ang)
    first = (jnp.arange(HEAD_DIM) % half) < (half // 2)
    sa = jnp.where(first, -sin, 0.0)
    sb = jnp.where(first, 0.0, sin)

    def stack(lat, ctx_value):
        return jnp.concatenate([lat] * BATCH + [jnp.full((N_CTX, HEAD_DIM), ctx_value, F32)], axis=0)

    return stack(cos, 1.0), stack(sa, 0.0), stack(sb, 0.0)


def kernel(x, c, ctx, c_ctx, w_ada, b_ada, norm_g, ffn1_gate, ffn1_up, ffn1_down, w_in,
           sgu_g, sgu_w, sgu_b, attn_sink, hy_conv_w, hy_conv_b, hy_f_w1, hy_f_b1, hy_f_w2,
           hy_f_b2, hy_f_w3, hy_freq, hy_skip, w_out, ffn2_gate, ffn2_up, ffn2_down):
    xs = jnp.concatenate([x.reshape(N_LAT, D_MODEL), ctx.reshape(N_CTX, D_MODEL)], axis=0)
    cc = jnp.concatenate([c, c_ctx[None], jnp.zeros((MOD_ROWS - BATCH - 1, D_MODEL), F32)], axis=0)
    mods = _ada_mods(cc, w_ada, b_ada)

    w1 = [w[0].astype(BF16) for w in (ffn1_gate, ffn1_up, ffn1_down)]
    cos, sa, sb = _rope_tables()
    conv_b = hy_conv_b.reshape(DEPTH, 1, (HY_ORDER + 1) * HY_WIDTH)
    skip = hy_skip.reshape(DEPTH * HY_ORDER, 1, HY_WIDTH)
    filt_w = (hy_f_w1, hy_f_b1, hy_f_w2, hy_f_b2, hy_f_w3, hy_freq)
    filt_lat = _hyena_filters(SEQ, *filt_w)
    filt_ctx = _hyena_filters(CTX_LEN, *filt_w)
    f_lat, g_lat = _dft_matrices(SEQ, min(HY_KF, SEQ))
    f_ctx, g_ctx = _dft_matrices(CTX_LEN, min(HY_KF, CTX_LEN))

    all_tiles = N_TOK // TM
    lat_tiles = N_LAT // TM
    ctx_rb = N_LAT // CTX_LEN

    for i in range(DEPTH):
        last = i == DEPTH - 1
        n_tiles = lat_tiles if last else all_tiles
        n_rows = n_tiles * TM

        jobs = [(w, i) for w in (ffn2_gate, ffn2_up, ffn2_down, w_in, w_out)]
        xs, (w2g, w2u, w2d, w_in_b, w_out_b) = _ffn(xs, mods, norm_g, *w1, jobs, layer=i, k=0, n_tiles=all_tiles)
        z = _inproj(xs, mods, norm_g, w_in_b, cos, sa, sb, layer=i)

        a_out = _sgu(z, sgu_g, sgu_w, sgu_b, layer=i, n_rows=n_rows)
        b_out = _attention(z, attn_sink, layer=i, n_rows=n_rows)
        c_out = _hyena(z, 0, hy_conv_w, conv_b, skip, filt_lat, f_lat, g_lat, layer=i, seq=SEQ,
                       out_rows=n_rows, out_rb0=0)
        if not last:
            c_out = _hyena(z, ctx_rb, hy_conv_w, conv_b, skip, filt_ctx, f_ctx, g_ctx, layer=i, seq=CTX_LEN,
                           out_rows=n_rows, out_rb0=ctx_rb, alias_out=c_out)

        xs = _outproj(a_out, b_out, c_out, w_out_b, xs, mods, norm_g, layer=i, n_tiles=n_tiles)
        jobs = [] if last else [(w, i + 1) for w in (ffn1_gate, ffn1_up, ffn1_down)]
        xs, w1 = _ffn(xs, mods, norm_g, w2g, w2u, w2d, jobs, layer=i, k=2, n_tiles=n_tiles)

    return xs[:N_LAT].reshape(BATCH, SEQ, D_MODEL)
```

```python
import functools
import math

import jax
import jax.numpy as jnp
from jax import lax
from jax.experimental import pallas as pl
from jax.experimental.pallas import tpu as pltpu

D_MODEL = 2048
BATCH = 2
SEQ = 4096
DEPTH = 4
GRID_W = 64
CTX_LEN = 256
HEAD_DIM = 128
NORM_EPS = 1e-6
N_MOD = 9
D_FF = 5632
SGU_GROUPS = 4
SGU_WIDTH = SGU_GROUPS * HEAD_DIM
CHUNK = 128
N_Q_HEADS = 8
N_KV_HEADS = 2
Q_GROUP = N_Q_HEADS // N_KV_HEADS
ATTN_WIDTH = N_Q_HEADS * HEAD_DIM
KV_WIDTH = N_KV_HEADS * HEAD_DIM
WINDOW = 128
BLOCK = 128
ROPE_BASE = 10000.0
NEG_INF = -1e30
HY_GROUPS = 4
HY_WIDTH = HY_GROUPS * HEAD_DIM
HY_ORDER = 2
HY_SHORT = 3
HY_BANDS = 16
HY_EMB = 1 + 2 * HY_BANDS
HY_HIDDEN = 64
HY_TARGET = 1e-2
HY_FAST_DECAY = 0.3
HY_SLOW_DECAY = 1.5
D_MIX = SGU_WIDTH + ATTN_WIDTH + HY_WIDTH
A_END = 2 * SGU_WIDTH
Q_END = A_END + ATTN_WIDTH
K_END = Q_END + KV_WIDTH
V_END = K_END + KV_WIDTH
IN_COLS = V_END + (HY_ORDER + 1) * HY_WIDTH

N_LAT = BATCH * SEQ
N_CTX = BATCH * CTX_LEN
N_TOK = N_LAT + N_CTX
MOD_ROWS = 8

TM = 512
TF = 512
FFN_ROWS = 32
BF16_SUBLANES = 16
TN_IN = 1024
TN_ADA = 1024
HY_COLS = 256
HY_FILT_COLS = 2 * HY_COLS
HY_KF = 256
HY_ROWS = 512
HY_INV_ROWS = 1024
HY_EMB_PAD = 128
VMEM_LIMIT = 56 * 1024 * 1024

F32 = jnp.float32
BF16 = jnp.bfloat16


def _params(*sem):
    return pltpu.CompilerParams(dimension_semantics=sem, vmem_limit_bytes=VMEM_LIMIT)


def _mod_row(i):
    return jnp.minimum(i // (SEQ // TM), BATCH)


def _silu(a):
    return a / (1.0 + jnp.exp(-a))


def _rms(x, g):
    return x * lax.rsqrt(jnp.mean(x * x, axis=-1, keepdims=True) + NORM_EPS) * g


def _ada_kernel(c_ref, w_ref, b_ref, o_ref):
    s = _silu(c_ref[...]).astype(BF16)
    o_ref[...] = jnp.dot(s, w_ref[...].astype(BF16), preferred_element_type=F32) + b_ref[...]


def _ada_mods(cc, w_ada, b_ada):
    n_out = N_MOD * D_MODEL
    out = pl.pallas_call(
        _ada_kernel,
        grid=(DEPTH, n_out // TN_ADA),
        in_specs=[
            pl.BlockSpec((MOD_ROWS, D_MODEL), lambda l, j: (0, 0)),
            pl.BlockSpec((None, D_MODEL, TN_ADA), lambda l, j: (l, 0, j)),
            pl.BlockSpec((None, 1, TN_ADA), lambda l, j: (l, 0, j)),
        ],
        out_specs=pl.BlockSpec((None, MOD_ROWS, TN_ADA), lambda l, j: (l, 0, j)),
        out_shape=jax.ShapeDtypeStruct((DEPTH, MOD_ROWS, n_out), F32),
        compiler_params=_params("arbitrary", "arbitrary"),
        name="ada_mods",
    )(cc, w_ada, b_ada.reshape(DEPTH, 1, n_out))
    return out.reshape(DEPTH, MOD_ROWS, N_MOD, D_MODEL)


def _ffn_kernel(*refs, k, nf, n_jobs):
    x_ref, mod_ref, g_ref, wg_ref, wu_ref, wd_ref = refs[:6]
    cast_src = refs[6:6 + n_jobs]
    o_ref = refs[6 + n_jobs]
    cast_dst = refs[7 + n_jobs:7 + 2 * n_jobs]
    h_ref, acc_ref = refs[7 + 2 * n_jobs:]
    f = pl.program_id(1)

    for src, dst in zip(cast_src, cast_dst):
        dst[...] = src[...].astype(BF16)

    @pl.when(f == 0)
    def _():
        g = g_ref[2 * k:2 * k + 1, :]
        m1 = 1.0 + mod_ref[3 * k:3 * k + 1, :]
        m2 = mod_ref[3 * k + 1:3 * k + 2, :]
        for r0 in range(0, TM, FFN_ROWS):
            rs = slice(r0, r0 + FFN_ROWS)
            h_ref[rs, :] = (_rms(x_ref[rs, :], g) * m1 + m2).astype(BF16)
        acc_ref[...] = jnp.zeros_like(acc_ref)

    h = h_ref[...]
    a = jnp.dot(h, wg_ref[...], preferred_element_type=F32)
    b = jnp.dot(h, wu_ref[...], preferred_element_type=F32)
    p = (_silu(a) * b).astype(BF16)
    acc_ref[...] += jnp.dot(p, wd_ref[...], preferred_element_type=F32)

    @pl.when(f == nf - 1)
    def _():
        g = g_ref[2 * k + 1:2 * k + 2, :]
        gate = 0.5 * mod_ref[3 * k + 2:3 * k + 3, :]
        for r0 in range(0, TM, FFN_ROWS):
            rs = slice(r0, r0 + FFN_ROWS)
            o_ref[rs, :] = x_ref[rs, :] + gate * _rms(acc_ref[rs, :], g)


def _ffn(x, mods, norm_g, wg, wu, wd, cast_jobs, *, layer, k, n_tiles):
    nf = D_FF // TF
    n_steps = n_tiles * nf
    in_specs = [
        pl.BlockSpec((TM, D_MODEL), lambda i, f: (i, 0)),
        pl.BlockSpec((None, None, N_MOD, D_MODEL), lambda i, f: (layer, _mod_row(i), 0, 0)),
        pl.BlockSpec((None, 6, D_MODEL), lambda i, f: (layer, 0, 0)),
        pl.BlockSpec((D_MODEL, TF), lambda i, f: (0, f)),
        pl.BlockSpec((D_MODEL, TF), lambda i, f: (0, f)),
        pl.BlockSpec((TF, D_MODEL), lambda i, f: (f, 0)),
    ]
    out_specs = [pl.BlockSpec((TM, D_MODEL), lambda i, f: (i, 0))]
    out_shape = [jax.ShapeDtypeStruct((n_tiles * TM, D_MODEL), F32)]
    for w, src_layer in cast_jobs:
        n_rows, n_cols = w.shape[1:]
        rb = BF16_SUBLANES * pl.cdiv(pl.cdiv(n_rows, n_steps), BF16_SUBLANES)
        assert n_rows % rb == 0
        last_blk = n_rows // rb - 1

        def step_block(i, f, last_blk=last_blk):
            return jnp.minimum(i * nf + f, last_blk)

        in_specs.append(pl.BlockSpec((None, rb, n_cols),
                                     lambda i, f, sl=src_layer, sb=step_block: (sl, sb(i, f), 0)))
        out_specs.append(pl.BlockSpec((rb, n_cols), lambda i, f, sb=step_block: (sb(i, f), 0)))
        out_shape.append(jax.ShapeDtypeStruct((n_rows, n_cols), BF16))
    outs = pl.pallas_call(
        functools.partial(_ffn_kernel, k=k, nf=nf, n_jobs=len(cast_jobs)),
        grid=(n_tiles, nf),
        in_specs=in_specs,
        out_specs=out_specs,
        out_shape=out_shape,
        scratch_shapes=[pltpu.VMEM((TM, D_MODEL), BF16), pltpu.VMEM((TM, D_MODEL), F32)],
        compiler_params=_params("arbitrary", "arbitrary"),
        name=f"ffn{k}",
    )(x, mods, norm_g, wg, wu, wd, *[w for w, _ in cast_jobs])
    return outs[0], list(outs[1:])


def _rope(t, cos, sa, sb):
    return t * cos + pltpu.roll(t, 3 * HEAD_DIM // 4, 1) * sa + pltpu.roll(t, HEAD_DIM // 4, 1) * sb


def _inproj_kernel(x_ref, mod_ref, g_ref, w_ref, cos_ref, sa_ref, sb_ref, o_ref):
    y = _rms(x_ref[...], g_ref[2:3, :])
    h = (y * (1.0 + mod_ref[3:4, :]) + mod_ref[4:5, :]).astype(BF16)
    cos, sa, sb = cos_ref[...], sa_ref[...], sb_ref[...]
    n_rope = (K_END - A_END) // HEAD_DIM
    for n in range(IN_COLS // TN_IN):
        c0 = n * TN_IN
        z = jnp.dot(h, w_ref[:, c0:c0 + TN_IN], preferred_element_type=F32)
        for hd in range(TN_IN // HEAD_DIM):
            col = c0 + hd * HEAD_DIM
            zs = z[:, hd * HEAD_DIM:(hd + 1) * HEAD_DIM]
            if A_END <= col < A_END + n_rope * HEAD_DIM:
                zs = _rope(zs, cos, sa, sb)
            o_ref[:, col:col + HEAD_DIM] = zs


def _inproj(x, mods, norm_g, w_in, cos, sa, sb, *, layer):
    n_tiles = N_TOK // TM
    tab = pl.BlockSpec((TM, HEAD_DIM), lambda i: (i, 0))
    return pl.pallas_call(
        _inproj_kernel,
        grid=(n_tiles,),
        in_specs=[
            pl.BlockSpec((TM, D_MODEL), lambda i: (i, 0)),
            pl.BlockSpec((None, None, N_MOD, D_MODEL), lambda i: (layer, _mod_row(i), 0, 0)),
            pl.BlockSpec((None, 6, D_MODEL), lambda i: (layer, 0, 0)),
            pl.BlockSpec((D_MODEL, IN_COLS), lambda i: (0, 0), pipeline_mode=pl.Buffered(1)),
            tab, tab, tab,
        ],
        out_specs=pl.BlockSpec((TM, IN_COLS), lambda i: (i, 0)),
        out_shape=jax.ShapeDtypeStruct((N_TOK, IN_COLS), F32),
        compiler_params=_params("arbitrary"),
        name="inproj",
    )(x, mods, norm_g, w_in, cos, sa, sb)


def _sgu_kernel(z_ref, g_ref, w_ref, b_ref, o_ref, *, rows):
    inv_sqrt2 = 1.0 / math.sqrt(2.0)
    for c in range(rows // CHUNK):
        rs = slice(c * CHUNK, (c + 1) * CHUNK)
        a = z_ref[rs, :]
        a = a * (lax.erf(a * inv_sqrt2) + 1.0) * 0.5
        u, v = a[:, :SGU_WIDTH], a[:, SGU_WIDTH:]
        mu = jnp.mean(v, axis=-1, keepdims=True)
        vc = v - mu
        var = jnp.mean(vc * vc, axis=-1, keepdims=True)
        v = (vc * lax.rsqrt(var + NORM_EPS) * g_ref[...]).astype(BF16)
        for g in range(SGU_GROUPS):
            cs = slice(g * HEAD_DIM, (g + 1) * HEAD_DIM)
            mixed = jnp.dot(w_ref[g].astype(BF16), v[:, cs], preferred_element_type=F32) + b_ref[:, g:g + 1]
            o_ref[rs, cs] = (u[:, cs] * mixed).astype(o_ref.dtype)


def _sgu(z, sgu_g, sgu_w, sgu_b, *, layer, n_rows):
    rows = TM
    return pl.pallas_call(
        functools.partial(_sgu_kernel, rows=rows),
        grid=(n_rows // rows,),
        in_specs=[
            pl.BlockSpec((rows, A_END), lambda i: (i, 0)),
            pl.BlockSpec((None, 1, SGU_WIDTH), lambda i: (layer, 0, 0)),
            pl.BlockSpec((None, SGU_GROUPS, CHUNK, CHUNK), lambda i: (layer, 0, 0, 0)),
            pl.BlockSpec((None, CHUNK, SGU_GROUPS), lambda i: (layer, 0, 0)),
        ],
        out_specs=pl.BlockSpec((rows, SGU_WIDTH), lambda i: (i, 0)),
        out_shape=jax.ShapeDtypeStruct((n_rows, SGU_WIDTH), BF16),
        compiler_params=_params("arbitrary"),
        name="sgu",
    )(z, sgu_g.reshape(DEPTH, 1, SGU_WIDTH), sgu_w, jnp.swapaxes(sgu_b, 1, 2))


def _attn_kernel(sink_ref, q_ref, kp_ref, kc_ref, kn_ref, vp_ref, vc_ref, vn_ref, kx_ref, vx_ref,
                 o_ref, *, layer, nb, n_lat_blocks):
    s_id = pl.program_id(0)
    is_lat = s_id < n_lat_blocks
    n = s_id % nb
    n_keys = CTX_LEN + 3 * BLOCK
    n_rows = Q_GROUP * BLOCK
    row_id = lax.broadcasted_iota(jnp.int32, (n_rows, 1), 0)
    row = lax.broadcasted_iota(jnp.int32, (n_rows, n_keys), 0) & (BLOCK - 1)
    col = lax.broadcasted_iota(jnp.int32, (n_rows, n_keys), 1)
    rel = col - (CTX_LEN + BLOCK)
    lo = jnp.where(is_lat, -n * BLOCK, 4 * BLOCK)
    hi = jnp.where(is_lat, (nb - n) * BLOCK, -4 * BLOCK)
    dist = rel - row
    valid = (col < CTX_LEN) | ((dist >= -WINDOW) & (dist <= WINDOW) & (rel >= lo) & (rel < hi))
    scale = HEAD_DIM ** -0.5
    nt = (((1,), (1,)), ((), ()))
    for kh in range(N_KV_HEADS):
        ks = slice(kh * HEAD_DIM, (kh + 1) * HEAD_DIM)
        keys = jnp.concatenate([kx_ref[:, ks], kp_ref[:, ks], kc_ref[:, ks], kn_ref[:, ks]], axis=0).astype(BF16)
        vals = jnp.concatenate([vx_ref[:, ks], vp_ref[:, ks], vc_ref[:, ks], vn_ref[:, ks]], axis=0).astype(BF16)
        heads = [kh * Q_GROUP + g for g in range(Q_GROUP)]
        q = jnp.concatenate([q_ref[:, hd * HEAD_DIM:(hd + 1) * HEAD_DIM] for hd in heads], axis=0).astype(BF16)
        sink = jnp.full((n_rows, 1), sink_ref[layer, heads[0]], F32)
        for g, hd in enumerate(heads[1:], start=1):
            sink = jnp.where(row_id >= g * BLOCK, sink_ref[layer, hd], sink)
        s = lax.dot_general(q, keys, nt, preferred_element_type=F32) * scale
        s = jnp.where(valid, s, NEG_INF)
        m = jnp.maximum(jnp.max(s, axis=-1, keepdims=True), sink)
        p = jnp.exp(s - m)
        denom = jnp.sum(p, axis=-1, keepdims=True) + jnp.exp(sink - m)
        probs = (p / denom).astype(BF16)
        out = jnp.dot(probs, vals, preferred_element_type=F32).astype(o_ref.dtype)
        for g, hd in enumerate(heads):
            o_ref[:, hd * HEAD_DIM:(hd + 1) * HEAD_DIM] = out[g * BLOCK:(g + 1) * BLOCK]


def _attention(z, attn_sink, *, layer, n_rows):
    nb = SEQ // BLOCK
    n_lat_blocks = N_LAT // BLOCK
    n_blocks = n_rows // BLOCK
    ctx_blocks = CTX_LEN // BLOCK
    q_col = A_END // ATTN_WIDTH
    k_col = Q_END // KV_WIDTH
    v_col = K_END // KV_WIDTH

    def lat(s):
        return s < n_lat_blocks

    def band(off, col):
        def index(s):
            b = s // nb
            nbr = jnp.clip(s % nb + off, 0, nb - 1) + b * nb
            return (jnp.where(lat(s), nbr, s), col)
        return pl.BlockSpec((BLOCK, KV_WIDTH), index)

    def ctx(col):
        def index(s):
            b = jnp.where(lat(s), s // nb, (s - n_lat_blocks) // ctx_blocks)
            return (N_LAT // CTX_LEN + b, col)
        return pl.BlockSpec((CTX_LEN, KV_WIDTH), index)

    return pl.pallas_call(
        functools.partial(_attn_kernel, layer=layer, nb=nb, n_lat_blocks=n_lat_blocks),
        grid=(n_blocks,),
        in_specs=[
            pl.BlockSpec(memory_space=pltpu.SMEM),
            pl.BlockSpec((BLOCK, ATTN_WIDTH), lambda s: (s, q_col)),
            band(-1, k_col), band(0, k_col), band(1, k_col),
            band(-1, v_col), band(0, v_col), band(1, v_col),
            ctx(k_col), ctx(v_col),
        ],
        out_specs=pl.BlockSpec((BLOCK, ATTN_WIDTH), lambda s: (s, 0)),
        out_shape=jax.ShapeDtypeStruct((n_rows, ATTN_WIDTH), BF16),
        compiler_params=_params("arbitrary"),
        name="attn",
    )(attn_sink, z, z, z, z, z, z, z, z, z)


def _short_conv_parity(ref, w_ref, b_ref, m0, rows, half):
    ze = ref[pl.ds(2 * m0, rows, stride=2), :]
    zo = ref[pl.ds(2 * m0 + 1, rows, stride=2), :]
    m = lax.broadcasted_iota(jnp.int32, (rows, 1), 0) + m0
    if m0 == 0:
        zo_prev = jnp.where(m == 0, 0.0, pltpu.roll(zo, 1, 0))
    else:
        zo_prev = ref[pl.ds(2 * m0 - 1, rows, stride=2), :]
    if m0 + rows == half:
        ze_next = jnp.where(m == half - 1, 0.0, pltpu.roll(ze, rows - 1, 0))
    else:
        ze_next = ref[pl.ds(2 * m0 + 2, rows, stride=2), :]
    w0, w1, w2, b = w_ref[0:1, :], w_ref[1:2, :], w_ref[2:3, :], b_ref[...]
    return zo_prev * w0 + ze * w1 + zo * w2 + b, ze * w0 + zo * w1 + ze_next * w2 + b


def _alt_sign(r0, rows):
    t = lax.broadcasted_iota(jnp.int32, (rows, 1), 0) + r0
    return jnp.where((t & 1) == 0, 1.0, -1.0)


def _hy_chunks(seq, chunk=None):
    rows = min(chunk or HY_ROWS, seq)
    return [(r * rows, rows) for r in range(seq // rows)]


def _filt_kernel(feat_ref, w1_ref, b1_ref, w2_ref, b2_ref, w3_ref, fr_ref, dl_ref,
                 hb_ref, asum_ref, hnyq_ref, hd_ref, *, seq):
    hp = lax.Precision.HIGHEST

    @pl.when(pl.program_id(1) == 0)
    def _():
        fr = fr_ref[...]
        hd = jnp.sin(fr * (jnp.dot(feat_ref[...], w1_ref[...], precision=hp, preferred_element_type=F32)
                           + b1_ref[...]))
        hd_ref[...] = jnp.sin(fr * (jnp.dot(hd, w2_ref[...], precision=hp, preferred_element_type=F32)
                                    + b2_ref[...]))

    half = seq // 2
    asum = jnp.zeros((1, HY_FILT_COLS), F32)
    alt = [jnp.zeros((1, HY_FILT_COLS), F32), jnp.zeros((1, HY_FILT_COLS), F32)]
    for r0, rows in _hy_chunks(seq, min(HY_ROWS, half)):
        rs = slice(r0, r0 + rows)
        h = jnp.dot(hd_ref[rs, :], w3_ref[...], precision=hp, preferred_element_type=F32)
        h = h * jnp.exp(-feat_ref[rs, 0:1] * dl_ref[...])
        asum = asum + jnp.sum(jnp.abs(h), axis=0, keepdims=True)
        alt[r0 // half] = alt[r0 // half] + jnp.sum(h * _alt_sign(r0 % half, rows), axis=0, keepdims=True)
        hb_ref[rs, :] = h.astype(BF16)
    asum_ref[...] = asum
    hnyq_ref[0:1, :] = alt[0]
    hnyq_ref[1:2, :] = alt[1]


def _hyena_filters(seq, w1, b1, w2, b2, w3, freq):
    n_tiles = HY_WIDTH // HY_COLS
    w3 = w3.reshape(DEPTH, HY_HIDDEN, HY_ORDER, 2, n_tiles, HY_COLS)
    w3 = jnp.swapaxes(w3, 3, 4).reshape(DEPTH, HY_HIDDEN, HY_ORDER * 2 * HY_WIDTH)
    t = jnp.linspace(0.0, 1.0, seq, dtype=F32)[:, None]
    w = (2.0 * math.pi / seq) * jnp.arange(seq, dtype=F32)[:, None]
    bands = jnp.linspace(1e-4, HY_BANDS - 1, HY_BANDS, dtype=F32)[None, :]
    feat = jnp.concatenate([t, jnp.cos(bands * w), -jnp.sin(bands * w)], axis=-1)
    feat = jnp.pad(feat, ((0, 0), (0, HY_EMB_PAD - HY_EMB)))
    feat = jnp.concatenate([feat[0::2], feat[1::2]], axis=0)
    w1p = jnp.pad(w1, ((0, 0), (0, HY_EMB_PAD - HY_EMB), (0, 0)))
    deltas = jnp.abs(jnp.linspace(math.log(HY_TARGET) / HY_FAST_DECAY,
                                  math.log(HY_TARGET) / HY_SLOW_DECAY, HY_WIDTH, dtype=F32))
    deltas = jnp.broadcast_to(deltas.reshape(1, n_tiles, 1, HY_COLS), (HY_ORDER, n_tiles, 2, HY_COLS))
    deltas = deltas.reshape(1, HY_ORDER * 2 * HY_WIDTH)
    n_cols = HY_ORDER * 2 * HY_WIDTH

    def row(a):
        return a.reshape(DEPTH, 1, HY_HIDDEN)

    vec = pl.BlockSpec((None, 1, HY_HIDDEN), lambda l, j: (l, 0, 0))
    stat = pl.BlockSpec((None, 1, HY_FILT_COLS), lambda l, j: (l, 0, j))
    return pl.pallas_call(
        functools.partial(_filt_kernel, seq=seq),
        grid=(DEPTH, n_cols // HY_FILT_COLS),
        in_specs=[
            pl.BlockSpec((seq, HY_EMB_PAD), lambda l, j: (0, 0)),
            pl.BlockSpec((None, HY_EMB_PAD, HY_HIDDEN), lambda l, j: (l, 0, 0)),
            vec,
            pl.BlockSpec((None, HY_HIDDEN, HY_HIDDEN), lambda l, j: (l, 0, 0)),
            vec,
            pl.BlockSpec((None, HY_HIDDEN, HY_FILT_COLS), lambda l, j: (l, 0, j)),
            vec,
            pl.BlockSpec((1, HY_FILT_COLS), lambda l, j: (0, j)),
        ],
        out_specs=[pl.BlockSpec((None, seq, HY_FILT_COLS), lambda l, j: (l, 0, j)), stat,
                   pl.BlockSpec((None, 2, HY_FILT_COLS), lambda l, j: (l, 0, j))],
        out_shape=[jax.ShapeDtypeStruct((DEPTH, seq, n_cols), BF16),
                   jax.ShapeDtypeStruct((DEPTH, 1, n_cols), F32),
                   jax.ShapeDtypeStruct((DEPTH, 2, n_cols), F32)],
        scratch_shapes=[pltpu.VMEM((seq, HY_HIDDEN), F32)],
        compiler_params=_params("arbitrary", "arbitrary"),
        name=f"hy_filt_l{seq}",
    )(feat, w1p, row(b1), w2, row(b2), w3, row(freq), deltas)


def _dft_matrices(seq, kf):
    n_fft = 2 * seq
    n_ch = seq // kf
    th = 2.0 * math.pi / n_fft
    c = jnp.arange(n_ch, dtype=jnp.int32)[:, None]
    kl = jnp.arange(kf, dtype=jnp.int32)[:, None]
    n = jnp.arange(seq, dtype=jnp.int32)[None, :]
    a1 = th * ((c * kf * n) % n_fft).astype(F32)
    a2 = th * ((kl * n) % n_fft).astype(F32)
    c1, s1, c2, s2 = jnp.cos(a1), jnp.sin(a1), jnp.cos(a2), jnp.sin(a2)
    f_cos = c1[:, None, :] * c2[None, :, :] - s1[:, None, :] * s2[None, :, :]
    f_sin = s1[:, None, :] * c2[None, :, :] + c1[:, None, :] * s2[None, :, :]
    fmat = jnp.concatenate([f_cos, -f_sin], axis=1).astype(BF16)
    c2t, s2t = c2.T, s2.T
    g_cos = c1[:, :, None] * c2t[None, :, :] - s1[:, :, None] * s2t[None, :, :]
    g_sin = s1[:, :, None] * c2t[None, :, :] + c1[:, :, None] * s2t[None, :, :]
    gmat = jnp.concatenate([g_cos, -g_sin], axis=2).astype(BF16)
    return fmat, gmat


def _twiddles(seq):
    k = jnp.arange(seq // 2, dtype=F32)[:, None]
    ang = jnp.broadcast_to((math.pi / seq) * k, (seq // 2, HEAD_DIM))
    return jnp.cos(ang), jnp.sin(ang)


def _hy_pre_kernel(z_ref, w_ref, b_ref, ub_ref, ualt_ref, *, seq):
    half = seq // 2
    alt_e = jnp.zeros((1, HEAD_DIM), F32)
    alt_o = jnp.zeros((1, HEAD_DIM), F32)
    for m0, rows in _hy_chunks(half):
        ve, vo = _short_conv_parity(z_ref, w_ref, b_ref, m0, rows, half)
        ub_ref[m0:m0 + rows, :] = ve.astype(BF16)
        ub_ref[half + m0:half + m0 + rows, :] = vo.astype(BF16)
        sign = _alt_sign(m0, rows)
        alt_e = alt_e + jnp.sum(ve * sign, axis=0, keepdims=True)
        alt_o = alt_o + jnp.sum(vo * sign, axis=0, keepdims=True)
    ualt_ref[0:1, :] = alt_e
    ualt_ref[1:2, :] = alt_o


def _hy_spec_kernel(u_ref, h_ref, a_ref, f_ref, g_ref, tc_ref, ts_ref, y_ref, *, seq, kf):
    c = pl.program_id(1)
    cw = HY_COLS
    half = seq // 2
    n_fft = 2 * seq

    @pl.when(c == 0)
    def _():
        y_ref[...] = jnp.zeros_like(y_ref)

    f = f_ref[...]
    tw_c = jnp.concatenate([tc_ref[...]] * (cw // HEAD_DIM), axis=1)
    tw_s = jnp.concatenate([ts_ref[...]] * (cw // HEAD_DIM), axis=1)
    tw_c2 = jnp.concatenate([tw_c, tw_c], axis=1)
    tw_s2 = jnp.concatenate([tw_s, tw_s], axis=1)

    def spectrum(ref):
        e = jnp.dot(f, ref[:half, :], preferred_element_type=F32)
        o = jnp.dot(f, ref[half:, :], preferred_element_type=F32)
        er, ei, orr, oi = e[:kf], e[kf:], o[:kf], o[kf:]
        tr = orr * tw_c2 + oi * tw_s2
        ti = oi * tw_c2 - orr * tw_s2
        return er + tr, ei + ti, er - tr, ti - ei

    hpr, hpi, hmr, hmi = spectrum(h_ref)
    row = lax.broadcasted_iota(jnp.int32, (kf, 1), 0)
    a = a_ref[...]
    wgt = jnp.where((row == 0) & (c == 0), 1.0 / n_fft, 2.0 / n_fft) / (a[:, :cw] + a[:, cw:] + 1e-6)
    spr = (hpr[:, :cw] + hpr[:, cw:]) * wgt
    spi = (hpi[:, :cw] - hpi[:, cw:]) * wgt
    smr = (hmr[:, :cw] + hmr[:, cw:]) * wgt
    smi = (hmi[:, :cw] - hmi[:, cw:]) * wgt
    upr, upi, umr, umi = spectrum(u_ref)
    q0r, q0i, q1r, q1i = [], [], [], []
    for b in range(BATCH):
        bs = slice(b * cw, (b + 1) * cw)
        ppr = upr[:, bs] * spr - upi[:, bs] * spi
        ppi = upr[:, bs] * spi + upi[:, bs] * spr
        pmr = umr[:, bs] * smr - umi[:, bs] * smi
        pmi = umr[:, bs] * smi + umi[:, bs] * smr
        q0r.append(ppr + pmr)
        q0i.append(ppi - pmi)
        dr, di = ppr - pmr, ppi + pmi
        q1r.append(dr * tw_c - di * tw_s)
        q1i.append(dr * tw_s + di * tw_c)
    q0 = jnp.concatenate([jnp.concatenate(q0r, axis=1), jnp.concatenate(q0i, axis=1)], axis=0).astype(BF16)
    q1 = jnp.concatenate([jnp.concatenate(q1r, axis=1), jnp.concatenate(q1i, axis=1)], axis=0).astype(BF16)
    for r0, rows in _hy_chunks(half, HY_INV_ROWS):
        g = g_ref[r0:r0 + rows, :]
        y_ref[r0:r0 + rows, :] += jnp.dot(g, q0, preferred_element_type=F32)
        y_ref[half + r0:half + r0 + rows, :] += jnp.dot(g, q1, preferred_element_type=F32)


def _mid_terms(ualt_ref, hf_ref, hb_ref, af_ref, ab_ref, n_fft):
    ae, ao = ualt_ref[0:1, :], ualt_ref[1:2, :]
    sr = hf_ref[0:1, :] + hb_ref[0:1, :]
    si = hb_ref[1:2, :] - hf_ref[1:2, :]
    scale = (2.0 / n_fft) / (af_ref[...] + ab_ref[...] + 1e-6)
    return (ae * sr + ao * si) * scale, -(ae * si - ao * sr) * scale


def _hy_mid_kernel(zv_ref, zx_ref, y_ref, ualt_ref, hf_ref, hb_ref, af_ref, ab_ref, wv_ref, bv_ref,
                   wx_ref, bx_ref, skip_ref, s_ref, sb_ref, salt_ref, *, seq):
    half = seq // 2
    pe, po = _mid_terms(ualt_ref, hf_ref, hb_ref, af_ref, ab_ref, 2 * seq)
    alt_e = jnp.zeros((1, HEAD_DIM), F32)
    alt_o = jnp.zeros((1, HEAD_DIM), F32)
    skip = skip_ref[...]
    for m0, rows in _hy_chunks(half):
        re, ro = slice(m0, m0 + rows), slice(half + m0, half + m0 + rows)
        sign = _alt_sign(m0, rows)
        ve, vo = _short_conv_parity(zv_ref, wv_ref, bv_ref, m0, rows, half)
        xe, xo = _short_conv_parity(zx_ref, wx_ref, bx_ref, m0, rows, half)
        se = xe * (y_ref[re, :] + sign * pe + ve * skip)
        so = xo * (y_ref[ro, :] + sign * po + vo * skip)
        s_ref[re, :] = se
        s_ref[ro, :] = so
        sb_ref[re, :] = se.astype(BF16)
        sb_ref[ro, :] = so.astype(BF16)
        alt_e = alt_e + jnp.sum(se * sign, axis=0, keepdims=True)
        alt_o = alt_o + jnp.sum(so * sign, axis=0, keepdims=True)
    salt_ref[0:1, :] = alt_e
    salt_ref[1:2, :] = alt_o


def _hy_post_kernel(zx_ref, y_ref, s_ref, salt_ref, hf_ref, hb_ref, af_ref, ab_ref, wx_ref, bx_ref,
                    skip_ref, *rest, seq):
    o_ref = rest[-1]
    half = seq // 2
    pe, po = _mid_terms(salt_ref, hf_ref, hb_ref, af_ref, ab_ref, 2 * seq)
    skip = skip_ref[...]
    for m0, rows in _hy_chunks(half):
        re, ro = slice(m0, m0 + rows), slice(half + m0, half + m0 + rows)
        sign = _alt_sign(m0, rows)
        xe, xo = _short_conv_parity(zx_ref, wx_ref, bx_ref, m0, rows, half)
        ye = y_ref[re, :] + sign * pe + s_ref[re, :] * skip
        yo = y_ref[ro, :] + sign * po + s_ref[ro, :] * skip
        o_ref[pl.ds(2 * m0, rows, stride=2), :] = xe * ye
        o_ref[pl.ds(2 * m0 + 1, rows, stride=2), :] = xo * yo


def _hyena(z, z_rb0, conv_w, conv_b, skip, filt, fmat, gmat, tw, *, layer, seq, out_rows, out_rb0,
           alias_out=None):
    hb, asum, halt = filt
    twc, tws = tw
    half = seq // 2
    kf = min(HY_KF, half)
    n_ch = half // kf
    n_cols = HY_WIDTH // HY_COLS
    ew = HEAD_DIM
    sub = HY_COLS // ew
    n_ew = HY_WIDTH // ew
    hy0 = V_END // ew

    def zcols(which):
        return pl.BlockSpec((seq, ew), lambda b, e: (z_rb0 + b, hy0 + which * n_ew + e))

    def cw(which):
        return pl.BlockSpec((None, HY_SHORT, ew), lambda b, e: (layer, 0, which * n_ew + e))

    def cb(which):
        return pl.BlockSpec((None, 1, ew), lambda b, e: (layer, 0, which * n_ew + e))

    def fstat(order, direction, rows):
        return pl.BlockSpec((None, rows, ew),
                            lambda b, e: (layer, 0, ((order * n_cols + e // sub) * 2 + direction) * sub + e % sub))

    def skip_spec(order):
        return pl.BlockSpec((None, 1, ew), lambda b, e: (layer * HY_ORDER + order, 0, e))

    def act_col(b, e):
        return ((e // sub) * BATCH + b) * sub + e % sub

    seq_blk = pl.BlockSpec((seq, ew), lambda b, e: (0, act_col(b, e)))
    alt_spec = pl.BlockSpec((2, ew), lambda b, e: (0, act_col(b, e)))
    ew_params = _params("arbitrary", "arbitrary")
    act = jax.ShapeDtypeStruct((seq, BATCH * HY_WIDTH), F32)
    act_b = jax.ShapeDtypeStruct((seq, BATCH * HY_WIDTH), BF16)
    alt_shape = jax.ShapeDtypeStruct((2, BATCH * HY_WIDTH), F32)

    def spectral(ub, order):
        twid = pl.BlockSpec((kf, HEAD_DIM), lambda j, c: (c, 0))
        return pl.pallas_call(
            functools.partial(_hy_spec_kernel, seq=seq, kf=kf),
            grid=(n_cols, n_ch),
            in_specs=[
                pl.BlockSpec((seq, BATCH * HY_COLS), lambda j, c: (0, j), pipeline_mode=pl.Buffered(1)),
                pl.BlockSpec((None, seq, 2 * HY_COLS), lambda j, c: (layer, 0, order * n_cols + j),
                             pipeline_mode=pl.Buffered(1)),
                pl.BlockSpec((None, 1, 2 * HY_COLS), lambda j, c: (layer, 0, order * n_cols + j)),
                pl.BlockSpec((None, 2 * kf, half), lambda j, c: (c, 0, 0)),
                pl.BlockSpec((None, half, 2 * kf), lambda j, c: (c, 0, 0)),
                twid, twid,
            ],
            out_specs=pl.BlockSpec((seq, BATCH * HY_COLS), lambda j, c: (0, j)),
            out_shape=act,
            compiler_params=_params("arbitrary", "arbitrary"),
            name=f"hy_spec{order}_l{seq}",
        )(ub, hb, asum, fmat, gmat, twc, tws)

    ub, ualt = pl.pallas_call(
        functools.partial(_hy_pre_kernel, seq=seq),
        grid=(BATCH, n_ew),
        in_specs=[zcols(0), cw(0), cb(0)],
        out_specs=[seq_blk, alt_spec],
        out_shape=[act_b, alt_shape],
        compiler_params=ew_params,
        name=f"hy_pre_l{seq}",
    )(z, conv_w, conv_b)
    y0 = spectral(ub, 0)
    s, sb, salt = pl.pallas_call(
        functools.partial(_hy_mid_kernel, seq=seq),
        grid=(BATCH, n_ew),
        in_specs=[zcols(0), zcols(1), seq_blk, alt_spec, fstat(0, 0, 2), fstat(0, 1, 2), fstat(0, 0, 1),
                  fstat(0, 1, 1), cw(0), cb(0), cw(1), cb(1), skip_spec(0)],
        out_specs=[seq_blk, seq_blk, alt_spec],
        out_shape=[act, act_b, alt_shape],
        compiler_params=ew_params,
        name=f"hy_mid_l{seq}",
    )(z, z, y0, ualt, halt, halt, asum, asum, conv_w, conv_b, conv_w, conv_b, skip)
    y1 = spectral(sb, 1)
    in_specs = [zcols(2), seq_blk, seq_blk, alt_spec, fstat(1, 0, 2), fstat(1, 1, 2), fstat(1, 0, 1),
                fstat(1, 1, 1), cw(2), cb(2), skip_spec(1)]
    args = [z, y1, s, salt, halt, halt, asum, asum, conv_w, conv_b, skip]
    aliases = {}
    if alias_out is not None:
        in_specs.append(pl.BlockSpec(memory_space=pl.ANY))
        args.append(alias_out)
        aliases = {len(args) - 1: 0}
    return pl.pallas_call(
        functools.partial(_hy_post_kernel, seq=seq),
        grid=(BATCH, n_ew),
        in_specs=in_specs,
        out_specs=pl.BlockSpec((seq, ew), lambda b, e: (out_rb0 + b, e)),
        out_shape=jax.ShapeDtypeStruct((out_rows, HY_WIDTH), F32),
        input_output_aliases=aliases,
        compiler_params=ew_params,
        name=f"hy_post_l{seq}",
    )(*args)


def _outproj_kernel(a_ref, b_ref, c_ref, w_ref, x_ref, mod_ref, g_ref, o_ref):
    y = jnp.dot(a_ref[...], w_ref[:SGU_WIDTH, :], preferred_element_type=F32)
    y += jnp.dot(b_ref[...], w_ref[SGU_WIDTH:SGU_WIDTH + ATTN_WIDTH, :], preferred_element_type=F32)
    y += jnp.dot(c_ref[...].astype(BF16), w_ref[SGU_WIDTH + ATTN_WIDTH:, :], preferred_element_type=F32)
    o_ref[...] = x_ref[...] + mod_ref[5:6, :] * _rms(y, g_ref[3:4, :])


def _outproj(a, b, c, w_out, x, mods, norm_g, *, layer, n_tiles):
    return pl.pallas_call(
        _outproj_kernel,
        grid=(n_tiles,),
        in_specs=[
            pl.BlockSpec((TM, SGU_WIDTH), lambda i: (i, 0)),
            pl.BlockSpec((TM, ATTN_WIDTH), lambda i: (i, 0)),
            pl.BlockSpec((TM, HY_WIDTH), lambda i: (i, 0)),
            pl.BlockSpec((D_MIX, D_MODEL), lambda i: (0, 0), pipeline_mode=pl.Buffered(1)),
            pl.BlockSpec((TM, D_MODEL), lambda i: (i, 0)),
            pl.BlockSpec((None, None, N_MOD, D_MODEL), lambda i: (layer, _mod_row(i), 0, 0)),
            pl.BlockSpec((None, 6, D_MODEL), lambda i: (layer, 0, 0)),
        ],
        out_specs=pl.BlockSpec((TM, D_MODEL), lambda i: (i, 0)),
        out_shape=jax.ShapeDtypeStruct((n_tiles * TM, D_MODEL), F32),
        compiler_params=_params("arbitrary"),
        name="outproj",
    )(a, b, c, w_out, x, mods, norm_g)


def _rope_tables():
    rows = SEQ // GRID_W
    row = jnp.broadcast_to(jnp.arange(rows)[:, None], (rows, GRID_W)).reshape(SEQ).astype(F32)
    col = jnp.broadcast_to(jnp.arange(GRID_W)[None, :], (rows, GRID_W)).reshape(SEQ).astype(F32)
    half = HEAD_DIM // 2
    inv = ROPE_BASE ** (-jnp.arange(0, half, 2, dtype=F32) / half)
    ang_r = row[:, None] * inv
    ang_c = col[:, None] * inv
    ang = jnp.concatenate([ang_r, ang_r, ang_c, ang_c], axis=-1)
    cos, sin = jnp.cos(ang), jnp.sin(ang)
    first = (jnp.arange(HEAD_DIM) % half) < (half // 2)
    sa = jnp.where(first, -sin, 0.0)
    sb = jnp.where(first, 0.0, sin)

    def stack(lat, ctx_value):
        return jnp.concatenate([lat] * BATCH + [jnp.full((N_CTX, HEAD_DIM), ctx_value, F32)], axis=0)

    return stack(cos, 1.0), stack(sa, 0.0), stack(sb, 0.0)


def kernel(x, c, ctx, c_ctx, w_ada, b_ada, norm_g, ffn1_gate, ffn1_up, ffn1_down, w_in,
           sgu_g, sgu_w, sgu_b, attn_sink, hy_conv_w, hy_conv_b, hy_f_w1, hy_f_b1, hy_f_w2,
           hy_f_b2, hy_f_w3, hy_freq, hy_skip, w_out, ffn2_gate, ffn2_up, ffn2_down):
    xs = jnp.concatenate([x.reshape(N_LAT, D_MODEL), ctx.reshape(N_CTX, D_MODEL)], axis=0)
    cc = jnp.concatenate([c, c_ctx[None], jnp.zeros((MOD_ROWS - BATCH - 1, D_MODEL), F32)], axis=0)
    mods = _ada_mods(cc, w_ada, b_ada)

    w1 = [w[0].astype(BF16) for w in (ffn1_gate, ffn1_up, ffn1_down)]
    cos, sa, sb = _rope_tables()
    conv_b = hy_conv_b.reshape(DEPTH, 1, (HY_ORDER + 1) * HY_WIDTH)
    skip = hy_skip.reshape(DEPTH * HY_ORDER, 1, HY_WIDTH)
    filt_w = (hy_f_w1, hy_f_b1, hy_f_w2, hy_f_b2, hy_f_w3, hy_freq)
    filt_lat = _hyena_filters(SEQ, *filt_w)
    filt_ctx = _hyena_filters(CTX_LEN, *filt_w)
    f_lat, g_lat = _dft_matrices(SEQ // 2, min(HY_KF, SEQ // 2))
    f_ctx, g_ctx = _dft_matrices(CTX_LEN // 2, min(HY_KF, CTX_LEN // 2))
    tw_lat, tw_ctx = _twiddles(SEQ), _twiddles(CTX_LEN)

    all_tiles = N_TOK // TM
    lat_tiles = N_LAT // TM
    ctx_rb = N_LAT // CTX_LEN

    for i in range(DEPTH):
        last = i == DEPTH - 1
        n_tiles = lat_tiles if last else all_tiles
        n_rows = n_tiles * TM

        jobs = [(w, i) for w in (ffn2_gate, ffn2_up, ffn2_down, w_in, w_out)]
        xs, (w2g, w2u, w2d, w_in_b, w_out_b) = _ffn(xs, mods, norm_g, *w1, jobs, layer=i, k=0, n_tiles=all_tiles)
        z = _inproj(xs, mods, norm_g, w_in_b, cos, sa, sb, layer=i)

        a_out = _sgu(z, sgu_g, sgu_w, sgu_b, layer=i, n_rows=n_rows)
        b_out = _attention(z, attn_sink, layer=i, n_rows=n_rows)
        c_out = _hyena(z, 0, hy_conv_w, conv_b, skip, filt_lat, f_lat, g_lat, tw_lat, layer=i, seq=SEQ,
                       out_rows=n_rows, out_rb0=0)
        if not last:
            c_out = _hyena(z, ctx_rb, hy_conv_w, conv_b, skip, filt_ctx, f_ctx, g_ctx, tw_ctx, layer=i,
                           seq=CTX_LEN, out_rows=n_rows, out_rb0=ctx_rb, alias_out=c_out)

        xs = _outproj(a_out, b_out, c_out, w_out_b, xs, mods, norm_g, layer=i, n_tiles=n_tiles)
        jobs = [] if last else [(w, i + 1) for w in (ffn1_gate, ffn1_up, ffn1_down)]
        xs, w1 = _ffn(xs, mods, norm_g, w2g, w2u, w2d, jobs, layer=i, k=2, n_tiles=n_tiles)

    return xs[:N_LAT].reshape(BATCH, SEQ, D_MODEL)
```

```python
import functools
import math

import jax
import jax.numpy as jnp
from jax import lax
from jax.experimental import pallas as pl
from jax.experimental.pallas import tpu as pltpu

D_MODEL = 2048
BATCH = 2
SEQ = 4096
DEPTH = 4
GRID_W = 64
CTX_LEN = 256
HEAD_DIM = 128
NORM_EPS = 1e-6
N_MOD = 9
D_FF = 5632
SGU_GROUPS = 4
SGU_WIDTH = SGU_GROUPS * HEAD_DIM
CHUNK = 128
N_Q_HEADS = 8
N_KV_HEADS = 2
Q_GROUP = N_Q_HEADS // N_KV_HEADS
ATTN_WIDTH = N_Q_HEADS * HEAD_DIM
KV_WIDTH = N_KV_HEADS * HEAD_DIM
WINDOW = 128
BLOCK = 128
ROPE_BASE = 10000.0
NEG_INF = -1e30
HY_GROUPS = 4
HY_WIDTH = HY_GROUPS * HEAD_DIM
HY_ORDER = 2
HY_SHORT = 3
HY_BANDS = 16
HY_EMB = 1 + 2 * HY_BANDS
HY_HIDDEN = 64
HY_TARGET = 1e-2
HY_FAST_DECAY = 0.3
HY_SLOW_DECAY = 1.5
D_MIX = SGU_WIDTH + ATTN_WIDTH + HY_WIDTH
A_END = 2 * SGU_WIDTH
Q_END = A_END + ATTN_WIDTH
K_END = Q_END + KV_WIDTH
V_END = K_END + KV_WIDTH
IN_COLS = V_END + (HY_ORDER + 1) * HY_WIDTH

N_LAT = BATCH * SEQ
N_CTX = BATCH * CTX_LEN
N_TOK = N_LAT + N_CTX
MOD_ROWS = 8

TM = 512
TF = 512
FFN_ROWS = 32
BF16_SUBLANES = 16
TN_IN = 1024
TN_ADA = 1024
HY_COLS = 256
HY_FILT_COLS = 2 * HY_COLS
HY_KF = 256
HY_ROWS = 512
HY_INV_ROWS = 1024
HY_EMB_PAD = 128
VMEM_LIMIT = 56 * 1024 * 1024

F32 = jnp.float32
BF16 = jnp.bfloat16


def _params(*sem):
    return pltpu.CompilerParams(dimension_semantics=sem, vmem_limit_bytes=VMEM_LIMIT)


def _mod_row(i):
    return jnp.minimum(i // (SEQ // TM), BATCH)


def _silu(a):
    return a / (1.0 + jnp.exp(-a))


def _rms(x, g):
    return x * lax.rsqrt(jnp.mean(x * x, axis=-1, keepdims=True) + NORM_EPS) * g


def _ada_kernel(c_ref, w_ref, b_ref, o_ref):
    s = _silu(c_ref[...]).astype(BF16)
    o_ref[...] = jnp.dot(s, w_ref[...].astype(BF16), preferred_element_type=F32) + b_ref[...]


def _ada_mods(cc, w_ada, b_ada):
    n_out = N_MOD * D_MODEL
    out = pl.pallas_call(
        _ada_kernel,
        grid=(DEPTH, n_out // TN_ADA),
        in_specs=[
            pl.BlockSpec((MOD_ROWS, D_MODEL), lambda l, j: (0, 0)),
            pl.BlockSpec((None, D_MODEL, TN_ADA), lambda l, j: (l, 0, j)),
            pl.BlockSpec((None, 1, TN_ADA), lambda l, j: (l, 0, j)),
        ],
        out_specs=pl.BlockSpec((None, MOD_ROWS, TN_ADA), lambda l, j: (l, 0, j)),
        out_shape=jax.ShapeDtypeStruct((DEPTH, MOD_ROWS, n_out), F32),
        compiler_params=_params("arbitrary", "arbitrary"),
        name="ada_mods",
    )(cc, w_ada, b_ada.reshape(DEPTH, 1, n_out))
    return out.reshape(DEPTH, MOD_ROWS, N_MOD, D_MODEL)


def _ffn_kernel(*refs, k, nf, n_tiles, n_jobs):
    x_ref, modp_ref, modn_ref, g_ref, wg_ref, wu_ref, wd_ref = refs[:7]
    cast_src = refs[7:7 + n_jobs]
    o_ref = refs[7 + n_jobs]
    cast_dst = refs[8 + n_jobs:8 + 2 * n_jobs]
    h_ref, hn_ref, acc_ref, accf_ref = refs[8 + 2 * n_jobs:]
    i = pl.program_id(0)
    f = pl.program_id(1)

    def prenorm(mod_ref, dst_ref):
        g = g_ref[2 * k:2 * k + 1, :]
        m1 = 1.0 + mod_ref[3 * k:3 * k + 1, :]
        m2 = mod_ref[3 * k + 1:3 * k + 2, :]
        for r0 in range(0, TM, FFN_ROWS):
            rs = slice(r0, r0 + FFN_ROWS)
            dst_ref[rs, :] = (_rms(x_ref[rs, :], g) * m1 + m2).astype(BF16)

    def postnorm():
        g = g_ref[2 * k + 1:2 * k + 2, :]
        gate = 0.5 * modp_ref[3 * k + 2:3 * k + 3, :]
        for r0 in range(0, TM, FFN_ROWS):
            rs = slice(r0, r0 + FFN_ROWS)
            o_ref[rs, :] = x_ref[rs, :] + gate * _rms(accf_ref[rs, :], g)

    def matmuls(src_ref, first=False, final=False):
        h = src_ref[...]
        a = jnp.dot(h, wg_ref[...], preferred_element_type=F32)
        b = jnp.dot(h, wu_ref[...], preferred_element_type=F32)
        p = (_silu(a) * b).astype(BF16)
        d = jnp.dot(p, wd_ref[...], preferred_element_type=F32)
        if first:
            acc_ref[...] = d
        elif final:
            accf_ref[...] = acc_ref[...] + d
        else:
            acc_ref[...] += d
        for src, dst in zip(cast_src, cast_dst):
            dst[...] = src[...].astype(BF16)

    @pl.when((f == 0) & (i == 0))
    def _():
        prenorm(modp_ref, h_ref)
        matmuls(h_ref, first=True)

    @pl.when((f == 0) & (i > 0) & (i < n_tiles))
    def _():
        matmuls(hn_ref, first=True)
        h_ref[...] = hn_ref[...]
        postnorm()

    @pl.when((f == 0) & (i == n_tiles))
    def _():
        postnorm()

    @pl.when((f > 0) & (f < nf - 1) & (i < n_tiles))
    def _():
        matmuls(h_ref)

    @pl.when((f == nf - 1) & (i < n_tiles))
    def _():
        matmuls(h_ref, final=True)
        prenorm(modn_ref, hn_ref)


def _ffn(x, mods, norm_g, wg, wu, wd, cast_jobs, *, layer, k, n_tiles):
    nf = D_FF // TF
    n_steps = n_tiles * nf
    last = n_tiles - 1

    def prev_tile(i):
        return jnp.maximum(i - 1, 0)

    def next_tile(i):
        return jnp.minimum(i + 1, last)

    def chunk(i, f):
        return jnp.where(i == n_tiles, nf - 1, f)

    in_specs = [
        pl.BlockSpec((TM, D_MODEL), lambda i, f: (jnp.where(f == 0, prev_tile(i), next_tile(i)), 0)),
        pl.BlockSpec((None, None, N_MOD, D_MODEL), lambda i, f: (layer, _mod_row(prev_tile(i)), 0, 0)),
        pl.BlockSpec((None, None, N_MOD, D_MODEL), lambda i, f: (layer, _mod_row(next_tile(i)), 0, 0)),
        pl.BlockSpec((None, 6, D_MODEL), lambda i, f: (layer, 0, 0)),
        pl.BlockSpec((D_MODEL, TF), lambda i, f: (0, chunk(i, f))),
        pl.BlockSpec((D_MODEL, TF), lambda i, f: (0, chunk(i, f))),
        pl.BlockSpec((TF, D_MODEL), lambda i, f: (chunk(i, f), 0)),
    ]
    out_specs = [pl.BlockSpec((TM, D_MODEL), lambda i, f: (prev_tile(i), 0))]
    out_shape = [jax.ShapeDtypeStruct((n_tiles * TM, D_MODEL), F32)]
    for w, src_layer in cast_jobs:
        n_rows, n_cols = w.shape[1:]
        rb = BF16_SUBLANES * pl.cdiv(pl.cdiv(n_rows, n_steps), BF16_SUBLANES)
        assert n_rows % rb == 0
        last_blk = n_rows // rb - 1

        def step_block(i, f, last_blk=last_blk):
            return jnp.minimum(i * nf + f, last_blk)

        in_specs.append(pl.BlockSpec((None, rb, n_cols),
                                     lambda i, f, sl=src_layer, sb=step_block: (sl, sb(i, f), 0)))
        out_specs.append(pl.BlockSpec((rb, n_cols), lambda i, f, sb=step_block: (sb(i, f), 0)))
        out_shape.append(jax.ShapeDtypeStruct((n_rows, n_cols), BF16))
    outs = pl.pallas_call(
        functools.partial(_ffn_kernel, k=k, nf=nf, n_tiles=n_tiles, n_jobs=len(cast_jobs)),
        grid=(n_tiles + 1, nf),
        in_specs=in_specs,
        out_specs=out_specs,
        out_shape=out_shape,
        scratch_shapes=[pltpu.VMEM((TM, D_MODEL), BF16), pltpu.VMEM((TM, D_MODEL), BF16),
                        pltpu.VMEM((TM, D_MODEL), F32), pltpu.VMEM((TM, D_MODEL), F32)],
        compiler_params=_params("arbitrary", "arbitrary"),
        name=f"ffn{k}",
    )(x, mods, mods, norm_g, wg, wu, wd, *[w for w, _ in cast_jobs])
    return outs[0], list(outs[1:])


def _rope(t, cos, sa, sb):
    return t * cos + pltpu.roll(t, 3 * HEAD_DIM // 4, 1) * sa + pltpu.roll(t, HEAD_DIM // 4, 1) * sb


def _inproj_kernel(x_ref, mod_ref, g_ref, w_ref, cos_ref, sa_ref, sb_ref, o_ref):
    y = _rms(x_ref[...], g_ref[2:3, :])
    h = (y * (1.0 + mod_ref[3:4, :]) + mod_ref[4:5, :]).astype(BF16)
    cos, sa, sb = cos_ref[...], sa_ref[...], sb_ref[...]
    n_rope = (K_END - A_END) // HEAD_DIM
    for n in range(IN_COLS // TN_IN):
        c0 = n * TN_IN
        z = jnp.dot(h, w_ref[:, c0:c0 + TN_IN], preferred_element_type=F32)
        for hd in range(TN_IN // HEAD_DIM):
            col = c0 + hd * HEAD_DIM
            zs = z[:, hd * HEAD_DIM:(hd + 1) * HEAD_DIM]
            if A_END <= col < A_END + n_rope * HEAD_DIM:
                zs = _rope(zs, cos, sa, sb)
            o_ref[:, col:col + HEAD_DIM] = zs


def _inproj(x, mods, norm_g, w_in, cos, sa, sb, *, layer):
    n_tiles = N_TOK // TM
    tab = pl.BlockSpec((TM, HEAD_DIM), lambda i: (i, 0))
    return pl.pallas_call(
        _inproj_kernel,
        grid=(n_tiles,),
        in_specs=[
            pl.BlockSpec((TM, D_MODEL), lambda i: (i, 0)),
            pl.BlockSpec((None, None, N_MOD, D_MODEL), lambda i: (layer, _mod_row(i), 0, 0)),
            pl.BlockSpec((None, 6, D_MODEL), lambda i: (layer, 0, 0)),
            pl.BlockSpec((D_MODEL, IN_COLS), lambda i: (0, 0), pipeline_mode=pl.Buffered(1)),
            tab, tab, tab,
        ],
        out_specs=pl.BlockSpec((TM, IN_COLS), lambda i: (i, 0)),
        out_shape=jax.ShapeDtypeStruct((N_TOK, IN_COLS), F32),
        compiler_params=_params("arbitrary"),
        name="inproj",
    )(x, mods, norm_g, w_in, cos, sa, sb)


def _sgu_kernel(z_ref, g_ref, w_ref, b_ref, o_ref, *, rows):
    inv_sqrt2 = 1.0 / math.sqrt(2.0)
    for c in range(rows // CHUNK):
        rs = slice(c * CHUNK, (c + 1) * CHUNK)
        a = z_ref[rs, :]
        a = a * (lax.erf(a * inv_sqrt2) + 1.0) * 0.5
        u, v = a[:, :SGU_WIDTH], a[:, SGU_WIDTH:]
        mu = jnp.mean(v, axis=-1, keepdims=True)
        vc = v - mu
        var = jnp.mean(vc * vc, axis=-1, keepdims=True)
        v = (vc * lax.rsqrt(var + NORM_EPS) * g_ref[...]).astype(BF16)
        for g in range(SGU_GROUPS):
            cs = slice(g * HEAD_DIM, (g + 1) * HEAD_DIM)
            mixed = jnp.dot(w_ref[g].astype(BF16), v[:, cs], preferred_element_type=F32) + b_ref[:, g:g + 1]
            o_ref[rs, cs] = (u[:, cs] * mixed).astype(o_ref.dtype)


def _sgu(z, sgu_g, sgu_w, sgu_b, *, layer, n_rows):
    rows = TM
    return pl.pallas_call(
        functools.partial(_sgu_kernel, rows=rows),
        grid=(n_rows // rows,),
        in_specs=[
            pl.BlockSpec((rows, A_END), lambda i: (i, 0)),
            pl.BlockSpec((None, 1, SGU_WIDTH), lambda i: (layer, 0, 0)),
            pl.BlockSpec((None, SGU_GROUPS, CHUNK, CHUNK), lambda i: (layer, 0, 0, 0)),
            pl.BlockSpec((None, CHUNK, SGU_GROUPS), lambda i: (layer, 0, 0)),
        ],
        out_specs=pl.BlockSpec((rows, SGU_WIDTH), lambda i: (i, 0)),
        out_shape=jax.ShapeDtypeStruct((n_rows, SGU_WIDTH), BF16),
        compiler_params=_params("arbitrary"),
        name="sgu",
    )(z, sgu_g.reshape(DEPTH, 1, SGU_WIDTH), sgu_w, jnp.swapaxes(sgu_b, 1, 2))


def _attn_kernel(sink_ref, q_ref, kp_ref, kc_ref, kn_ref, vp_ref, vc_ref, vn_ref, kx_ref, vx_ref,
                 o_ref, *, layer, nb, n_lat_blocks):
    s_id = pl.program_id(0)
    is_lat = s_id < n_lat_blocks
    n = s_id % nb
    n_keys = CTX_LEN + 3 * BLOCK
    n_rows = Q_GROUP * BLOCK
    row_id = lax.broadcasted_iota(jnp.int32, (n_rows, 1), 0)
    row = lax.broadcasted_iota(jnp.int32, (n_rows, n_keys), 0) & (BLOCK - 1)
    col = lax.broadcasted_iota(jnp.int32, (n_rows, n_keys), 1)
    rel = col - (CTX_LEN + BLOCK)
    lo = jnp.where(is_lat, -n * BLOCK, 4 * BLOCK)
    hi = jnp.where(is_lat, (nb - n) * BLOCK, -4 * BLOCK)
    dist = rel - row
    valid = (col < CTX_LEN) | ((dist >= -WINDOW) & (dist <= WINDOW) & (rel >= lo) & (rel < hi))
    scale = HEAD_DIM ** -0.5
    nt = (((1,), (1,)), ((), ()))
    for kh in range(N_KV_HEADS):
        ks = slice(kh * HEAD_DIM, (kh + 1) * HEAD_DIM)
        keys = jnp.concatenate([kx_ref[:, ks], kp_ref[:, ks], kc_ref[:, ks], kn_ref[:, ks]], axis=0).astype(BF16)
        vals = jnp.concatenate([vx_ref[:, ks], vp_ref[:, ks], vc_ref[:, ks], vn_ref[:, ks]], axis=0).astype(BF16)
        heads = [kh * Q_GROUP + g for g in range(Q_GROUP)]
        q = jnp.concatenate([q_ref[:, hd * HEAD_DIM:(hd + 1) * HEAD_DIM] for hd in heads], axis=0).astype(BF16)
        sink = jnp.full((n_rows, 1), sink_ref[layer, heads[0]], F32)
        for g, hd in enumerate(heads[1:], start=1):
            sink = jnp.where(row_id >= g * BLOCK, sink_ref[layer, hd], sink)
        s = lax.dot_general(q, keys, nt, preferred_element_type=F32) * scale
        s = jnp.where(valid, s, NEG_INF)
        m = jnp.maximum(jnp.max(s, axis=-1, keepdims=True), sink)
        p = jnp.exp(s - m)
        denom = jnp.sum(p, axis=-1, keepdims=True) + jnp.exp(sink - m)
        probs = (p / denom).astype(BF16)
        out = jnp.dot(probs, vals, preferred_element_type=F32).astype(o_ref.dtype)
        for g, hd in enumerate(heads):
            o_ref[:, hd * HEAD_DIM:(hd + 1) * HEAD_DIM] = out[g * BLOCK:(g + 1) * BLOCK]


def _attention(z, attn_sink, *, layer, n_rows):
    nb = SEQ // BLOCK
    n_lat_blocks = N_LAT // BLOCK
    n_blocks = n_rows // BLOCK
    ctx_blocks = CTX_LEN // BLOCK
    q_col = A_END // ATTN_WIDTH
    k_col = Q_END // KV_WIDTH
    v_col = K_END // KV_WIDTH

    def lat(s):
        return s < n_lat_blocks

    def band(off, col):
        def index(s):
            b = s // nb
            nbr = jnp.clip(s % nb + off, 0, nb - 1) + b * nb
            return (jnp.where(lat(s), nbr, s), col)
        return pl.BlockSpec((BLOCK, KV_WIDTH), index)

    def ctx(col):
        def index(s):
            b = jnp.where(lat(s), s // nb, (s - n_lat_blocks) // ctx_blocks)
            return (N_LAT // CTX_LEN + b, col)
        return pl.BlockSpec((CTX_LEN, KV_WIDTH), index)

    return pl.pallas_call(
        functools.partial(_attn_kernel, layer=layer, nb=nb, n_lat_blocks=n_lat_blocks),
        grid=(n_blocks,),
        in_specs=[
            pl.BlockSpec(memory_space=pltpu.SMEM),
            pl.BlockSpec((BLOCK, ATTN_WIDTH), lambda s: (s, q_col)),
            band(-1, k_col), band(0, k_col), band(1, k_col),
            band(-1, v_col), band(0, v_col), band(1, v_col),
            ctx(k_col), ctx(v_col),
        ],
        out_specs=pl.BlockSpec((BLOCK, ATTN_WIDTH), lambda s: (s, 0)),
        out_shape=jax.ShapeDtypeStruct((n_rows, ATTN_WIDTH), BF16),
        compiler_params=_params("arbitrary"),
        name="attn",
    )(attn_sink, z, z, z, z, z, z, z, z, z)


def _short_conv_parity(ref, w_ref, b_ref, m0, rows, half):
    ze = ref[pl.ds(2 * m0, rows, stride=2), :]
    zo = ref[pl.ds(2 * m0 + 1, rows, stride=2), :]
    m = lax.broadcasted_iota(jnp.int32, (rows, 1), 0) + m0
    if m0 == 0:
        zo_prev = jnp.where(m == 0, 0.0, pltpu.roll(zo, 1, 0))
    else:
        zo_prev = ref[pl.ds(2 * m0 - 1, rows, stride=2), :]
    if m0 + rows == half:
        ze_next = jnp.where(m == half - 1, 0.0, pltpu.roll(ze, rows - 1, 0))
    else:
        ze_next = ref[pl.ds(2 * m0 + 2, rows, stride=2), :]
    w0, w1, w2, b = w_ref[0:1, :], w_ref[1:2, :], w_ref[2:3, :], b_ref[...]
    return zo_prev * w0 + ze * w1 + zo * w2 + b, ze * w0 + zo * w1 + ze_next * w2 + b


def _alt_sign(r0, rows):
    t = lax.broadcasted_iota(jnp.int32, (rows, 1), 0) + r0
    return jnp.where((t & 1) == 0, 1.0, -1.0)


def _hy_chunks(seq, chunk=None):
    rows = min(chunk or HY_ROWS, seq)
    return [(r * rows, rows) for r in range(seq // rows)]


def _filt_kernel(feat_ref, w1_ref, b1_ref, w2_ref, b2_ref, w3_ref, fr_ref, dl_ref,
                 hb_ref, asum_ref, hnyq_ref, hd_ref, *, seq):
    hp = lax.Precision.HIGHEST

    @pl.when(pl.program_id(1) == 0)
    def _():
        fr = fr_ref[...]
        hd = jnp.sin(fr * (jnp.dot(feat_ref[...], w1_ref[...], precision=hp, preferred_element_type=F32)
                           + b1_ref[...]))
        hd_ref[...] = jnp.sin(fr * (jnp.dot(hd, w2_ref[...], precision=hp, preferred_element_type=F32)
                                    + b2_ref[...]))

    half = seq // 2
    asum = jnp.zeros((1, HY_FILT_COLS), F32)
    alt = [jnp.zeros((1, HY_FILT_COLS), F32), jnp.zeros((1, HY_FILT_COLS), F32)]
    for r0, rows in _hy_chunks(seq, min(HY_ROWS, half)):
        rs = slice(r0, r0 + rows)
        h = jnp.dot(hd_ref[rs, :].astype(BF16), w3_ref[...].astype(BF16), preferred_element_type=F32)
        h = h * jnp.exp(-feat_ref[rs, 0:1] * dl_ref[...])
        asum = asum + jnp.sum(jnp.abs(h), axis=0, keepdims=True)
        alt[r0 // half] = alt[r0 // half] + jnp.sum(h * _alt_sign(r0 % half, rows), axis=0, keepdims=True)
        hb_ref[rs, :] = h.astype(BF16)
    asum_ref[...] = asum
    hnyq_ref[0:1, :] = alt[0]
    hnyq_ref[1:2, :] = alt[1]


def _hyena_filters(seq, w1, b1, w2, b2, w3, freq):
    n_tiles = HY_WIDTH // HY_COLS
    w3 = w3.reshape(DEPTH, HY_HIDDEN, HY_ORDER, 2, n_tiles, HY_COLS)
    w3 = jnp.swapaxes(w3, 3, 4).reshape(DEPTH, HY_HIDDEN, HY_ORDER * 2 * HY_WIDTH)
    t = jnp.linspace(0.0, 1.0, seq, dtype=F32)[:, None]
    w = (2.0 * math.pi / seq) * jnp.arange(seq, dtype=F32)[:, None]
    bands = jnp.linspace(1e-4, HY_BANDS - 1, HY_BANDS, dtype=F32)[None, :]
    feat = jnp.concatenate([t, jnp.cos(bands * w), -jnp.sin(bands * w)], axis=-1)
    feat = jnp.pad(feat, ((0, 0), (0, HY_EMB_PAD - HY_EMB)))
    feat = jnp.concatenate([feat[0::2], feat[1::2]], axis=0)
    w1p = jnp.pad(w1, ((0, 0), (0, HY_EMB_PAD - HY_EMB), (0, 0)))
    deltas = jnp.abs(jnp.linspace(math.log(HY_TARGET) / HY_FAST_DECAY,
                                  math.log(HY_TARGET) / HY_SLOW_DECAY, HY_WIDTH, dtype=F32))
    deltas = jnp.broadcast_to(deltas.reshape(1, n_tiles, 1, HY_COLS), (HY_ORDER, n_tiles, 2, HY_COLS))
    deltas = deltas.reshape(1, HY_ORDER * 2 * HY_WIDTH)
    n_cols = HY_ORDER * 2 * HY_WIDTH

    def row(a):
        return a.reshape(DEPTH, 1, HY_HIDDEN)

    vec = pl.BlockSpec((None, 1, HY_HIDDEN), lambda l, j: (l, 0, 0))
    stat = pl.BlockSpec((None, 1, HY_FILT_COLS), lambda l, j: (l, 0, j))
    return pl.pallas_call(
        functools.partial(_filt_kernel, seq=seq),
        grid=(DEPTH, n_cols // HY_FILT_COLS),
        in_specs=[
            pl.BlockSpec((seq, HY_EMB_PAD), lambda l, j: (0, 0)),
            pl.BlockSpec((None, HY_EMB_PAD, HY_HIDDEN), lambda l, j: (l, 0, 0)),
            vec,
            pl.BlockSpec((None, HY_HIDDEN, HY_HIDDEN), lambda l, j: (l, 0, 0)),
            vec,
            pl.BlockSpec((None, HY_HIDDEN, HY_FILT_COLS), lambda l, j: (l, 0, j)),
            vec,
            pl.BlockSpec((1, HY_FILT_COLS), lambda l, j: (0, j)),
        ],
        out_specs=[pl.BlockSpec((None, seq, HY_FILT_COLS), lambda l, j: (l, 0, j)), stat,
                   pl.BlockSpec((None, 2, HY_FILT_COLS), lambda l, j: (l, 0, j))],
        out_shape=[jax.ShapeDtypeStruct((DEPTH, seq, n_cols), BF16),
                   jax.ShapeDtypeStruct((DEPTH, 1, n_cols), F32),
                   jax.ShapeDtypeStruct((DEPTH, 2, n_cols), F32)],
        scratch_shapes=[pltpu.VMEM((seq, HY_HIDDEN), F32)],
        compiler_params=_params("arbitrary", "arbitrary"),
        name=f"hy_filt_l{seq}",
    )(feat, w1p, row(b1), w2, row(b2), w3, row(freq), deltas)


def _dft_matrices(seq, kf):
    n_fft = 2 * seq
    n_ch = seq // kf
    th = 2.0 * math.pi / n_fft
    c = jnp.arange(n_ch, dtype=jnp.int32)[:, None]
    kl = jnp.arange(kf, dtype=jnp.int32)[:, None]
    n = jnp.arange(seq, dtype=jnp.int32)[None, :]
    a1 = th * ((c * kf * n) % n_fft).astype(F32)
    a2 = th * ((kl * n) % n_fft).astype(F32)
    c1, s1, c2, s2 = jnp.cos(a1), jnp.sin(a1), jnp.cos(a2), jnp.sin(a2)
    f_cos = c1[:, None, :] * c2[None, :, :] - s1[:, None, :] * s2[None, :, :]
    f_sin = s1[:, None, :] * c2[None, :, :] + c1[:, None, :] * s2[None, :, :]
    fmat = jnp.concatenate([f_cos, -f_sin], axis=1).astype(BF16)
    c2t, s2t = c2.T, s2.T
    g_cos = c1[:, :, None] * c2t[None, :, :] - s1[:, :, None] * s2t[None, :, :]
    g_sin = s1[:, :, None] * c2t[None, :, :] + c1[:, :, None] * s2t[None, :, :]
    gmat = jnp.concatenate([g_cos, -g_sin], axis=2).astype(BF16)
    return fmat, gmat


def _twiddles(seq):
    k = jnp.arange(seq // 2, dtype=F32)[:, None]
    ang = jnp.broadcast_to((math.pi / seq) * k, (seq // 2, HEAD_DIM))
    return jnp.cos(ang), jnp.sin(ang)


def _hy_pre_kernel(z_ref, w_ref, b_ref, ub_ref, ualt_ref, *, seq):
    half = seq // 2
    alt_e = jnp.zeros((1, HEAD_DIM), F32)
    alt_o = jnp.zeros((1, HEAD_DIM), F32)
    for m0, rows in _hy_chunks(half):
        ve, vo = _short_conv_parity(z_ref, w_ref, b_ref, m0, rows, half)
        ub_ref[m0:m0 + rows, :] = ve.astype(BF16)
        ub_ref[half + m0:half + m0 + rows, :] = vo.astype(BF16)
        sign = _alt_sign(m0, rows)
        alt_e = alt_e + jnp.sum(ve * sign, axis=0, keepdims=True)
        alt_o = alt_o + jnp.sum(vo * sign, axis=0, keepdims=True)
    ualt_ref[0:1, :] = alt_e
    ualt_ref[1:2, :] = alt_o


def _hy_spec_kernel(u_ref, h_ref, a_ref, f_ref, g_ref, tc_ref, ts_ref, y_ref, *, seq, kf):
    c = pl.program_id(1)
    cw = HY_COLS
    half = seq // 2
    n_fft = 2 * seq

    @pl.when(c == 0)
    def _():
        y_ref[...] = jnp.zeros_like(y_ref)

    f = f_ref[...]
    tw_c = jnp.concatenate([tc_ref[...]] * (cw // HEAD_DIM), axis=1)
    tw_s = jnp.concatenate([ts_ref[...]] * (cw // HEAD_DIM), axis=1)
    tw_c2 = jnp.concatenate([tw_c, tw_c], axis=1)
    tw_s2 = jnp.concatenate([tw_s, tw_s], axis=1)

    def spectrum(ref):
        e = jnp.dot(f, ref[:half, :], preferred_element_type=F32)
        o = jnp.dot(f, ref[half:, :], preferred_element_type=F32)
        er, ei, orr, oi = e[:kf], e[kf:], o[:kf], o[kf:]
        tr = orr * tw_c2 + oi * tw_s2
        ti = oi * tw_c2 - orr * tw_s2
        return er + tr, ei + ti, er - tr, ti - ei

    hpr, hpi, hmr, hmi = spectrum(h_ref)
    row = lax.broadcasted_iota(jnp.int32, (kf, 1), 0)
    a = a_ref[...]
    wgt = jnp.where((row == 0) & (c == 0), 1.0 / n_fft, 2.0 / n_fft) / (a[:, :cw] + a[:, cw:] + 1e-6)
    spr = (hpr[:, :cw] + hpr[:, cw:]) * wgt
    spi = (hpi[:, :cw] - hpi[:, cw:]) * wgt
    smr = (hmr[:, :cw] + hmr[:, cw:]) * wgt
    smi = (hmi[:, :cw] - hmi[:, cw:]) * wgt
    upr, upi, umr, umi = spectrum(u_ref)
    q0r, q0i, q1r, q1i = [], [], [], []
    for b in range(BATCH):
        bs = slice(b * cw, (b + 1) * cw)
        ppr = upr[:, bs] * spr - upi[:, bs] * spi
        ppi = upr[:, bs] * spi + upi[:, bs] * spr
        pmr = umr[:, bs] * smr - umi[:, bs] * smi
        pmi = umr[:, bs] * smi + umi[:, bs] * smr
        q0r.append(ppr + pmr)
        q0i.append(ppi - pmi)
        dr, di = ppr - pmr, ppi + pmi
        q1r.append(dr * tw_c - di * tw_s)
        q1i.append(dr * tw_s + di * tw_c)
    q0 = jnp.concatenate([jnp.concatenate(q0r, axis=1), jnp.concatenate(q0i, axis=1)], axis=0).astype(BF16)
    q1 = jnp.concatenate([jnp.concatenate(q1r, axis=1), jnp.concatenate(q1i, axis=1)], axis=0).astype(BF16)
    for r0, rows in _hy_chunks(half, HY_INV_ROWS):
        g = g_ref[r0:r0 + rows, :]
        y_ref[r0:r0 + rows, :] += jnp.dot(g, q0, preferred_element_type=F32)
        y_ref[half + r0:half + r0 + rows, :] += jnp.dot(g, q1, preferred_element_type=F32)


def _mid_terms(ualt_ref, hf_ref, hb_ref, af_ref, ab_ref, n_fft):
    ae, ao = ualt_ref[0:1, :], ualt_ref[1:2, :]
    sr = hf_ref[0:1, :] + hb_ref[0:1, :]
    si = hb_ref[1:2, :] - hf_ref[1:2, :]
    scale = (2.0 / n_fft) / (af_ref[...] + ab_ref[...] + 1e-6)
    return (ae * sr + ao * si) * scale, -(ae * si - ao * sr) * scale


def _hy_mid_kernel(zv_ref, zx_ref, y_ref, ualt_ref, hf_ref, hb_ref, af_ref, ab_ref, wv_ref, bv_ref,
                   wx_ref, bx_ref, skip_ref, s_ref, sb_ref, salt_ref, *, seq):
    half = seq // 2
    pe, po = _mid_terms(ualt_ref, hf_ref, hb_ref, af_ref, ab_ref, 2 * seq)
    alt_e = jnp.zeros((1, HEAD_DIM), F32)
    alt_o = jnp.zeros((1, HEAD_DIM), F32)
    skip = skip_ref[...]
    for m0, rows in _hy_chunks(half):
        re, ro = slice(m0, m0 + rows), slice(half + m0, half + m0 + rows)
        sign = _alt_sign(m0, rows)
        ve, vo = _short_conv_parity(zv_ref, wv_ref, bv_ref, m0, rows, half)
        xe, xo = _short_conv_parity(zx_ref, wx_ref, bx_ref, m0, rows, half)
        se = xe * (y_ref[re, :] + sign * pe + ve * skip)
        so = xo * (y_ref[ro, :] + sign * po + vo * skip)
        s_ref[re, :] = se
        s_ref[ro, :] = so
        sb_ref[re, :] = se.astype(BF16)
        sb_ref[ro, :] = so.astype(BF16)
        alt_e = alt_e + jnp.sum(se * sign, axis=0, keepdims=True)
        alt_o = alt_o + jnp.sum(so * sign, axis=0, keepdims=True)
    salt_ref[0:1, :] = alt_e
    salt_ref[1:2, :] = alt_o


def _hy_post_kernel(zx_ref, y_ref, s_ref, salt_ref, hf_ref, hb_ref, af_ref, ab_ref, wx_ref, bx_ref,
                    skip_ref, *rest, seq):
    o_ref = rest[-1]
    half = seq // 2
    pe, po = _mid_terms(salt_ref, hf_ref, hb_ref, af_ref, ab_ref, 2 * seq)
    skip = skip_ref[...]
    for m0, rows in _hy_chunks(half):
        re, ro = slice(m0, m0 + rows), slice(half + m0, half + m0 + rows)
        sign = _alt_sign(m0, rows)
        xe, xo = _short_conv_parity(zx_ref, wx_ref, bx_ref, m0, rows, half)
        ye = y_ref[re, :] + sign * pe + s_ref[re, :] * skip
        yo = y_ref[ro, :] + sign * po + s_ref[ro, :] * skip
        o_ref[pl.ds(2 * m0, rows, stride=2), :] = xe * ye
        o_ref[pl.ds(2 * m0 + 1, rows, stride=2), :] = xo * yo


def _hyena(z, z_rb0, conv_w, conv_b, skip, filt, fmat, gmat, tw, *, layer, seq, out_rows, out_rb0,
           alias_out=None):
    hb, asum, halt = filt
    twc, tws = tw
    half = seq // 2
    kf = min(HY_KF, half)
    n_ch = half // kf
    n_cols = HY_WIDTH // HY_COLS
    ew = HEAD_DIM
    sub = HY_COLS // ew
    n_ew = HY_WIDTH // ew
    hy0 = V_END // ew

    def zcols(which):
        return pl.BlockSpec((seq, ew), lambda b, e: (z_rb0 + b, hy0 + which * n_ew + e))

    def cw(which):
        return pl.BlockSpec((None, HY_SHORT, ew), lambda b, e: (layer, 0, which * n_ew + e))

    def cb(which):
        return pl.BlockSpec((None, 1, ew), lambda b, e: (layer, 0, which * n_ew + e))

    def fstat(order, direction, rows):
        return pl.BlockSpec((None, rows, ew),
                            lambda b, e: (layer, 0, ((order * n_cols + e // sub) * 2 + direction) * sub + e % sub))

    def skip_spec(order):
        return pl.BlockSpec((None, 1, ew), lambda b, e: (layer * HY_ORDER + order, 0, e))

    def act_col(b, e):
        return ((e // sub) * BATCH + b) * sub + e % sub

    seq_blk = pl.BlockSpec((seq, ew), lambda b, e: (0, act_col(b, e)))
    alt_spec = pl.BlockSpec((2, ew), lambda b, e: (0, act_col(b, e)))
    ew_params = _params("arbitrary", "arbitrary")
    act = jax.ShapeDtypeStruct((seq, BATCH * HY_WIDTH), F32)
    act_b = jax.ShapeDtypeStruct((seq, BATCH * HY_WIDTH), BF16)
    alt_shape = jax.ShapeDtypeStruct((2, BATCH * HY_WIDTH), F32)

    def spectral(ub, order):
        twid = pl.BlockSpec((kf, HEAD_DIM), lambda j, c: (c, 0))
        return pl.pallas_call(
            functools.partial(_hy_spec_kernel, seq=seq, kf=kf),
            grid=(n_cols, n_ch),
            in_specs=[
                pl.BlockSpec((seq, BATCH * HY_COLS), lambda j, c: (0, j), pipeline_mode=pl.Buffered(1)),
                pl.BlockSpec((None, seq, 2 * HY_COLS), lambda j, c: (layer, 0, order * n_cols + j),
                             pipeline_mode=pl.Buffered(1)),
                pl.BlockSpec((None, 1, 2 * HY_COLS), lambda j, c: (layer, 0, order * n_cols + j)),
                pl.BlockSpec((None, 2 * kf, half), lambda j, c: (c, 0, 0)),
                pl.BlockSpec((None, half, 2 * kf), lambda j, c: (c, 0, 0)),
                twid, twid,
            ],
            out_specs=pl.BlockSpec((seq, BATCH * HY_COLS), lambda j, c: (0, j)),
            out_shape=act,
            compiler_params=_params("arbitrary", "arbitrary"),
            name=f"hy_spec{order}_l{seq}",
        )(ub, hb, asum, fmat, gmat, twc, tws)

    ub, ualt = pl.pallas_call(
        functools.partial(_hy_pre_kernel, seq=seq),
        grid=(BATCH, n_ew),
        in_specs=[zcols(0), cw(0), cb(0)],
        out_specs=[seq_blk, alt_spec],
        out_shape=[act_b, alt_shape],
        compiler_params=ew_params,
        name=f"hy_pre_l{seq}",
    )(z, conv_w, conv_b)
    y0 = spectral(ub, 0)
    s, sb, salt = pl.pallas_call(
        functools.partial(_hy_mid_kernel, seq=seq),
        grid=(BATCH, n_ew),
        in_specs=[zcols(0), zcols(1), seq_blk, alt_spec, fstat(0, 0, 2), fstat(0, 1, 2), fstat(0, 0, 1),
                  fstat(0, 1, 1), cw(0), cb(0), cw(1), cb(1), skip_spec(0)],
        out_specs=[seq_blk, seq_blk, alt_spec],
        out_shape=[act, act_b, alt_shape],
        compiler_params=ew_params,
        name=f"hy_mid_l{seq}",
    )(z, z, y0, ualt, halt, halt, asum, asum, conv_w, conv_b, conv_w, conv_b, skip)
    y1 = spectral(sb, 1)
    in_specs = [zcols(2), seq_blk, seq_blk, alt_spec, fstat(1, 0, 2), fstat(1, 1, 2), fstat(1, 0, 1),
                fstat(1, 1, 1), cw(2), cb(2), skip_spec(1)]
    args = [z, y1, s, salt, halt, halt, asum, asum, conv_w, conv_b, skip]
    aliases = {}
    if alias_out is not None:
        in_specs.append(pl.BlockSpec(memory_space=pl.ANY))
        args.append(alias_out)
        aliases = {len(args) - 1: 0}
    return pl.pallas_call(
        functools.partial(_hy_post_kernel, seq=seq),
        grid=(BATCH, n_ew),
        in_specs=in_specs,
        out_specs=pl.BlockSpec((seq, ew), lambda b, e: (out_rb0 + b, e)),
        out_shape=jax.ShapeDtypeStruct((out_rows, HY_WIDTH), F32),
        input_output_aliases=aliases,
        compiler_params=ew_params,
        name=f"hy_post_l{seq}",
    )(*args)


def _outproj_kernel(a_ref, b_ref, c_ref, w_ref, x_ref, mod_ref, g_ref, o_ref):
    y = jnp.dot(a_ref[...], w_ref[:SGU_WIDTH, :], preferred_element_type=F32)
    y += jnp.dot(b_ref[...], w_ref[SGU_WIDTH:SGU_WIDTH + ATTN_WIDTH, :], preferred_element_type=F32)
    y += jnp.dot(c_ref[...].astype(BF16), w_ref[SGU_WIDTH + ATTN_WIDTH:, :], preferred_element_type=F32)
    o_ref[...] = x_ref[...] + mod_ref[5:6, :] * _rms(y, g_ref[3:4, :])


def _outproj(a, b, c, w_out, x, mods, norm_g, *, layer, n_tiles):
    return pl.pallas_call(
        _outproj_kernel,
        grid=(n_tiles,),
        in_specs=[
            pl.BlockSpec((TM, SGU_WIDTH), lambda i: (i, 0)),
            pl.BlockSpec((TM, ATTN_WIDTH), lambda i: (i, 0)),
            pl.BlockSpec((TM, HY_WIDTH), lambda i: (i, 0)),
            pl.BlockSpec((D_MIX, D_MODEL), lambda i: (0, 0), pipeline_mode=pl.Buffered(1)),
            pl.BlockSpec((TM, D_MODEL), lambda i: (i, 0)),
            pl.BlockSpec((None, None, N_MOD, D_MODEL), lambda i: (layer, _mod_row(i), 0, 0)),
            pl.BlockSpec((None, 6, D_MODEL), lambda i: (layer, 0, 0)),
        ],
        out_specs=pl.BlockSpec((TM, D_MODEL), lambda i: (i, 0)),
        out_shape=jax.ShapeDtypeStruct((n_tiles * TM, D_MODEL), F32),
        compiler_params=_params("arbitrary"),
        name="outproj",
    )(a, b, c, w_out, x, mods, norm_g)


def _rope_tables():
    rows = SEQ // GRID_W
    row = jnp.broadcast_to(jnp.arange(rows)[:, None], (rows, GRID_W)).reshape(SEQ).astype(F32)
    col = jnp.broadcast_to(jnp.arange(GRID_W)[None, :], (rows, GRID_W)).reshape(SEQ).astype(F32)
    half = HEAD_DIM // 2
    inv = ROPE_BASE ** (-jnp.arange(0, half, 2, dtype=F32) / half)
    ang_r = row[:, None] * inv
    ang_c = col[:, None] * inv
    ang = jnp.concatenate([ang_r, ang_r, ang_c, ang_c], axis=-1)
    cos, sin = jnp.cos(ang), jnp.sin(ang)
    first = (jnp.arange(HEAD_DIM) % half) < (half // 2)
    sa = jnp.where(first, -sin, 0.0)
    sb = jnp.where(first, 0.0, sin)

    def stack(lat, ctx_value):
        return jnp.concatenate([lat] * BATCH + [jnp.full((N_CTX, HEAD_DIM), ctx_value, F32)], axis=0)

    return stack(cos, 1.0), stack(sa, 0.0), stack(sb, 0.0)


def kernel(x, c, ctx, c_ctx, w_ada, b_ada, norm_g, ffn1_gate, ffn1_up, ffn1_down, w_in,
           sgu_g, sgu_w, sgu_b, attn_sink, hy_conv_w, hy_conv_b, hy_f_w1, hy_f_b1, hy_f_w2,
           hy_f_b2, hy_f_w3, hy_freq, hy_skip, w_out, ffn2_gate, ffn2_up, ffn2_down):
    xs = jnp.concatenate([x.reshape(N_LAT, D_MODEL), ctx.reshape(N_CTX, D_MODEL)], axis=0)
    cc = jnp.concatenate([c, c_ctx[None], jnp.zeros((MOD_ROWS - BATCH - 1, D_MODEL), F32)], axis=0)
    mods = _ada_mods(cc, w_ada, b_ada)

    w1 = [w[0].astype(BF16) for w in (ffn1_gate, ffn1_up, ffn1_down)]
    cos, sa, sb = _rope_tables()
    conv_b = hy_conv_b.reshape(DEPTH, 1, (HY_ORDER + 1) * HY_WIDTH)
    skip = hy_skip.reshape(DEPTH * HY_ORDER, 1, HY_WIDTH)
    filt_w = (hy_f_w1, hy_f_b1, hy_f_w2, hy_f_b2, hy_f_w3, hy_freq)
    filt_lat = _hyena_filters(SEQ, *filt_w)
    filt_ctx = _hyena_filters(CTX_LEN, *filt_w)
    f_lat, g_lat = _dft_matrices(SEQ // 2, min(HY_KF, SEQ // 2))
    f_ctx, g_ctx = _dft_matrices(CTX_LEN // 2, min(HY_KF, CTX_LEN // 2))
    tw_lat, tw_ctx = _twiddles(SEQ), _twiddles(CTX_LEN)

    all_tiles = N_TOK // TM
    lat_tiles = N_LAT // TM
    ctx_rb = N_LAT // CTX_LEN

    for i in range(DEPTH):
        last = i == DEPTH - 1
        n_tiles = lat_tiles if last else all_tiles
        n_rows = n_tiles * TM

        jobs = [(w, i) for w in (ffn2_gate, ffn2_up, ffn2_down, w_in, w_out)]
        xs, (w2g, w2u, w2d, w_in_b, w_out_b) = _ffn(xs, mods, norm_g, *w1, jobs, layer=i, k=0, n_tiles=all_tiles)
        z = _inproj(xs, mods, norm_g, w_in_b, cos, sa, sb, layer=i)

        a_out = _sgu(z, sgu_g, sgu_w, sgu_b, layer=i, n_rows=n_rows)
        b_out = _attention(z, attn_sink, layer=i, n_rows=n_rows)
        c_out = _hyena(z, 0, hy_conv_w, conv_b, skip, filt_lat, f_lat, g_lat, tw_lat, layer=i, seq=SEQ,
                       out_rows=n_rows, out_rb0=0)
        if not last:
            c_out = _hyena(z, ctx_rb, hy_conv_w, conv_b, skip, filt_ctx, f_ctx, g_ctx, tw_ctx, layer=i,
                           seq=CTX_LEN, out_rows=n_rows, out_rb0=ctx_rb, alias_out=c_out)

        xs = _outproj(a_out, b_out, c_out, w_out_b, xs, mods, norm_g, layer=i, n_tiles=n_tiles)
        jobs = [] if last else [(w, i + 1) for w in (ffn1_gate, ffn1_up, ffn1_down)]
        xs, w1 = _ffn(xs, mods, norm_g, w2g, w2u, w2d, jobs, layer=i, k=2, n_tiles=n_tiles)

    return xs[:N_LAT].reshape(BATCH, SEQ, D_MODEL)
```

```python
import functools
import math

import jax
import jax.numpy as jnp
from jax import lax
from jax.experimental import pallas as pl
from jax.experimental.pallas import tpu as pltpu

D_MODEL = 2048
BATCH = 2
SEQ = 4096
DEPTH = 4
GRID_W = 64
CTX_LEN = 256
HEAD_DIM = 128
NORM_EPS = 1e-6
N_MOD = 9
D_FF = 5632
SGU_GROUPS = 4
SGU_WIDTH = SGU_GROUPS * HEAD_DIM
CHUNK = 128
N_Q_HEADS = 8
N_KV_HEADS = 2
Q_GROUP = N_Q_HEADS // N_KV_HEADS
ATTN_WIDTH = N_Q_HEADS * HEAD_DIM
KV_WIDTH = N_KV_HEADS * HEAD_DIM
WINDOW = 128
BLOCK = 128
ROPE_BASE = 10000.0
NEG_INF = -1e30
HY_GROUPS = 4
HY_WIDTH = HY_GROUPS * HEAD_DIM
HY_ORDER = 2
HY_SHORT = 3
HY_BANDS = 16
HY_EMB = 1 + 2 * HY_BANDS
HY_HIDDEN = 64
HY_TARGET = 1e-2
HY_FAST_DECAY = 0.3
HY_SLOW_DECAY = 1.5
D_MIX = SGU_WIDTH + ATTN_WIDTH + HY_WIDTH
A_END = 2 * SGU_WIDTH
Q_END = A_END + ATTN_WIDTH
K_END = Q_END + KV_WIDTH
V_END = K_END + KV_WIDTH
IN_COLS = V_END + (HY_ORDER + 1) * HY_WIDTH

N_LAT = BATCH * SEQ
N_CTX = BATCH * CTX_LEN
N_TOK = N_LAT + N_CTX
MOD_ROWS = 8

TM = 512
TF = 512
FFN_ROWS = 32
BF16_SUBLANES = 16
TN_IN = 1024
TN_ADA = 1024
HY_COLS = 256
HY_FILT_COLS = 2 * HY_COLS
HY_KF = 256
HY_PHASES = 4
HY_ROWS = 512
HY_INV_ROWS = 1024
HY_EMB_PAD = 128
VMEM_LIMIT = 56 * 1024 * 1024

F32 = jnp.float32
BF16 = jnp.bfloat16


def _params(*sem):
    return pltpu.CompilerParams(dimension_semantics=sem, vmem_limit_bytes=VMEM_LIMIT)


def _mod_row(i):
    return jnp.minimum(i // (SEQ // TM), BATCH)


def _silu(a):
    return a / (1.0 + jnp.exp(-a))


def _rms(x, g):
    return x * lax.rsqrt(jnp.mean(x * x, axis=-1, keepdims=True) + NORM_EPS) * g


def _ada_kernel(c_ref, w_ref, b_ref, o_ref):
    s = _silu(c_ref[...]).astype(BF16)
    o_ref[...] = jnp.dot(s, w_ref[...].astype(BF16), preferred_element_type=F32) + b_ref[...]


def _ada_mods(cc, w_ada, b_ada):
    n_out = N_MOD * D_MODEL
    out = pl.pallas_call(
        _ada_kernel,
        grid=(DEPTH, n_out // TN_ADA),
        in_specs=[
            pl.BlockSpec((MOD_ROWS, D_MODEL), lambda l, j: (0, 0)),
            pl.BlockSpec((None, D_MODEL, TN_ADA), lambda l, j: (l, 0, j)),
            pl.BlockSpec((None, 1, TN_ADA), lambda l, j: (l, 0, j)),
        ],
        out_specs=pl.BlockSpec((None, MOD_ROWS, TN_ADA), lambda l, j: (l, 0, j)),
        out_shape=jax.ShapeDtypeStruct((DEPTH, MOD_ROWS, n_out), F32),
        compiler_params=_params("arbitrary", "arbitrary"),
        name="ada_mods",
    )(cc, w_ada, b_ada.reshape(DEPTH, 1, n_out))
    return out.reshape(DEPTH, MOD_ROWS, N_MOD, D_MODEL)


def _ffn_kernel(*refs, k, nf, n_jobs):
    x_ref, mod_ref, g_ref, wg_ref, wu_ref, wd_ref = refs[:6]
    cast_src = refs[6:6 + n_jobs]
    o_ref = refs[6 + n_jobs]
    cast_dst = refs[7 + n_jobs:7 + 2 * n_jobs]
    h_ref, acc_ref = refs[7 + 2 * n_jobs:]
    f = pl.program_id(1)

    for src, dst in zip(cast_src, cast_dst):
        dst[...] = src[...].astype(BF16)

    @pl.when(f == 0)
    def _():
        g = g_ref[2 * k:2 * k + 1, :]
        m1 = 1.0 + mod_ref[3 * k:3 * k + 1, :]
        m2 = mod_ref[3 * k + 1:3 * k + 2, :]
        for r0 in range(0, TM, FFN_ROWS):
            rs = slice(r0, r0 + FFN_ROWS)
            h_ref[rs, :] = (_rms(x_ref[rs, :], g) * m1 + m2).astype(BF16)
        acc_ref[...] = jnp.zeros_like(acc_ref)

    h = h_ref[...]
    a = jnp.dot(h, wg_ref[...], preferred_element_type=F32)
    b = jnp.dot(h, wu_ref[...], preferred_element_type=F32)
    p = (_silu(a) * b).astype(BF16)
    acc_ref[...] += jnp.dot(p, wd_ref[...], preferred_element_type=F32)

    @pl.when(f == nf - 1)
    def _():
        g = g_ref[2 * k + 1:2 * k + 2, :]
        gate = 0.5 * mod_ref[3 * k + 2:3 * k + 3, :]
        for r0 in range(0, TM, FFN_ROWS):
            rs = slice(r0, r0 + FFN_ROWS)
            o_ref[rs, :] = x_ref[rs, :] + gate * _rms(acc_ref[rs, :], g)


def _ffn(x, mods, norm_g, wg, wu, wd, cast_jobs, *, layer, k, n_tiles):
    nf = D_FF // TF
    n_steps = n_tiles * nf
    in_specs = [
        pl.BlockSpec((TM, D_MODEL), lambda i, f: (i, 0)),
        pl.BlockSpec((None, None, N_MOD, D_MODEL), lambda i, f: (layer, _mod_row(i), 0, 0)),
        pl.BlockSpec((None, 6, D_MODEL), lambda i, f: (layer, 0, 0)),
        pl.BlockSpec((D_MODEL, TF), lambda i, f: (0, f)),
        pl.BlockSpec((D_MODEL, TF), lambda i, f: (0, f)),
        pl.BlockSpec((TF, D_MODEL), lambda i, f: (f, 0)),
    ]
    out_specs = [pl.BlockSpec((TM, D_MODEL), lambda i, f: (i, 0))]
    out_shape = [jax.ShapeDtypeStruct((n_tiles * TM, D_MODEL), F32)]
    for w, src_layer in cast_jobs:
        n_rows, n_cols = w.shape[1:]
        rb = BF16_SUBLANES * pl.cdiv(pl.cdiv(n_rows, n_steps), BF16_SUBLANES)
        assert n_rows % rb == 0
        last_blk = n_rows // rb - 1

        def step_block(i, f, last_blk=last_blk):
            return jnp.minimum(i * nf + f, last_blk)

        in_specs.append(pl.BlockSpec((None, rb, n_cols),
                                     lambda i, f, sl=src_layer, sb=step_block: (sl, sb(i, f), 0)))
        out_specs.append(pl.BlockSpec((rb, n_cols), lambda i, f, sb=step_block: (sb(i, f), 0)))
        out_shape.append(jax.ShapeDtypeStruct((n_rows, n_cols), BF16))
    outs = pl.pallas_call(
        functools.partial(_ffn_kernel, k=k, nf=nf, n_jobs=len(cast_jobs)),
        grid=(n_tiles, nf),
        in_specs=in_specs,
        out_specs=out_specs,
        out_shape=out_shape,
        scratch_shapes=[pltpu.VMEM((TM, D_MODEL), BF16), pltpu.VMEM((TM, D_MODEL), F32)],
        compiler_params=_params("arbitrary", "arbitrary"),
        name=f"ffn{k}",
    )(x, mods, norm_g, wg, wu, wd, *[w for w, _ in cast_jobs])
    return outs[0], list(outs[1:])


def _rope(t, cos, sa, sb):
    return t * cos + pltpu.roll(t, 3 * HEAD_DIM // 4, 1) * sa + pltpu.roll(t, HEAD_DIM // 4, 1) * sb


def _inproj_kernel(x_ref, mod_ref, g_ref, w_ref, cos_ref, sa_ref, sb_ref, o_ref):
    y = _rms(x_ref[...], g_ref[2:3, :])
    h = (y * (1.0 + mod_ref[3:4, :]) + mod_ref[4:5, :]).astype(BF16)
    cos, sa, sb = cos_ref[...], sa_ref[...], sb_ref[...]
    n_rope = (K_END - A_END) // HEAD_DIM
    for n in range(IN_COLS // TN_IN):
        c0 = n * TN_IN
        z = jnp.dot(h, w_ref[:, c0:c0 + TN_IN], preferred_element_type=F32)
        for hd in range(TN_IN // HEAD_DIM):
            col = c0 + hd * HEAD_DIM
            zs = z[:, hd * HEAD_DIM:(hd + 1) * HEAD_DIM]
            if A_END <= col < A_END + n_rope * HEAD_DIM:
                zs = _rope(zs, cos, sa, sb)
            o_ref[:, col:col + HEAD_DIM] = zs


def _inproj(x, mods, norm_g, w_in, cos, sa, sb, *, layer):
    n_tiles = N_TOK // TM
    tab = pl.BlockSpec((TM, HEAD_DIM), lambda i: (i, 0))
    return pl.pallas_call(
        _inproj_kernel,
        grid=(n_tiles,),
        in_specs=[
            pl.BlockSpec((TM, D_MODEL), lambda i: (i, 0)),
            pl.BlockSpec((None, None, N_MOD, D_MODEL), lambda i: (layer, _mod_row(i), 0, 0)),
            pl.BlockSpec((None, 6, D_MODEL), lambda i: (layer, 0, 0)),
            pl.BlockSpec((D_MODEL, IN_COLS), lambda i: (0, 0), pipeline_mode=pl.Buffered(1)),
            tab, tab, tab,
        ],
        out_specs=pl.BlockSpec((TM, IN_COLS), lambda i: (i, 0)),
        out_shape=jax.ShapeDtypeStruct((N_TOK, IN_COLS), F32),
        compiler_params=_params("arbitrary"),
        name="inproj",
    )(x, mods, norm_g, w_in, cos, sa, sb)


def _sgu_kernel(z_ref, g_ref, w_ref, b_ref, o_ref, *, rows):
    inv_sqrt2 = 1.0 / math.sqrt(2.0)
    for c in range(rows // CHUNK):
        rs = slice(c * CHUNK, (c + 1) * CHUNK)
        a = z_ref[rs, :]
        a = a * (lax.erf(a * inv_sqrt2) + 1.0) * 0.5
        u, v = a[:, :SGU_WIDTH], a[:, SGU_WIDTH:]
        mu = jnp.mean(v, axis=-1, keepdims=True)
        vc = v - mu
        var = jnp.mean(vc * vc, axis=-1, keepdims=True)
        v = (vc * lax.rsqrt(var + NORM_EPS) * g_ref[...]).astype(BF16)
        for g in range(SGU_GROUPS):
            cs = slice(g * HEAD_DIM, (g + 1) * HEAD_DIM)
            mixed = jnp.dot(w_ref[g].astype(BF16), v[:, cs], preferred_element_type=F32) + b_ref[:, g:g + 1]
            o_ref[rs, cs] = (u[:, cs] * mixed).astype(o_ref.dtype)


def _sgu(z, sgu_g, sgu_w, sgu_b, *, layer, n_rows):
    rows = TM
    return pl.pallas_call(
        functools.partial(_sgu_kernel, rows=rows),
        grid=(n_rows // rows,),
        in_specs=[
            pl.BlockSpec((rows, A_END), lambda i: (i, 0)),
            pl.BlockSpec((None, 1, SGU_WIDTH), lambda i: (layer, 0, 0)),
            pl.BlockSpec((None, SGU_GROUPS, CHUNK, CHUNK), lambda i: (layer, 0, 0, 0)),
            pl.BlockSpec((None, CHUNK, SGU_GROUPS), lambda i: (layer, 0, 0)),
        ],
        out_specs=pl.BlockSpec((rows, SGU_WIDTH), lambda i: (i, 0)),
        out_shape=jax.ShapeDtypeStruct((n_rows, SGU_WIDTH), BF16),
        compiler_params=_params("arbitrary"),
        name="sgu",
    )(z, sgu_g.reshape(DEPTH, 1, SGU_WIDTH), sgu_w, jnp.swapaxes(sgu_b, 1, 2))


def _attn_kernel(sink_ref, q_ref, kp_ref, kc_ref, kn_ref, vp_ref, vc_ref, vn_ref, kx_ref, vx_ref,
                 o_ref, *, layer, nb, n_lat_blocks):
    s_id = pl.program_id(0)
    is_lat = s_id < n_lat_blocks
    n = s_id % nb
    n_keys = CTX_LEN + 3 * BLOCK
    n_rows = Q_GROUP * BLOCK
    row_id = lax.broadcasted_iota(jnp.int32, (n_rows, 1), 0)
    row = lax.broadcasted_iota(jnp.int32, (n_rows, n_keys), 0) & (BLOCK - 1)
    col = lax.broadcasted_iota(jnp.int32, (n_rows, n_keys), 1)
    rel = col - (CTX_LEN + BLOCK)
    lo = jnp.where(is_lat, -n * BLOCK, 4 * BLOCK)
    hi = jnp.where(is_lat, (nb - n) * BLOCK, -4 * BLOCK)
    dist = rel - row
    valid = (col < CTX_LEN) | ((dist >= -WINDOW) & (dist <= WINDOW) & (rel >= lo) & (rel < hi))
    scale = HEAD_DIM ** -0.5
    nt = (((1,), (1,)), ((), ()))
    for kh in range(N_KV_HEADS):
        ks = slice(kh * HEAD_DIM, (kh + 1) * HEAD_DIM)
        keys = jnp.concatenate([kx_ref[:, ks], kp_ref[:, ks], kc_ref[:, ks], kn_ref[:, ks]], axis=0).astype(BF16)
        vals = jnp.concatenate([vx_ref[:, ks], vp_ref[:, ks], vc_ref[:, ks], vn_ref[:, ks]], axis=0).astype(BF16)
        heads = [kh * Q_GROUP + g for g in range(Q_GROUP)]
        q = jnp.concatenate([q_ref[:, hd * HEAD_DIM:(hd + 1) * HEAD_DIM] for hd in heads], axis=0).astype(BF16)
        sink = jnp.full((n_rows, 1), sink_ref[layer, heads[0]], F32)
        for g, hd in enumerate(heads[1:], start=1):
            sink = jnp.where(row_id >= g * BLOCK, sink_ref[layer, hd], sink)
        s = lax.dot_general(q, keys, nt, preferred_element_type=F32) * scale
        s = jnp.where(valid, s, NEG_INF)
        m = jnp.maximum(jnp.max(s, axis=-1, keepdims=True), sink)
        p = jnp.exp(s - m)
        denom = jnp.sum(p, axis=-1, keepdims=True) + jnp.exp(sink - m)
        probs = (p / denom).astype(BF16)
        out = jnp.dot(probs, vals, preferred_element_type=F32).astype(o_ref.dtype)
        for g, hd in enumerate(heads):
            o_ref[:, hd * HEAD_DIM:(hd + 1) * HEAD_DIM] = out[g * BLOCK:(g + 1) * BLOCK]


def _attention(z, attn_sink, *, layer, n_rows):
    nb = SEQ // BLOCK
    n_lat_blocks = N_LAT // BLOCK
    n_blocks = n_rows // BLOCK
    ctx_blocks = CTX_LEN // BLOCK
    q_col = A_END // ATTN_WIDTH
    k_col = Q_END // KV_WIDTH
    v_col = K_END // KV_WIDTH

    def lat(s):
        return s < n_lat_blocks

    def band(off, col):
        def index(s):
            b = s // nb
            nbr = jnp.clip(s % nb + off, 0, nb - 1) + b * nb
            return (jnp.where(lat(s), nbr, s), col)
        return pl.BlockSpec((BLOCK, KV_WIDTH), index)

    def ctx(col):
        def index(s):
            b = jnp.where(lat(s), s // nb, (s - n_lat_blocks) // ctx_blocks)
            return (N_LAT // CTX_LEN + b, col)
        return pl.BlockSpec((CTX_LEN, KV_WIDTH), index)

    return pl.pallas_call(
        functools.partial(_attn_kernel, layer=layer, nb=nb, n_lat_blocks=n_lat_blocks),
        grid=(n_blocks,),
        in_specs=[
            pl.BlockSpec(memory_space=pltpu.SMEM),
            pl.BlockSpec((BLOCK, ATTN_WIDTH), lambda s: (s, q_col)),
            band(-1, k_col), band(0, k_col), band(1, k_col),
            band(-1, v_col), band(0, v_col), band(1, v_col),
            ctx(k_col), ctx(v_col),
        ],
        out_specs=pl.BlockSpec((BLOCK, ATTN_WIDTH), lambda s: (s, 0)),
        out_shape=jax.ShapeDtypeStruct((n_rows, ATTN_WIDTH), BF16),
        compiler_params=_params("arbitrary"),
        name="attn",
    )(attn_sink, z, z, z, z, z, z, z, z, z)


def _short_conv_phases(ref, w_ref, b_ref, m0, rows, n_m):
    ph = HY_PHASES
    z = [ref[pl.ds(ph * m0 + r, rows, stride=ph), :] for r in range(ph)]
    m = lax.broadcasted_iota(jnp.int32, (rows, 1), 0) + m0
    if m0 == 0:
        z_prev = jnp.where(m == 0, 0.0, pltpu.roll(z[ph - 1], 1, 0))
    else:
        z_prev = ref[pl.ds(ph * m0 - 1, rows, stride=ph), :]
    if m0 + rows == n_m:
        z_next = jnp.where(m == n_m - 1, 0.0, pltpu.roll(z[0], rows - 1, 0))
    else:
        z_next = ref[pl.ds(ph * m0 + ph, rows, stride=ph), :]
    ext = [z_prev] + z + [z_next]
    w0, w1, w2, b = w_ref[0:1, :], w_ref[1:2, :], w_ref[2:3, :], b_ref[...]
    return [ext[r] * w0 + ext[r + 1] * w1 + ext[r + 2] * w2 + b for r in range(ph)]


def _cmul(a, b):
    return a[0] * b[0] - a[1] * b[1], a[0] * b[1] + a[1] * b[0]


def _cadd(a, b):
    return a[0] + b[0], a[1] + b[1]


def _csub(a, b):
    return a[0] - b[0], a[1] - b[1]


def _conj(a):
    return a[0], -a[1]


def _split_fwd(e, o, w):
    t = _cmul(w, o)
    return _cadd(e, t), _conj(_csub(e, t))


def _split_inv(pp, pm, w):
    cm = _conj(pm)
    return _cadd(pp, cm), _cmul(_csub(pp, cm), _conj(w))


def _alt_sign(r0, rows):
    t = lax.broadcasted_iota(jnp.int32, (rows, 1), 0) + r0
    return jnp.where((t & 1) == 0, 1.0, -1.0)


def _hy_chunks(seq, chunk=None):
    rows = min(chunk or HY_ROWS, seq)
    return [(r * rows, rows) for r in range(seq // rows)]


def _filt_kernel(feat_ref, w1_ref, b1_ref, w2_ref, b2_ref, w3_ref, fr_ref, dl_ref,
                 hb_ref, asum_ref, hnyq_ref, hd_ref, *, seq):
    hp = lax.Precision.HIGHEST

    @pl.when(pl.program_id(1) == 0)
    def _():
        fr = fr_ref[...]
        hd = jnp.sin(fr * (jnp.dot(feat_ref[...], w1_ref[...], precision=hp, preferred_element_type=F32)
                           + b1_ref[...]))
        hd_ref[...] = jnp.sin(fr * (jnp.dot(hd, w2_ref[...], precision=hp, preferred_element_type=F32)
                                    + b2_ref[...]))

    q = seq // HY_PHASES
    asum = jnp.zeros((1, HY_FILT_COLS), F32)
    alt = [jnp.zeros((1, HY_FILT_COLS), F32) for _ in range(HY_PHASES)]
    for r0, rows in _hy_chunks(seq, min(HY_ROWS, q)):
        rs = slice(r0, r0 + rows)
        h = jnp.dot(hd_ref[rs, :].astype(BF16), w3_ref[...].astype(BF16), preferred_element_type=F32)
        h = h * jnp.exp(-feat_ref[rs, 0:1] * dl_ref[...])
        asum = asum + jnp.sum(jnp.abs(h), axis=0, keepdims=True)
        alt[r0 // q] = alt[r0 // q] + jnp.sum(h * _alt_sign(r0 % q, rows), axis=0, keepdims=True)
        hb_ref[rs, :] = h.astype(BF16)
    asum_ref[...] = asum
    for r in range(HY_PHASES):
        hnyq_ref[r:r + 1, :] = alt[r]


def _hyena_filters(seq, w1, b1, w2, b2, w3, freq):
    n_tiles = HY_WIDTH // HY_COLS
    w3 = w3.reshape(DEPTH, HY_HIDDEN, HY_ORDER, 2, n_tiles, HY_COLS)
    w3 = jnp.swapaxes(w3, 3, 4).reshape(DEPTH, HY_HIDDEN, HY_ORDER * 2 * HY_WIDTH)
    t = jnp.linspace(0.0, 1.0, seq, dtype=F32)[:, None]
    w = (2.0 * math.pi / seq) * jnp.arange(seq, dtype=F32)[:, None]
    bands = jnp.linspace(1e-4, HY_BANDS - 1, HY_BANDS, dtype=F32)[None, :]
    feat = jnp.concatenate([t, jnp.cos(bands * w), -jnp.sin(bands * w)], axis=-1)
    feat = jnp.pad(feat, ((0, 0), (0, HY_EMB_PAD - HY_EMB)))
    feat = jnp.concatenate([feat[r::HY_PHASES] for r in range(HY_PHASES)], axis=0)
    w1p = jnp.pad(w1, ((0, 0), (0, HY_EMB_PAD - HY_EMB), (0, 0)))
    deltas = jnp.abs(jnp.linspace(math.log(HY_TARGET) / HY_FAST_DECAY,
                                  math.log(HY_TARGET) / HY_SLOW_DECAY, HY_WIDTH, dtype=F32))
    deltas = jnp.broadcast_to(deltas.reshape(1, n_tiles, 1, HY_COLS), (HY_ORDER, n_tiles, 2, HY_COLS))
    deltas = deltas.reshape(1, HY_ORDER * 2 * HY_WIDTH)
    n_cols = HY_ORDER * 2 * HY_WIDTH

    def row(a):
        return a.reshape(DEPTH, 1, HY_HIDDEN)

    vec = pl.BlockSpec((None, 1, HY_HIDDEN), lambda l, j: (l, 0, 0))
    stat = pl.BlockSpec((None, 1, HY_FILT_COLS), lambda l, j: (l, 0, j))
    return pl.pallas_call(
        functools.partial(_filt_kernel, seq=seq),
        grid=(DEPTH, n_cols // HY_FILT_COLS),
        in_specs=[
            pl.BlockSpec((seq, HY_EMB_PAD), lambda l, j: (0, 0)),
            pl.BlockSpec((None, HY_EMB_PAD, HY_HIDDEN), lambda l, j: (l, 0, 0)),
            vec,
            pl.BlockSpec((None, HY_HIDDEN, HY_HIDDEN), lambda l, j: (l, 0, 0)),
            vec,
            pl.BlockSpec((None, HY_HIDDEN, HY_FILT_COLS), lambda l, j: (l, 0, j)),
            vec,
            pl.BlockSpec((1, HY_FILT_COLS), lambda l, j: (0, j)),
        ],
        out_specs=[pl.BlockSpec((None, seq, HY_FILT_COLS), lambda l, j: (l, 0, j)), stat,
                   pl.BlockSpec((None, HY_PHASES, HY_FILT_COLS), lambda l, j: (l, 0, j))],
        out_shape=[jax.ShapeDtypeStruct((DEPTH, seq, n_cols), BF16),
                   jax.ShapeDtypeStruct((DEPTH, 1, n_cols), F32),
                   jax.ShapeDtypeStruct((DEPTH, HY_PHASES, n_cols), F32)],
        scratch_shapes=[pltpu.VMEM((seq, HY_HIDDEN), F32)],
        compiler_params=_params("arbitrary", "arbitrary"),
        name=f"hy_filt_l{seq}",
    )(feat, w1p, row(b1), w2, row(b2), w3, row(freq), deltas)


def _dft_matrices(seq, kf):
    n_fft = 2 * seq
    n_ch = seq // kf
    th = 2.0 * math.pi / n_fft
    c = jnp.arange(n_ch, dtype=jnp.int32)[:, None]
    kl = jnp.arange(kf, dtype=jnp.int32)[:, None]
    n = jnp.arange(seq, dtype=jnp.int32)[None, :]
    a1 = th * ((c * kf * n) % n_fft).astype(F32)
    a2 = th * ((kl * n) % n_fft).astype(F32)
    c1, s1, c2, s2 = jnp.cos(a1), jnp.sin(a1), jnp.cos(a2), jnp.sin(a2)
    f_cos = c1[:, None, :] * c2[None, :, :] - s1[:, None, :] * s2[None, :, :]
    f_sin = s1[:, None, :] * c2[None, :, :] + c1[:, None, :] * s2[None, :, :]
    fmat = jnp.concatenate([f_cos, -f_sin], axis=1).astype(BF16)
    c2t, s2t = c2.T, s2.T
    g_cos = c1[:, :, None] * c2t[None, :, :] - s1[:, :, None] * s2t[None, :, :]
    g_sin = s1[:, :, None] * c2t[None, :, :] + c1[:, :, None] * s2t[None, :, :]
    gmat = jnp.concatenate([g_cos, -g_sin], axis=2).astype(BF16)
    return fmat, gmat


def _twiddles(seq):
    k = jnp.arange(seq // HY_PHASES, dtype=F32)[:, None]
    angles = [2.0 * math.pi * k / seq,
              math.pi * k / seq,
              math.pi * (seq // 2 - k) / seq]
    tabs = []
    for ang in angles:
        ang = jnp.broadcast_to(ang, (seq // HY_PHASES, HEAD_DIM))
        tabs += [jnp.cos(ang), jnp.sin(ang)]
    return jnp.stack(tabs, axis=0)


def _hy_pre_kernel(z_ref, w_ref, b_ref, ub_ref, ualt_ref, *, seq):
    q = seq // HY_PHASES
    alt = [jnp.zeros((1, HEAD_DIM), F32) for _ in range(HY_PHASES)]
    for m0, rows in _hy_chunks(q):
        v = _short_conv_phases(z_ref, w_ref, b_ref, m0, rows, q)
        sign = _alt_sign(m0, rows)
        for r in range(HY_PHASES):
            ub_ref[r * q + m0:r * q + m0 + rows, :] = v[r].astype(BF16)
            alt[r] = alt[r] + jnp.sum(v[r] * sign, axis=0, keepdims=True)
    for r in range(HY_PHASES):
        ualt_ref[r:r + 1, :] = alt[r]


def _hy_spec_kernel(u_ref, h_ref, a_ref, f_ref, g_ref, tw_ref, y_ref, *, seq, kf):
    c = pl.program_id(1)
    cw = HY_COLS
    q = seq // HY_PHASES
    n_fft = 2 * seq

    @pl.when(c == 0)
    def _():
        y_ref[...] = jnp.zeros_like(y_ref)

    f = f_ref[...]

    def twiddle(idx, width):
        rep = width // HEAD_DIM
        return (jnp.concatenate([tw_ref[2 * idx]] * rep, axis=1),
                -jnp.concatenate([tw_ref[2 * idx + 1]] * rep, axis=1))

    def spectrum(ref, width):
        w_l, w_n, w_nh = twiddle(0, width), twiddle(1, width), twiddle(2, width)
        g = []
        for r in range(HY_PHASES):
            d = jnp.dot(f, ref[r * q:(r + 1) * q, :], preferred_element_type=F32)
            g.append((d[:kf], d[kf:]))
        e_k, e_hk = _split_fwd(g[0], g[2], w_l)
        o_k, o_hk = _split_fwd(g[1], g[3], w_l)
        x_k, x_lk = _split_fwd(e_k, o_k, w_n)
        x_hk, x_hpk = _split_fwd(e_hk, o_hk, w_nh)
        return [x_k, x_lk, x_hk, x_hpk]

    row = lax.broadcasted_iota(jnp.int32, (kf, 1), 0)
    a = a_ref[...]
    wgt = jnp.where((row == 0) & (c == 0), 1.0 / n_fft, 2.0 / n_fft) / (a[:, :cw] + a[:, cw:] + 1e-6)
    filt = [((x[0][:, :cw] + x[0][:, cw:]) * wgt, (x[1][:, :cw] - x[1][:, cw:]) * wgt)
            for x in spectrum(h_ref, 2 * cw)]
    sig = spectrum(u_ref, BATCH * cw)
    w_l, w_n, w_nh = twiddle(0, cw), twiddle(1, cw), twiddle(2, cw)
    out = [[] for _ in range(HY_PHASES)]
    for b in range(BATCH):
        bs = slice(b * cw, (b + 1) * cw)
        p = [_cmul((x[0][:, bs], x[1][:, bs]), s) for x, s in zip(sig, filt)]
        q0_k, q1_k = _split_inv(p[0], p[1], w_n)
        q0_hk, q1_hk = _split_inv(p[2], p[3], w_nh)
        r00, r01 = _split_inv(q0_k, q0_hk, w_l)
        r10, r11 = _split_inv(q1_k, q1_hk, w_l)
        for r, val in zip((0, 2, 1, 3), (r00, r01, r10, r11)):
            out[r].append(val)
    for r in range(HY_PHASES):
        spec = jnp.concatenate([jnp.concatenate([v[0] for v in out[r]], axis=1),
                                jnp.concatenate([v[1] for v in out[r]], axis=1)], axis=0).astype(BF16)
        for r0, rows in _hy_chunks(q, HY_INV_ROWS):
            y_ref[r * q + r0:r * q + r0 + rows, :] += jnp.dot(g_ref[r0:r0 + rows, :], spec,
                                                              preferred_element_type=F32)


def _edge_terms(ualt_ref, hf_ref, hb_ref, af_ref, ab_ref, n_fft):
    def edge(ref):
        e = (ref[0:1, :], -ref[2:3, :])
        o = (ref[1:2, :], -ref[3:4, :])
        return _split_fwd(e, o, (math.sqrt(0.5), -math.sqrt(0.5)))

    u_a, u_b = edge(ualt_ref)
    f_a, f_b = edge(hf_ref)
    b_a, b_b = edge(hb_ref)
    p_a = _cmul(u_a, _cadd(f_a, _conj(b_a)))
    p_b = _cmul(u_b, _cadd(f_b, _conj(b_b)))
    q0, q1 = _split_inv(p_a, p_b, (math.sqrt(0.5), -math.sqrt(0.5)))
    scale = (2.0 / n_fft) / (af_ref[...] + ab_ref[...] + 1e-6)
    return [q0[0] * scale, q1[0] * scale, -q0[1] * scale, -q1[1] * scale]


def _hy_mid_kernel(zv_ref, zx_ref, y_ref, ualt_ref, hf_ref, hb_ref, af_ref, ab_ref, wv_ref, bv_ref,
                   wx_ref, bx_ref, skip_ref, s_ref, sb_ref, salt_ref, *, seq):
    q = seq // HY_PHASES
    amp = _edge_terms(ualt_ref, hf_ref, hb_ref, af_ref, ab_ref, 2 * seq)
    alt = [jnp.zeros((1, HEAD_DIM), F32) for _ in range(HY_PHASES)]
    skip = skip_ref[...]
    for m0, rows in _hy_chunks(q):
        sign = _alt_sign(m0, rows)
        v = _short_conv_phases(zv_ref, wv_ref, bv_ref, m0, rows, q)
        x1 = _short_conv_phases(zx_ref, wx_ref, bx_ref, m0, rows, q)
        for r in range(HY_PHASES):
            rs = slice(r * q + m0, r * q + m0 + rows)
            s = x1[r] * (y_ref[rs, :] + sign * amp[r] + v[r] * skip)
            s_ref[rs, :] = s
            sb_ref[rs, :] = s.astype(BF16)
            alt[r] = alt[r] + jnp.sum(s * sign, axis=0, keepdims=True)
    for r in range(HY_PHASES):
        salt_ref[r:r + 1, :] = alt[r]


def _hy_post_kernel(zx_ref, y_ref, s_ref, salt_ref, hf_ref, hb_ref, af_ref, ab_ref, wx_ref, bx_ref,
                    skip_ref, *rest, seq):
    o_ref = rest[-1]
    q = seq // HY_PHASES
    amp = _edge_terms(salt_ref, hf_ref, hb_ref, af_ref, ab_ref, 2 * seq)
    skip = skip_ref[...]
    for m0, rows in _hy_chunks(q):
        sign = _alt_sign(m0, rows)
        x2 = _short_conv_phases(zx_ref, wx_ref, bx_ref, m0, rows, q)
        for r in range(HY_PHASES):
            rs = slice(r * q + m0, r * q + m0 + rows)
            y = y_ref[rs, :] + sign * amp[r] + s_ref[rs, :] * skip
            o_ref[pl.ds(HY_PHASES * m0 + r, rows, stride=HY_PHASES), :] = x2[r] * y


def _hyena(z, z_rb0, conv_w, conv_b, skip, filt, fmat, gmat, tw, *, layer, seq, out_rows, out_rb0,
           alias_out=None):
    hb, asum, halt = filt
    q = seq // HY_PHASES
    kf = min(HY_KF, q)
    n_ch = q // kf
    n_cols = HY_WIDTH // HY_COLS
    ew = HEAD_DIM
    sub = HY_COLS // ew
    n_ew = HY_WIDTH // ew
    hy0 = V_END // ew

    def zcols(which):
        return pl.BlockSpec((seq, ew), lambda b, e: (z_rb0 + b, hy0 + which * n_ew + e))

    def cw(which):
        return pl.BlockSpec((None, HY_SHORT, ew), lambda b, e: (layer, 0, which * n_ew + e))

    def cb(which):
        return pl.BlockSpec((None, 1, ew), lambda b, e: (layer, 0, which * n_ew + e))

    def fstat(order, direction, rows):
        return pl.BlockSpec((None, rows, ew),
                            lambda b, e: (layer, 0, ((order * n_cols + e // sub) * 2 + direction) * sub + e % sub))

    def skip_spec(order):
        return pl.BlockSpec((None, 1, ew), lambda b, e: (layer * HY_ORDER + order, 0, e))

    def act_col(b, e):
        return ((e // sub) * BATCH + b) * sub + e % sub

    seq_blk = pl.BlockSpec((seq, ew), lambda b, e: (0, act_col(b, e)))
    alt_spec = pl.BlockSpec((HY_PHASES, ew), lambda b, e: (0, act_col(b, e)))
    ew_params = _params("arbitrary", "arbitrary")
    act = jax.ShapeDtypeStruct((seq, BATCH * HY_WIDTH), F32)
    act_b = jax.ShapeDtypeStruct((seq, BATCH * HY_WIDTH), BF16)
    alt_shape = jax.ShapeDtypeStruct((HY_PHASES, BATCH * HY_WIDTH), F32)

    def spectral(ub, order):
        return pl.pallas_call(
            functools.partial(_hy_spec_kernel, seq=seq, kf=kf),
            grid=(n_cols, n_ch),
            in_specs=[
                pl.BlockSpec((seq, BATCH * HY_COLS), lambda j, c: (0, j), pipeline_mode=pl.Buffered(1)),
                pl.BlockSpec((None, seq, 2 * HY_COLS), lambda j, c: (layer, 0, order * n_cols + j),
                             pipeline_mode=pl.Buffered(1)),
                pl.BlockSpec((None, 1, 2 * HY_COLS), lambda j, c: (layer, 0, order * n_cols + j)),
                pl.BlockSpec((None, 2 * kf, q), lambda j, c: (c, 0, 0)),
                pl.BlockSpec((None, q, 2 * kf), lambda j, c: (c, 0, 0)),
                pl.BlockSpec((tw.shape[0], kf, HEAD_DIM), lambda j, c: (0, c, 0)),
            ],
            out_specs=pl.BlockSpec((seq, BATCH * HY_COLS), lambda j, c: (0, j)),
            out_shape=act,
            compiler_params=_params("arbitrary", "arbitrary"),
            name=f"hy_spec{order}_l{seq}",
        )(ub, hb, asum, fmat, gmat, tw)

    ub, ualt = pl.pallas_call(
        functools.partial(_hy_pre_kernel, seq=seq),
        grid=(BATCH, n_ew),
        in_specs=[zcols(0), cw(0), cb(0)],
        out_specs=[seq_blk, alt_spec],
        out_shape=[act_b, alt_shape],
        compiler_params=ew_params,
        name=f"hy_pre_l{seq}",
    )(z, conv_w, conv_b)
    y0 = spectral(ub, 0)
    s, sb, salt = pl.pallas_call(
        functools.partial(_hy_mid_kernel, seq=seq),
        grid=(BATCH, n_ew),
        in_specs=[zcols(0), zcols(1), seq_blk, alt_spec, fstat(0, 0, HY_PHASES), fstat(0, 1, HY_PHASES), fstat(0, 0, 1),
                  fstat(0, 1, 1), cw(0), cb(0), cw(1), cb(1), skip_spec(0)],
        out_specs=[seq_blk, seq_blk, alt_spec],
        out_shape=[act, act_b, alt_shape],
        compiler_params=ew_params,
        name=f"hy_mid_l{seq}",
    )(z, z, y0, ualt, halt, halt, asum, asum, conv_w, conv_b, conv_w, conv_b, skip)
    y1 = spectral(sb, 1)
    in_specs = [zcols(2), seq_blk, seq_blk, alt_spec, fstat(1, 0, HY_PHASES), fstat(1, 1, HY_PHASES), fstat(1, 0, 1),
                fstat(1, 1, 1), cw(2), cb(2), skip_spec(1)]
    args = [z, y1, s, salt, halt, halt, asum, asum, conv_w, conv_b, skip]
    aliases = {}
    if alias_out is not None:
        in_specs.append(pl.BlockSpec(memory_space=pl.ANY))
        args.append(alias_out)
        aliases = {len(args) - 1: 0}
    return pl.pallas_call(
        functools.partial(_hy_post_kernel, seq=seq),
        grid=(BATCH, n_ew),
        in_specs=in_specs,
        out_specs=pl.BlockSpec((seq, ew), lambda b, e: (out_rb0 + b, e)),
        out_shape=jax.ShapeDtypeStruct((out_rows, HY_WIDTH), F32),
        input_output_aliases=aliases,
        compiler_params=ew_params,
        name=f"hy_post_l{seq}",
    )(*args)


def _outproj_kernel(a_ref, b_ref, c_ref, w_ref, x_ref, mod_ref, g_ref, o_ref):
    y = jnp.dot(a_ref[...], w_ref[:SGU_WIDTH, :], preferred_element_type=F32)
    y += jnp.dot(b_ref[...], w_ref[SGU_WIDTH:SGU_WIDTH + ATTN_WIDTH, :], preferred_element_type=F32)
    y += jnp.dot(c_ref[...].astype(BF16), w_ref[SGU_WIDTH + ATTN_WIDTH:, :], preferred_element_type=F32)
    o_ref[...] = x_ref[...] + mod_ref[5:6, :] * _rms(y, g_ref[3:4, :])


def _outproj(a, b, c, w_out, x, mods, norm_g, *, layer, n_tiles):
    return pl.pallas_call(
        _outproj_kernel,
        grid=(n_tiles,),
        in_specs=[
            pl.BlockSpec((TM, SGU_WIDTH), lambda i: (i, 0)),
            pl.BlockSpec((TM, ATTN_WIDTH), lambda i: (i, 0)),
            pl.BlockSpec((TM, HY_WIDTH), lambda i: (i, 0)),
            pl.BlockSpec((D_MIX, D_MODEL), lambda i: (0, 0), pipeline_mode=pl.Buffered(1)),
            pl.BlockSpec((TM, D_MODEL), lambda i: (i, 0)),
            pl.BlockSpec((None, None, N_MOD, D_MODEL), lambda i: (layer, _mod_row(i), 0, 0)),
            pl.BlockSpec((None, 6, D_MODEL), lambda i: (layer, 0, 0)),
        ],
        out_specs=pl.BlockSpec((TM, D_MODEL), lambda i: (i, 0)),
        out_shape=jax.ShapeDtypeStruct((n_tiles * TM, D_MODEL), F32),
        compiler_params=_params("arbitrary"),
        name="outproj",
    )(a, b, c, w_out, x, mods, norm_g)


def _rope_tables():
    rows = SEQ // GRID_W
    row = jnp.broadcast_to(jnp.arange(rows)[:, None], (rows, GRID_W)).reshape(SEQ).astype(F32)
    col = jnp.broadcast_to(jnp.arange(GRID_W)[None, :], (rows, GRID_W)).reshape(SEQ).astype(F32)
    half = HEAD_DIM // 2
    inv = ROPE_BASE ** (-jnp.arange(0, half, 2, dtype=F32) / half)
    ang_r = row[:, None] * inv
    ang_c = col[:, None] * inv
    ang = jnp.concatenate([ang_r, ang_r, ang_c, ang_c], axis=-1)
    cos, sin = jnp.cos(ang), jnp.sin(ang)
    first = (jnp.arange(HEAD_DIM) % half) < (half // 2)
    sa = jnp.where(first, -sin, 0.0)
    sb = jnp.where(first, 0.0, sin)

    def stack(lat, ctx_value):
        return jnp.concatenate([lat] * BATCH + [jnp.full((N_CTX, HEAD_DIM), ctx_value, F32)], axis=0)

    return stack(cos, 1.0), stack(sa, 0.0), stack(sb, 0.0)


def kernel(x, c, ctx, c_ctx, w_ada, b_ada, norm_g, ffn1_gate, ffn1_up, ffn1_down, w_in,
           sgu_g, sgu_w, sgu_b, attn_sink, hy_conv_w, hy_conv_b, hy_f_w1, hy_f_b1, hy_f_w2,
           hy_f_b2, hy_f_w3, hy_freq, hy_skip, w_out, ffn2_gate, ffn2_up, ffn2_down):
    xs = jnp.concatenate([x.reshape(N_LAT, D_MODEL), ctx.reshape(N_CTX, D_MODEL)], axis=0)
    cc = jnp.concatenate([c, c_ctx[None], jnp.zeros((MOD_ROWS - BATCH - 1, D_MODEL), F32)], axis=0)
    mods = _ada_mods(cc, w_ada, b_ada)

    w1 = [w[0].astype(BF16) for w in (ffn1_gate, ffn1_up, ffn1_down)]
    cos, sa, sb = _rope_tables()
    conv_b = hy_conv_b.reshape(DEPTH, 1, (HY_ORDER + 1) * HY_WIDTH)
    skip = hy_skip.reshape(DEPTH * HY_ORDER, 1, HY_WIDTH)
    filt_w = (hy_f_w1, hy_f_b1, hy_f_w2, hy_f_b2, hy_f_w3, hy_freq)
    filt_lat = _hyena_filters(SEQ, *filt_w)
    filt_ctx = _hyena_filters(CTX_LEN, *filt_w)
    f_lat, g_lat = _dft_matrices(SEQ // HY_PHASES, min(HY_KF, SEQ // HY_PHASES))
    f_ctx, g_ctx = _dft_matrices(CTX_LEN // HY_PHASES, min(HY_KF, CTX_LEN // HY_PHASES))
    tw_lat, tw_ctx = _twiddles(SEQ), _twiddles(CTX_LEN)

    all_tiles = N_TOK // TM
    lat_tiles = N_LAT // TM
    ctx_rb = N_LAT // CTX_LEN

    for i in range(DEPTH):
        last = i == DEPTH - 1
        n_tiles = lat_tiles if last else all_tiles
        n_rows = n_tiles * TM

        jobs = [(w, i) for w in (ffn2_gate, ffn2_up, ffn2_down, w_in, w_out)]
        xs, (w2g, w2u, w2d, w_in_b, w_out_b) = _ffn(xs, mods, norm_g, *w1, jobs, layer=i, k=0, n_tiles=all_tiles)
        z = _inproj(xs, mods, norm_g, w_in_b, cos, sa, sb, layer=i)

        a_out = _sgu(z, sgu_g, sgu_w, sgu_b, layer=i, n_rows=n_rows)
        b_out = _attention(z, attn_sink, layer=i, n_rows=n_rows)
        c_out = _hyena(z, 0, hy_conv_w, conv_b, skip, filt_lat, f_lat, g_lat, tw_lat, layer=i, seq=SEQ,
                       out_rows=n_rows, out_rb0=0)
        if not last:
            c_out = _hyena(z, ctx_rb, hy_conv_w, conv_b, skip, filt_ctx, f_ctx, g_ctx, tw_ctx, layer=i,
                           seq=CTX_LEN, out_rows=n_rows, out_rb0=ctx_rb, alias_out=c_out)

        xs = _outproj(a_out, b_out, c_out, w_out_b, xs, mods, norm_g, layer=i, n_tiles=n_tiles)
        jobs = [] if last else [(w, i + 1) for w in (ffn1_gate, ffn1_up, ffn1_down)]
        xs, w1 = _ffn(xs, mods, norm_g, w2g, w2u, w2d, jobs, layer=i, k=2, n_tiles=n_tiles)

    return xs[:N_LAT].reshape(BATCH, SEQ, D_MODEL)
```

```python
import functools
import math

import jax
import jax.numpy as jnp
from jax import lax
from jax.experimental import pallas as pl
from jax.experimental.pallas import tpu as pltpu

D_MODEL = 2048
BATCH = 2
SEQ = 4096
DEPTH = 4
GRID_W = 64
CTX_LEN = 256
HEAD_DIM = 128
NORM_EPS = 1e-6
N_MOD = 9
D_FF = 5632
SGU_GROUPS = 4
SGU_WIDTH = SGU_GROUPS * HEAD_DIM
CHUNK = 128
N_Q_HEADS = 8
N_KV_HEADS = 2
Q_GROUP = N_Q_HEADS // N_KV_HEADS
ATTN_WIDTH = N_Q_HEADS * HEAD_DIM
KV_WIDTH = N_KV_HEADS * HEAD_DIM
WINDOW = 128
BLOCK = 128
ROPE_BASE = 10000.0
NEG_INF = -1e30
HY_GROUPS = 4
HY_WIDTH = HY_GROUPS * HEAD_DIM
HY_ORDER = 2
HY_SHORT = 3
HY_BANDS = 16
HY_EMB = 1 + 2 * HY_BANDS
HY_HIDDEN = 64
HY_TARGET = 1e-2
HY_FAST_DECAY = 0.3
HY_SLOW_DECAY = 1.5
D_MIX = SGU_WIDTH + ATTN_WIDTH + HY_WIDTH
A_END = 2 * SGU_WIDTH
Q_END = A_END + ATTN_WIDTH
K_END = Q_END + KV_WIDTH
V_END = K_END + KV_WIDTH
IN_COLS = V_END + (HY_ORDER + 1) * HY_WIDTH

N_LAT = BATCH * SEQ
N_CTX = BATCH * CTX_LEN
N_TOK = N_LAT + N_CTX
MOD_ROWS = 8

TM = 512
TF = 512
FFN_ROWS = 32
BF16_SUBLANES = 16
TN_IN = 1024
TN_ADA = 1024
HY_COLS = 256
HY_FILT_COLS = 2 * HY_COLS
HY_KF = 256
HY_LEVELS = 3
HY_PHASES = 2 ** HY_LEVELS
HY_ROWS = 512
HY_INV_ROWS = 1024
HY_EMB_PAD = 128
VMEM_LIMIT = 56 * 1024 * 1024

F32 = jnp.float32
BF16 = jnp.bfloat16


def _params(*sem):
    return pltpu.CompilerParams(dimension_semantics=sem, vmem_limit_bytes=VMEM_LIMIT)


def _mod_row(i):
    return jnp.minimum(i // (SEQ // TM), BATCH)


def _silu(a):
    return a / (1.0 + jnp.exp(-a))


def _rms(x, g):
    return x * lax.rsqrt(jnp.mean(x * x, axis=-1, keepdims=True) + NORM_EPS) * g


def _ada_kernel(c_ref, w_ref, b_ref, o_ref):
    s = _silu(c_ref[...]).astype(BF16)
    o_ref[...] = jnp.dot(s, w_ref[...].astype(BF16), preferred_element_type=F32) + b_ref[...]


def _ada_mods(cc, w_ada, b_ada):
    n_out = N_MOD * D_MODEL
    out = pl.pallas_call(
        _ada_kernel,
        grid=(DEPTH, n_out // TN_ADA),
        in_specs=[
            pl.BlockSpec((MOD_ROWS, D_MODEL), lambda l, j: (0, 0)),
            pl.BlockSpec((None, D_MODEL, TN_ADA), lambda l, j: (l, 0, j)),
            pl.BlockSpec((None, 1, TN_ADA), lambda l, j: (l, 0, j)),
        ],
        out_specs=pl.BlockSpec((None, MOD_ROWS, TN_ADA), lambda l, j: (l, 0, j)),
        out_shape=jax.ShapeDtypeStruct((DEPTH, MOD_ROWS, n_out), F32),
        compiler_params=_params("arbitrary", "arbitrary"),
        name="ada_mods",
    )(cc, w_ada, b_ada.reshape(DEPTH, 1, n_out))
    return out.reshape(DEPTH, MOD_ROWS, N_MOD, D_MODEL)


def _ffn_kernel(*refs, k, nf, n_jobs):
    x_ref, mod_ref, g_ref, wg_ref, wu_ref, wd_ref = refs[:6]
    cast_src = refs[6:6 + n_jobs]
    o_ref = refs[6 + n_jobs]
    cast_dst = refs[7 + n_jobs:7 + 2 * n_jobs]
    h_ref, acc_ref = refs[7 + 2 * n_jobs:]
    f = pl.program_id(1)

    for src, dst in zip(cast_src, cast_dst):
        dst[...] = src[...].astype(BF16)

    @pl.when(f == 0)
    def _():
        g = g_ref[2 * k:2 * k + 1, :]
        m1 = 1.0 + mod_ref[3 * k:3 * k + 1, :]
        m2 = mod_ref[3 * k + 1:3 * k + 2, :]
        for r0 in range(0, TM, FFN_ROWS):
            rs = slice(r0, r0 + FFN_ROWS)
            h_ref[rs, :] = (_rms(x_ref[rs, :], g) * m1 + m2).astype(BF16)
        acc_ref[...] = jnp.zeros_like(acc_ref)

    h = h_ref[...]
    a = jnp.dot(h, wg_ref[...], preferred_element_type=F32)
    b = jnp.dot(h, wu_ref[...], preferred_element_type=F32)
    p = (_silu(a) * b).astype(BF16)
    acc_ref[...] += jnp.dot(p, wd_ref[...], preferred_element_type=F32)

    @pl.when(f == nf - 1)
    def _():
        g = g_ref[2 * k + 1:2 * k + 2, :]
        gate = 0.5 * mod_ref[3 * k + 2:3 * k + 3, :]
        for r0 in range(0, TM, FFN_ROWS):
            rs = slice(r0, r0 + FFN_ROWS)
            o_ref[rs, :] = x_ref[rs, :] + gate * _rms(acc_ref[rs, :], g)


def _ffn(x, mods, norm_g, wg, wu, wd, cast_jobs, *, layer, k, n_tiles):
    nf = D_FF // TF
    n_steps = n_tiles * nf
    in_specs = [
        pl.BlockSpec((TM, D_MODEL), lambda i, f: (i, 0)),
        pl.BlockSpec((None, None, N_MOD, D_MODEL), lambda i, f: (layer, _mod_row(i), 0, 0)),
        pl.BlockSpec((None, 6, D_MODEL), lambda i, f: (layer, 0, 0)),
        pl.BlockSpec((D_MODEL, TF), lambda i, f: (0, f)),
        pl.BlockSpec((D_MODEL, TF), lambda i, f: (0, f)),
        pl.BlockSpec((TF, D_MODEL), lambda i, f: (f, 0)),
    ]
    out_specs = [pl.BlockSpec((TM, D_MODEL), lambda i, f: (i, 0))]
    out_shape = [jax.ShapeDtypeStruct((n_tiles * TM, D_MODEL), F32)]
    for w, src_layer in cast_jobs:
        n_rows, n_cols = w.shape[1:]
        rb = BF16_SUBLANES * pl.cdiv(pl.cdiv(n_rows, n_steps), BF16_SUBLANES)
        assert n_rows % rb == 0
        last_blk = n_rows // rb - 1

        def step_block(i, f, last_blk=last_blk):
            return jnp.minimum(i * nf + f, last_blk)

        in_specs.append(pl.BlockSpec((None, rb, n_cols),
                                     lambda i, f, sl=src_layer, sb=step_block: (sl, sb(i, f), 0)))
        out_specs.append(pl.BlockSpec((rb, n_cols), lambda i, f, sb=step_block: (sb(i, f), 0)))
        out_shape.append(jax.ShapeDtypeStruct((n_rows, n_cols), BF16))
    outs = pl.pallas_call(
        functools.partial(_ffn_kernel, k=k, nf=nf, n_jobs=len(cast_jobs)),
        grid=(n_tiles, nf),
        in_specs=in_specs,
        out_specs=out_specs,
        out_shape=out_shape,
        scratch_shapes=[pltpu.VMEM((TM, D_MODEL), BF16), pltpu.VMEM((TM, D_MODEL), F32)],
        compiler_params=_params("arbitrary", "arbitrary"),
        name=f"ffn{k}",
    )(x, mods, norm_g, wg, wu, wd, *[w for w, _ in cast_jobs])
    return outs[0], list(outs[1:])


def _rope(t, cos, sa, sb):
    return t * cos + pltpu.roll(t, 3 * HEAD_DIM // 4, 1) * sa + pltpu.roll(t, HEAD_DIM // 4, 1) * sb


def _inproj_kernel(x_ref, mod_ref, g_ref, w_ref, cos_ref, sa_ref, sb_ref, o_ref):
    y = _rms(x_ref[...], g_ref[2:3, :])
    h = (y * (1.0 + mod_ref[3:4, :]) + mod_ref[4:5, :]).astype(BF16)
    cos, sa, sb = cos_ref[...], sa_ref[...], sb_ref[...]
    n_rope = (K_END - A_END) // HEAD_DIM
    for n in range(IN_COLS // TN_IN):
        c0 = n * TN_IN
        z = jnp.dot(h, w_ref[:, c0:c0 + TN_IN], preferred_element_type=F32)
        for hd in range(TN_IN // HEAD_DIM):
            col = c0 + hd * HEAD_DIM
            zs = z[:, hd * HEAD_DIM:(hd + 1) * HEAD_DIM]
            if A_END <= col < A_END + n_rope * HEAD_DIM:
                zs = _rope(zs, cos, sa, sb)
            o_ref[:, col:col + HEAD_DIM] = zs


def _inproj(x, mods, norm_g, w_in, cos, sa, sb, *, layer):
    n_tiles = N_TOK // TM
    tab = pl.BlockSpec((TM, HEAD_DIM), lambda i: (i, 0))
    return pl.pallas_call(
        _inproj_kernel,
        grid=(n_tiles,),
        in_specs=[
            pl.BlockSpec((TM, D_MODEL), lambda i: (i, 0)),
            pl.BlockSpec((None, None, N_MOD, D_MODEL), lambda i: (layer, _mod_row(i), 0, 0)),
            pl.BlockSpec((None, 6, D_MODEL), lambda i: (layer, 0, 0)),
            pl.BlockSpec((D_MODEL, IN_COLS), lambda i: (0, 0), pipeline_mode=pl.Buffered(1)),
            tab, tab, tab,
        ],
        out_specs=pl.BlockSpec((TM, IN_COLS), lambda i: (i, 0)),
        out_shape=jax.ShapeDtypeStruct((N_TOK, IN_COLS), F32),
        compiler_params=_params("arbitrary"),
        name="inproj",
    )(x, mods, norm_g, w_in, cos, sa, sb)


def _sgu_kernel(z_ref, g_ref, w_ref, b_ref, o_ref, *, rows):
    inv_sqrt2 = 1.0 / math.sqrt(2.0)
    for c in range(rows // CHUNK):
        rs = slice(c * CHUNK, (c + 1) * CHUNK)
        a = z_ref[rs, :]
        a = a * (lax.erf(a * inv_sqrt2) + 1.0) * 0.5
        u, v = a[:, :SGU_WIDTH], a[:, SGU_WIDTH:]
        mu = jnp.mean(v, axis=-1, keepdims=True)
        vc = v - mu
        var = jnp.mean(vc * vc, axis=-1, keepdims=True)
        v = (vc * lax.rsqrt(var + NORM_EPS) * g_ref[...]).astype(BF16)
        for g in range(SGU_GROUPS):
            cs = slice(g * HEAD_DIM, (g + 1) * HEAD_DIM)
            mixed = jnp.dot(w_ref[g].astype(BF16), v[:, cs], preferred_element_type=F32) + b_ref[:, g:g + 1]
            o_ref[rs, cs] = (u[:, cs] * mixed).astype(o_ref.dtype)


def _sgu(z, sgu_g, sgu_w, sgu_b, *, layer, n_rows):
    rows = TM
    return pl.pallas_call(
        functools.partial(_sgu_kernel, rows=rows),
        grid=(n_rows // rows,),
        in_specs=[
            pl.BlockSpec((rows, A_END), lambda i: (i, 0)),
            pl.BlockSpec((None, 1, SGU_WIDTH), lambda i: (layer, 0, 0)),
            pl.BlockSpec((None, SGU_GROUPS, CHUNK, CHUNK), lambda i: (layer, 0, 0, 0)),
            pl.BlockSpec((None, CHUNK, SGU_GROUPS), lambda i: (layer, 0, 0)),
        ],
        out_specs=pl.BlockSpec((rows, SGU_WIDTH), lambda i: (i, 0)),
        out_shape=jax.ShapeDtypeStruct((n_rows, SGU_WIDTH), BF16),
        compiler_params=_params("arbitrary"),
        name="sgu",
    )(z, sgu_g.reshape(DEPTH, 1, SGU_WIDTH), sgu_w, jnp.swapaxes(sgu_b, 1, 2))


def _attn_kernel(sink_ref, q_ref, kp_ref, kc_ref, kn_ref, vp_ref, vc_ref, vn_ref, kx_ref, vx_ref,
                 o_ref, *, layer, nb, n_lat_blocks):
    s_id = pl.program_id(0)
    is_lat = s_id < n_lat_blocks
    n = s_id % nb
    n_keys = CTX_LEN + 3 * BLOCK
    n_rows = Q_GROUP * BLOCK
    row_id = lax.broadcasted_iota(jnp.int32, (n_rows, 1), 0)
    row = lax.broadcasted_iota(jnp.int32, (n_rows, n_keys), 0) & (BLOCK - 1)
    col = lax.broadcasted_iota(jnp.int32, (n_rows, n_keys), 1)
    rel = col - (CTX_LEN + BLOCK)
    lo = jnp.where(is_lat, -n * BLOCK, 4 * BLOCK)
    hi = jnp.where(is_lat, (nb - n) * BLOCK, -4 * BLOCK)
    dist = rel - row
    valid = (col < CTX_LEN) | ((dist >= -WINDOW) & (dist <= WINDOW) & (rel >= lo) & (rel < hi))
    scale = HEAD_DIM ** -0.5
    nt = (((1,), (1,)), ((), ()))
    for kh in range(N_KV_HEADS):
        ks = slice(kh * HEAD_DIM, (kh + 1) * HEAD_DIM)
        keys = jnp.concatenate([kx_ref[:, ks], kp_ref[:, ks], kc_ref[:, ks], kn_ref[:, ks]], axis=0).astype(BF16)
        vals = jnp.concatenate([vx_ref[:, ks], vp_ref[:, ks], vc_ref[:, ks], vn_ref[:, ks]], axis=0).astype(BF16)
        heads = [kh * Q_GROUP + g for g in range(Q_GROUP)]
        q = jnp.concatenate([q_ref[:, hd * HEAD_DIM:(hd + 1) * HEAD_DIM] for hd in heads], axis=0).astype(BF16)
        sink = jnp.full((n_rows, 1), sink_ref[layer, heads[0]], F32)
        for g, hd in enumerate(heads[1:], start=1):
            sink = jnp.where(row_id >= g * BLOCK, sink_ref[layer, hd], sink)
        s = lax.dot_general(q, keys, nt, preferred_element_type=F32) * scale
        s = jnp.where(valid, s, NEG_INF)
        m = jnp.maximum(jnp.max(s, axis=-1, keepdims=True), sink)
        p = jnp.exp(s - m)
        denom = jnp.sum(p, axis=-1, keepdims=True) + jnp.exp(sink - m)
        probs = (p / denom).astype(BF16)
        out = jnp.dot(probs, vals, preferred_element_type=F32).astype(o_ref.dtype)
        for g, hd in enumerate(heads):
            o_ref[:, hd * HEAD_DIM:(hd + 1) * HEAD_DIM] = out[g * BLOCK:(g + 1) * BLOCK]


def _attention(z, attn_sink, *, layer, n_rows):
    nb = SEQ // BLOCK
    n_lat_blocks = N_LAT // BLOCK
    n_blocks = n_rows // BLOCK
    ctx_blocks = CTX_LEN // BLOCK
    q_col = A_END // ATTN_WIDTH
    k_col = Q_END // KV_WIDTH
    v_col = K_END // KV_WIDTH

    def lat(s):
        return s < n_lat_blocks

    def band(off, col):
        def index(s):
            b = s // nb
            nbr = jnp.clip(s % nb + off, 0, nb - 1) + b * nb
            return (jnp.where(lat(s), nbr, s), col)
        return pl.BlockSpec((BLOCK, KV_WIDTH), index)

    def ctx(col):
        def index(s):
            b = jnp.where(lat(s), s // nb, (s - n_lat_blocks) // ctx_blocks)
            return (N_LAT // CTX_LEN + b, col)
        return pl.BlockSpec((CTX_LEN, KV_WIDTH), index)

    return pl.pallas_call(
        functools.partial(_attn_kernel, layer=layer, nb=nb, n_lat_blocks=n_lat_blocks),
        grid=(n_blocks,),
        in_specs=[
            pl.BlockSpec(memory_space=pltpu.SMEM),
            pl.BlockSpec((BLOCK, ATTN_WIDTH), lambda s: (s, q_col)),
            band(-1, k_col), band(0, k_col), band(1, k_col),
            band(-1, v_col), band(0, v_col), band(1, v_col),
            ctx(k_col), ctx(v_col),
        ],
        out_specs=pl.BlockSpec((BLOCK, ATTN_WIDTH), lambda s: (s, 0)),
        out_shape=jax.ShapeDtypeStruct((n_rows, ATTN_WIDTH), BF16),
        compiler_params=_params("arbitrary"),
        name="attn",
    )(attn_sink, z, z, z, z, z, z, z, z, z)


def _short_conv_phases(ref, w_ref, b_ref, m0, rows, n_m):
    ph = HY_PHASES
    z = [ref[pl.ds(ph * m0 + r, rows, stride=ph), :] for r in range(ph)]
    m = lax.broadcasted_iota(jnp.int32, (rows, 1), 0) + m0
    if m0 == 0:
        z_prev = jnp.where(m == 0, 0.0, pltpu.roll(z[ph - 1], 1, 0))
    else:
        z_prev = ref[pl.ds(ph * m0 - 1, rows, stride=ph), :]
    if m0 + rows == n_m:
        z_next = jnp.where(m == n_m - 1, 0.0, pltpu.roll(z[0], rows - 1, 0))
    else:
        z_next = ref[pl.ds(ph * m0 + ph, rows, stride=ph), :]
    ext = [z_prev] + z + [z_next]
    w0, w1, w2, b = w_ref[0:1, :], w_ref[1:2, :], w_ref[2:3, :], b_ref[...]
    return [ext[r] * w0 + ext[r + 1] * w1 + ext[r + 2] * w2 + b for r in range(ph)]


def _cmul(a, b):
    return a[0] * b[0] - a[1] * b[1], a[0] * b[1] + a[1] * b[0]


def _cadd(a, b):
    return a[0] + b[0], a[1] + b[1]


def _csub(a, b):
    return a[0] - b[0], a[1] - b[1]


def _conj(a):
    return a[0], -a[1]


def _split_fwd(e, o, w):
    t = _cmul(w, o)
    return _cadd(e, t), _conj(_csub(e, t))


def _split_inv(pp, pm, w):
    cm = _conj(pm)
    return _cadd(pp, cm), _cmul(_csub(pp, cm), _conj(w))


def _alt_sign(r0, rows):
    t = lax.broadcasted_iota(jnp.int32, (rows, 1), 0) + r0
    return jnp.where((t & 1) == 0, 1.0, -1.0)


def _hy_chunks(seq, chunk=None):
    rows = min(chunk or HY_ROWS, seq)
    return [(r * rows, rows) for r in range(seq // rows)]


def _filt_kernel(feat_ref, w1_ref, b1_ref, w2_ref, b2_ref, w3_ref, fr_ref, dl_ref,
                 hb_ref, asum_ref, hnyq_ref, hd_ref, *, seq):
    hp = lax.Precision.HIGHEST

    @pl.when(pl.program_id(1) == 0)
    def _():
        fr = fr_ref[...]
        hd = jnp.sin(fr * (jnp.dot(feat_ref[...], w1_ref[...], precision=hp, preferred_element_type=F32)
                           + b1_ref[...]))
        hd_ref[...] = jnp.sin(fr * (jnp.dot(hd, w2_ref[...], precision=hp, preferred_element_type=F32)
                                    + b2_ref[...]))

    q = seq // HY_PHASES
    asum = jnp.zeros((1, HY_FILT_COLS), F32)
    alt = [jnp.zeros((1, HY_FILT_COLS), F32) for _ in range(HY_PHASES)]
    for r0, rows in _hy_chunks(seq, min(HY_ROWS, q)):
        rs = slice(r0, r0 + rows)
        h = jnp.dot(hd_ref[rs, :].astype(BF16), w3_ref[...].astype(BF16), preferred_element_type=F32)
        h = h * jnp.exp(-feat_ref[rs, 0:1] * dl_ref[...])
        asum = asum + jnp.sum(jnp.abs(h), axis=0, keepdims=True)
        alt[r0 // q] = alt[r0 // q] + jnp.sum(h * _alt_sign(r0 % q, rows), axis=0, keepdims=True)
        hb_ref[rs, :] = h.astype(BF16)
    asum_ref[...] = asum
    for r in range(HY_PHASES):
        hnyq_ref[r:r + 1, :] = alt[r]


def _hyena_filters(seq, w1, b1, w2, b2, w3, freq):
    n_tiles = HY_WIDTH // HY_COLS
    w3 = w3.reshape(DEPTH, HY_HIDDEN, HY_ORDER, 2, n_tiles, HY_COLS)
    w3 = jnp.swapaxes(w3, 3, 4).reshape(DEPTH, HY_HIDDEN, HY_ORDER * 2 * HY_WIDTH)
    t = jnp.linspace(0.0, 1.0, seq, dtype=F32)[:, None]
    w = (2.0 * math.pi / seq) * jnp.arange(seq, dtype=F32)[:, None]
    bands = jnp.linspace(1e-4, HY_BANDS - 1, HY_BANDS, dtype=F32)[None, :]
    feat = jnp.concatenate([t, jnp.cos(bands * w), -jnp.sin(bands * w)], axis=-1)
    feat = jnp.pad(feat, ((0, 0), (0, HY_EMB_PAD - HY_EMB)))
    feat = jnp.concatenate([feat[r::HY_PHASES] for r in range(HY_PHASES)], axis=0)
    w1p = jnp.pad(w1, ((0, 0), (0, HY_EMB_PAD - HY_EMB), (0, 0)))
    deltas = jnp.abs(jnp.linspace(math.log(HY_TARGET) / HY_FAST_DECAY,
                                  math.log(HY_TARGET) / HY_SLOW_DECAY, HY_WIDTH, dtype=F32))
    deltas = jnp.broadcast_to(deltas.reshape(1, n_tiles, 1, HY_COLS), (HY_ORDER, n_tiles, 2, HY_COLS))
    deltas = deltas.reshape(1, HY_ORDER * 2 * HY_WIDTH)
    n_cols = HY_ORDER * 2 * HY_WIDTH

    def row(a):
        return a.reshape(DEPTH, 1, HY_HIDDEN)

    vec = pl.BlockSpec((None, 1, HY_HIDDEN), lambda l, j: (l, 0, 0))
    stat = pl.BlockSpec((None, 1, HY_FILT_COLS), lambda l, j: (l, 0, j))
    return pl.pallas_call(
        functools.partial(_filt_kernel, seq=seq),
        grid=(DEPTH, n_cols // HY_FILT_COLS),
        in_specs=[
            pl.BlockSpec((seq, HY_EMB_PAD), lambda l, j: (0, 0)),
            pl.BlockSpec((None, HY_EMB_PAD, HY_HIDDEN), lambda l, j: (l, 0, 0)),
            vec,
            pl.BlockSpec((None, HY_HIDDEN, HY_HIDDEN), lambda l, j: (l, 0, 0)),
            vec,
            pl.BlockSpec((None, HY_HIDDEN, HY_FILT_COLS), lambda l, j: (l, 0, j)),
            vec,
            pl.BlockSpec((1, HY_FILT_COLS), lambda l, j: (0, j)),
        ],
        out_specs=[pl.BlockSpec((None, seq, HY_FILT_COLS), lambda l, j: (l, 0, j)), stat,
                   pl.BlockSpec((None, HY_PHASES, HY_FILT_COLS), lambda l, j: (l, 0, j))],
        out_shape=[jax.ShapeDtypeStruct((DEPTH, seq, n_cols), BF16),
                   jax.ShapeDtypeStruct((DEPTH, 1, n_cols), F32),
                   jax.ShapeDtypeStruct((DEPTH, HY_PHASES, n_cols), F32)],
        scratch_shapes=[pltpu.VMEM((seq, HY_HIDDEN), F32)],
        compiler_params=_params("arbitrary", "arbitrary"),
        name=f"hy_filt_l{seq}",
    )(feat, w1p, row(b1), w2, row(b2), w3, row(freq), deltas)


def _dft_matrices(seq, kf):
    n_fft = 2 * seq
    n_ch = seq // kf
    th = 2.0 * math.pi / n_fft
    c = jnp.arange(n_ch, dtype=jnp.int32)[:, None]
    kl = jnp.arange(kf, dtype=jnp.int32)[:, None]
    n = jnp.arange(seq, dtype=jnp.int32)[None, :]
    a1 = th * ((c * kf * n) % n_fft).astype(F32)
    a2 = th * ((kl * n) % n_fft).astype(F32)
    c1, s1, c2, s2 = jnp.cos(a1), jnp.sin(a1), jnp.cos(a2), jnp.sin(a2)
    f_cos = c1[:, None, :] * c2[None, :, :] - s1[:, None, :] * s2[None, :, :]
    f_sin = s1[:, None, :] * c2[None, :, :] + c1[:, None, :] * s2[None, :, :]
    fmat = jnp.concatenate([f_cos, -f_sin], axis=1).astype(BF16)
    c2t, s2t = c2.T, s2.T
    g_cos = c1[:, :, None] * c2t[None, :, :] - s1[:, :, None] * s2t[None, :, :]
    g_sin = s1[:, :, None] * c2t[None, :, :] + c1[:, :, None] * s2t[None, :, :]
    gmat = jnp.concatenate([g_cos, -g_sin], axis=2).astype(BF16)
    return fmat, gmat


def _twiddle_plan(seq):
    freqs = [(0, 1)]
    plan = [None] * HY_LEVELS
    for d in reversed(range(HY_LEVELS)):
        l_d = seq >> d
        plan[d] = [(off, sgn, 2 * l_d) for off, sgn in freqs]
        freqs = [f for off, sgn in freqs for f in ((off, sgn), (l_d - off, -sgn))]
    index = 0
    for d in range(HY_LEVELS):
        plan[d] = [(index + j,) + t for j, t in enumerate(plan[d])]
        index += len(plan[d])
    return plan


def _twiddles(seq):
    k = jnp.arange(seq // HY_PHASES, dtype=F32)[:, None]
    tabs = []
    for level in _twiddle_plan(seq):
        for _, off, sgn, mod in level:
            ang = jnp.broadcast_to((2.0 * math.pi / mod) * (off + sgn * k), (seq // HY_PHASES, HEAD_DIM))
            tabs += [jnp.cos(ang), jnp.sin(ang)]
    return jnp.stack(tabs, axis=0)


def _spectrum_tree(leaves, twiddle, depth=0):
    if len(leaves) == 1:
        return leaves
    even = _spectrum_tree(leaves[0::2], twiddle, depth + 1)
    odd = _spectrum_tree(leaves[1::2], twiddle, depth + 1)
    out = []
    for j, (e, o) in enumerate(zip(even, odd)):
        out += list(_split_fwd(e, o, twiddle(depth, j)))
    return out


def _synthesis_tree(vals, twiddle, depth=0):
    if len(vals) == 1:
        return vals
    even, odd = [], []
    for j in range(len(vals) // 2):
        q0, q1 = _split_inv(vals[2 * j], vals[2 * j + 1], twiddle(depth, j))
        even.append(q0)
        odd.append(q1)
    out = [None] * len(vals)
    out[0::2] = _synthesis_tree(even, twiddle, depth + 1)
    out[1::2] = _synthesis_tree(odd, twiddle, depth + 1)
    return out


def _hy_pre_kernel(z_ref, w_ref, b_ref, ub_ref, ualt_ref, *, seq):
    q = seq // HY_PHASES
    alt = [jnp.zeros((1, HEAD_DIM), F32) for _ in range(HY_PHASES)]
    for m0, rows in _hy_chunks(q):
        v = _short_conv_phases(z_ref, w_ref, b_ref, m0, rows, q)
        sign = _alt_sign(m0, rows)
        for r in range(HY_PHASES):
            ub_ref[r * q + m0:r * q + m0 + rows, :] = v[r].astype(BF16)
            alt[r] = alt[r] + jnp.sum(v[r] * sign, axis=0, keepdims=True)
    for r in range(HY_PHASES):
        ualt_ref[r:r + 1, :] = alt[r]


def _hy_spec_kernel(u_ref, h_ref, a_ref, f_ref, g_ref, tw_ref, y_ref, *, seq, kf):
    c = pl.program_id(1)
    cw = HY_COLS
    q = seq // HY_PHASES
    n_fft = 2 * seq

    @pl.when(c == 0)
    def _():
        y_ref[...] = jnp.zeros_like(y_ref)

    f = f_ref[...]
    plan = _twiddle_plan(seq)

    def twiddles(width):
        rep = width // HEAD_DIM
        cache = {}

        def get(depth, j):
            idx = plan[depth][j][0]
            if idx not in cache:
                cache[idx] = (jnp.concatenate([tw_ref[2 * idx]] * rep, axis=1),
                              -jnp.concatenate([tw_ref[2 * idx + 1]] * rep, axis=1))
            return cache[idx]
        return get

    def spectrum(ref, width):
        leaves = []
        for r in range(HY_PHASES):
            d = jnp.dot(f, ref[r * q:(r + 1) * q, :], preferred_element_type=F32)
            leaves.append((d[:kf], d[kf:]))
        return _spectrum_tree(leaves, twiddles(width))

    row = lax.broadcasted_iota(jnp.int32, (kf, 1), 0)
    a = a_ref[...]
    wgt = jnp.where((row == 0) & (c == 0), 1.0 / n_fft, 2.0 / n_fft) / (a[:, :cw] + a[:, cw:] + 1e-6)
    filt = [((x[0][:, :cw] + x[0][:, cw:]) * wgt, (x[1][:, :cw] - x[1][:, cw:]) * wgt)
            for x in spectrum(h_ref, 2 * cw)]
    sig = spectrum(u_ref, BATCH * cw)
    tw_c = twiddles(cw)
    out = [[] for _ in range(HY_PHASES)]
    for b in range(BATCH):
        bs = slice(b * cw, (b + 1) * cw)
        p = [_cmul((x[0][:, bs], x[1][:, bs]), s) for x, s in zip(sig, filt)]
        for r, val in enumerate(_synthesis_tree(p, tw_c)):
            out[r].append(val)
    for r in range(HY_PHASES):
        spec = jnp.concatenate([jnp.concatenate([v[0] for v in out[r]], axis=1),
                                jnp.concatenate([v[1] for v in out[r]], axis=1)], axis=0).astype(BF16)
        for r0, rows in _hy_chunks(q, HY_INV_ROWS):
            y_ref[r * q + r0:r * q + r0 + rows, :] += jnp.dot(g_ref[r0:r0 + rows, :], spec,
                                                              preferred_element_type=F32)


def _edge_terms(ualt_ref, hf_ref, hb_ref, af_ref, ab_ref, seq):
    k_edge = seq // HY_PHASES
    plan = _twiddle_plan(seq)

    def twiddle(depth, j):
        _, off, sgn, mod = plan[depth][j]
        ang = 2.0 * math.pi * (off + sgn * k_edge) / mod
        return math.cos(ang), -math.sin(ang)

    def spectrum(ref):
        return _spectrum_tree([(ref[r:r + 1, :], 0.0) for r in range(HY_PHASES)], twiddle)

    sig, fwd, bwd = spectrum(ualt_ref), spectrum(hf_ref), spectrum(hb_ref)
    p = [_cmul(u, _cadd(f, _conj(b))) for u, f, b in zip(sig, fwd, bwd)]
    scale = (1.0 / (2 * seq)) / (af_ref[...] + ab_ref[...] + 1e-6)
    return [leaf[0] * scale for leaf in _synthesis_tree(p, twiddle)]


def _hy_mid_kernel(zv_ref, zx_ref, y_ref, ualt_ref, hf_ref, hb_ref, af_ref, ab_ref, wv_ref, bv_ref,
                   wx_ref, bx_ref, skip_ref, s_ref, sb_ref, salt_ref, *, seq):
    q = seq // HY_PHASES
    amp = _edge_terms(ualt_ref, hf_ref, hb_ref, af_ref, ab_ref, seq)
    alt = [jnp.zeros((1, HEAD_DIM), F32) for _ in range(HY_PHASES)]
    skip = skip_ref[...]
    for m0, rows in _hy_chunks(q):
        sign = _alt_sign(m0, rows)
        v = _short_conv_phases(zv_ref, wv_ref, bv_ref, m0, rows, q)
        x1 = _short_conv_phases(zx_ref, wx_ref, bx_ref, m0, rows, q)
        for r in range(HY_PHASES):
            rs = slice(r * q + m0, r * q + m0 + rows)
            s = x1[r] * (y_ref[rs, :] + sign * amp[r] + v[r] * skip)
            s_ref[rs, :] = s
            sb_ref[rs, :] = s.astype(BF16)
            alt[r] = alt[r] + jnp.sum(s * sign, axis=0, keepdims=True)
    for r in range(HY_PHASES):
        salt_ref[r:r + 1, :] = alt[r]


def _hy_post_kernel(zx_ref, y_ref, s_ref, salt_ref, hf_ref, hb_ref, af_ref, ab_ref, wx_ref, bx_ref,
                    skip_ref, *rest, seq):
    o_ref = rest[-1]
    q = seq // HY_PHASES
    amp = _edge_terms(salt_ref, hf_ref, hb_ref, af_ref, ab_ref, seq)
    skip = skip_ref[...]
    for m0, rows in _hy_chunks(q):
        sign = _alt_sign(m0, rows)
        x2 = _short_conv_phases(zx_ref, wx_ref, bx_ref, m0, rows, q)
        for r in range(HY_PHASES):
            rs = slice(r * q + m0, r * q + m0 + rows)
            y = y_ref[rs, :] + sign * amp[r] + s_ref[rs, :] * skip
            o_ref[pl.ds(HY_PHASES * m0 + r, rows, stride=HY_PHASES), :] = x2[r] * y


def _hyena(z, z_rb0, conv_w, conv_b, skip, filt, fmat, gmat, tw, *, layer, seq, out_rows, out_rb0,
           alias_out=None):
    hb, asum, halt = filt
    q = seq // HY_PHASES
    kf = min(HY_KF, q)
    n_ch = q // kf
    n_cols = HY_WIDTH // HY_COLS
    ew = HEAD_DIM
    sub = HY_COLS // ew
    n_ew = HY_WIDTH // ew
    hy0 = V_END // ew

    def zcols(which):
        return pl.BlockSpec((seq, ew), lambda b, e: (z_rb0 + b, hy0 + which * n_ew + e))

    def cw(which):
        return pl.BlockSpec((None, HY_SHORT, ew), lambda b, e: (layer, 0, which * n_ew + e))

    def cb(which):
        return pl.BlockSpec((None, 1, ew), lambda b, e: (layer, 0, which * n_ew + e))

    def fstat(order, direction, rows):
        return pl.BlockSpec((None, rows, ew),
                            lambda b, e: (layer, 0, ((order * n_cols + e // sub) * 2 + direction) * sub + e % sub))

    def skip_spec(order):
        return pl.BlockSpec((None, 1, ew), lambda b, e: (layer * HY_ORDER + order, 0, e))

    def act_col(b, e):
        return ((e // sub) * BATCH + b) * sub + e % sub

    seq_blk = pl.BlockSpec((seq, ew), lambda b, e: (0, act_col(b, e)))
    alt_spec = pl.BlockSpec((HY_PHASES, ew), lambda b, e: (0, act_col(b, e)))
    ew_params = _params("arbitrary", "arbitrary")
    act = jax.ShapeDtypeStruct((seq, BATCH * HY_WIDTH), F32)
    act_b = jax.ShapeDtypeStruct((seq, BATCH * HY_WIDTH), BF16)
    alt_shape = jax.ShapeDtypeStruct((HY_PHASES, BATCH * HY_WIDTH), F32)

    def spectral(ub, order):
        return pl.pallas_call(
            functools.partial(_hy_spec_kernel, seq=seq, kf=kf),
            grid=(n_cols, n_ch),
            in_specs=[
                pl.BlockSpec((seq, BATCH * HY_COLS), lambda j, c: (0, j), pipeline_mode=pl.Buffered(1)),
                pl.BlockSpec((None, seq, 2 * HY_COLS), lambda j, c: (layer, 0, order * n_cols + j),
                             pipeline_mode=pl.Buffered(1)),
                pl.BlockSpec((None, 1, 2 * HY_COLS), lambda j, c: (layer, 0, order * n_cols + j)),
                pl.BlockSpec((None, 2 * kf, q), lambda j, c: (c, 0, 0)),
                pl.BlockSpec((None, q, 2 * kf), lambda j, c: (c, 0, 0)),
                pl.BlockSpec((tw.shape[0], kf, HEAD_DIM), lambda j, c: (0, c, 0)),
            ],
            out_specs=pl.BlockSpec((seq, BATCH * HY_COLS), lambda j, c: (0, j)),
            out_shape=act,
            compiler_params=_params("arbitrary", "arbitrary"),
            name=f"hy_spec{order}_l{seq}",
        )(ub, hb, asum, fmat, gmat, tw)

    ub, ualt = pl.pallas_call(
        functools.partial(_hy_pre_kernel, seq=seq),
        grid=(BATCH, n_ew),
        in_specs=[zcols(0), cw(0), cb(0)],
        out_specs=[seq_blk, alt_spec],
        out_shape=[act_b, alt_shape],
        compiler_params=ew_params,
        name=f"hy_pre_l{seq}",
    )(z, conv_w, conv_b)
    y0 = spectral(ub, 0)
    s, sb, salt = pl.pallas_call(
        functools.partial(_hy_mid_kernel, seq=seq),
        grid=(BATCH, n_ew),
        in_specs=[zcols(0), zcols(1), seq_blk, alt_spec, fstat(0, 0, HY_PHASES), fstat(0, 1, HY_PHASES), fstat(0, 0, 1),
                  fstat(0, 1, 1), cw(0), cb(0), cw(1), cb(1), skip_spec(0)],
        out_specs=[seq_blk, seq_blk, alt_spec],
        out_shape=[act, act_b, alt_shape],
        compiler_params=ew_params,
        name=f"hy_mid_l{seq}",
    )(z, z, y0, ualt, halt, halt, asum, asum, conv_w, conv_b, conv_w, conv_b, skip)
    y1 = spectral(sb, 1)
    in_specs = [zcols(2), seq_blk, seq_blk, alt_spec, fstat(1, 0, HY_PHASES), fstat(1, 1, HY_PHASES), fstat(1, 0, 1),
                fstat(1, 1, 1), cw(2), cb(2), skip_spec(1)]
    args = [z, y1, s, salt, halt, halt, asum, asum, conv_w, conv_b, skip]
    aliases = {}
    if alias_out is not None:
        in_specs.append(pl.BlockSpec(memory_space=pl.ANY))
        args.append(alias_out)
        aliases = {len(args) - 1: 0}
    return pl.pallas_call(
        functools.partial(_hy_post_kernel, seq=seq),
        grid=(BATCH, n_ew),
        in_specs=in_specs,
        out_specs=pl.BlockSpec((seq, ew), lambda b, e: (out_rb0 + b, e)),
        out_shape=jax.ShapeDtypeStruct((out_rows, HY_WIDTH), F32),
        input_output_aliases=aliases,
        compiler_params=ew_params,
        name=f"hy_post_l{seq}",
    )(*args)


def _outproj_kernel(a_ref, b_ref, c_ref, w_ref, x_ref, mod_ref, g_ref, o_ref):
    y = jnp.dot(a_ref[...], w_ref[:SGU_WIDTH, :], preferred_element_type=F32)
    y += jnp.dot(b_ref[...], w_ref[SGU_WIDTH:SGU_WIDTH + ATTN_WIDTH, :], preferred_element_type=F32)
    y += jnp.dot(c_ref[...].astype(BF16), w_ref[SGU_WIDTH + ATTN_WIDTH:, :], preferred_element_type=F32)
    o_ref[...] = x_ref[...] + mod_ref[5:6, :] * _rms(y, g_ref[3:4, :])


def _outproj(a, b, c, w_out, x, mods, norm_g, *, layer, n_tiles):
    return pl.pallas_call(
        _outproj_kernel,
        grid=(n_tiles,),
        in_specs=[
            pl.BlockSpec((TM, SGU_WIDTH), lambda i: (i, 0)),
            pl.BlockSpec((TM, ATTN_WIDTH), lambda i: (i, 0)),
            pl.BlockSpec((TM, HY_WIDTH), lambda i: (i, 0)),
            pl.BlockSpec((D_MIX, D_MODEL), lambda i: (0, 0), pipeline_mode=pl.Buffered(1)),
            pl.BlockSpec((TM, D_MODEL), lambda i: (i, 0)),
            pl.BlockSpec((None, None, N_MOD, D_MODEL), lambda i: (layer, _mod_row(i), 0, 0)),
            pl.BlockSpec((None, 6, D_MODEL), lambda i: (layer, 0, 0)),
        ],
        out_specs=pl.BlockSpec((TM, D_MODEL), lambda i: (i, 0)),
        out_shape=jax.ShapeDtypeStruct((n_tiles * TM, D_MODEL), F32),
        compiler_params=_params("arbitrary"),
        name="outproj",
    )(a, b, c, w_out, x, mods, norm_g)


def _rope_tables():
    rows = SEQ // GRID_W
    row = jnp.broadcast_to(jnp.arange(rows)[:, None], (rows, GRID_W)).reshape(SEQ).astype(F32)
    col = jnp.broadcast_to(jnp.arange(GRID_W)[None, :], (rows, GRID_W)).reshape(SEQ).astype(F32)
    half = HEAD_DIM // 2
    inv = ROPE_BASE ** (-jnp.arange(0, half, 2, dtype=F32) / half)
    ang_r = row[:, None] * inv
    ang_c = col[:, None] * inv
    ang = jnp.concatenate([ang_r, ang_r, ang_c, ang_c], axis=-1)
    cos, sin = jnp.cos(ang), jnp.sin(ang)
    first = (jnp.arange(HEAD_DIM) % half) < (half // 2)
    sa = jnp.where(first, -sin, 0.0)
    sb = jnp.where(first, 0.0, sin)

    def stack(lat, ctx_value):
        return jnp.concatenate([lat] * BATCH + [jnp.full((N_CTX, HEAD_DIM), ctx_value, F32)], axis=0)

    return stack(cos, 1.0), stack(sa, 0.0), stack(sb, 0.0)


def kernel(x, c, ctx, c_ctx, w_ada, b_ada, norm_g, ffn1_gate, ffn1_up, ffn1_down, w_in,
           sgu_g, sgu_w, sgu_b, attn_sink, hy_conv_w, hy_conv_b, hy_f_w1, hy_f_b1, hy_f_w2,
           hy_f_b2, hy_f_w3, hy_freq, hy_skip, w_out, ffn2_gate, ffn2_up, ffn2_down):
    xs = jnp.concatenate([x.reshape(N_LAT, D_MODEL), ctx.reshape(N_CTX, D_MODEL)], axis=0)
    cc = jnp.concatenate([c, c_ctx[None], jnp.zeros((MOD_ROWS - BATCH - 1, D_MODEL), F32)], axis=0)
    mods = _ada_mods(cc, w_ada, b_ada)

    w1 = [w[0].astype(BF16) for w in (ffn1_gate, ffn1_up, ffn1_down)]
    cos, sa, sb = _rope_tables()
    conv_b = hy_conv_b.reshape(DEPTH, 1, (HY_ORDER + 1) * HY_WIDTH)
    skip = hy_skip.reshape(DEPTH * HY_ORDER, 1, HY_WIDTH)
    filt_w = (hy_f_w1, hy_f_b1, hy_f_w2, hy_f_b2, hy_f_w3, hy_freq)
    filt_lat = _hyena_filters(SEQ, *filt_w)
    filt_ctx = _hyena_filters(CTX_LEN, *filt_w)
    f_lat, g_lat = _dft_matrices(SEQ // HY_PHASES, min(HY_KF, SEQ // HY_PHASES))
    f_ctx, g_ctx = _dft_matrices(CTX_LEN // HY_PHASES, min(HY_KF, CTX_LEN // HY_PHASES))
    tw_lat, tw_ctx = _twiddles(SEQ), _twiddles(CTX_LEN)

    all_tiles = N_TOK // TM
    lat_tiles = N_LAT // TM
    ctx_rb = N_LAT // CTX_LEN

    for i in range(DEPTH):
        last = i == DEPTH - 1
        n_tiles = lat_tiles if last else all_tiles
        n_rows = n_tiles * TM

        jobs = [(w, i) for w in (ffn2_gate, ffn2_up, ffn2_down, w_in, w_out)]
        xs, (w2g, w2u, w2d, w_in_b, w_out_b) = _ffn(xs, mods, norm_g, *w1, jobs, layer=i, k=0, n_tiles=all_tiles)
        z = _inproj(xs, mods, norm_g, w_in_b, cos, sa, sb, layer=i)

        a_out = _sgu(z, sgu_g, sgu_w, sgu_b, layer=i, n_rows=n_rows)
        b_out = _attention(z, attn_sink, layer=i, n_rows=n_rows)
        c_out = _hyena(z, 0, hy_conv_w, conv_b, skip, filt_lat, f_lat, g_lat, tw_lat, layer=i, seq=SEQ,
                       out_rows=n_rows, out_rb0=0)
        if not last:
            c_out = _hyena(z, ctx_rb, hy_conv_w, conv_b, skip, filt_ctx, f_ctx, g_ctx, tw_ctx, layer=i,
                           seq=CTX_LEN, out_rows=n_rows, out_rb0=ctx_rb, alias_out=c_out)

        xs = _outproj(a_out, b_out, c_out, w_out_b, xs, mods, norm_g, layer=i, n_tiles=n_tiles)
        jobs = [] if last else [(w, i + 1) for w in (ffn1_gate, ffn1_up, ffn1_down)]
        xs, w1 = _ffn(xs, mods, norm_g, w2g, w2u, w2d, jobs, layer=i, k=2, n_tiles=n_tiles)

    return xs[:N_LAT].reshape(BATCH, SEQ, D_MODEL)
```

```python
import functools
import math

import jax
import jax.numpy as jnp
from jax import lax
from jax.experimental import pallas as pl
from jax.experimental.pallas import tpu as pltpu

D_MODEL = 2048
BATCH = 2
SEQ = 4096
DEPTH = 4
GRID_W = 64
CTX_LEN = 256
HEAD_DIM = 128
NORM_EPS = 1e-6
N_MOD = 9
D_FF = 5632
SGU_GROUPS = 4
SGU_WIDTH = SGU_GROUPS * HEAD_DIM
CHUNK = 128
N_Q_HEADS = 8
N_KV_HEADS = 2
Q_GROUP = N_Q_HEADS // N_KV_HEADS
ATTN_WIDTH = N_Q_HEADS * HEAD_DIM
KV_WIDTH = N_KV_HEADS * HEAD_DIM
WINDOW = 128
BLOCK = 128
ROPE_BASE = 10000.0
NEG_INF = -1e30
HY_GROUPS = 4
HY_WIDTH = HY_GROUPS * HEAD_DIM
HY_ORDER = 2
HY_SHORT = 3
HY_BANDS = 16
HY_EMB = 1 + 2 * HY_BANDS
HY_HIDDEN = 64
HY_TARGET = 1e-2
HY_FAST_DECAY = 0.3
HY_SLOW_DECAY = 1.5
D_MIX = SGU_WIDTH + ATTN_WIDTH + HY_WIDTH
A_END = 2 * SGU_WIDTH
Q_END = A_END + ATTN_WIDTH
K_END = Q_END + KV_WIDTH
V_END = K_END + KV_WIDTH
IN_COLS = V_END + (HY_ORDER + 1) * HY_WIDTH

N_LAT = BATCH * SEQ
N_CTX = BATCH * CTX_LEN
N_TOK = N_LAT + N_CTX
MOD_ROWS = 8

TM = 512
TF = 512
FFN_ROWS = 32
BF16_SUBLANES = 16
TN_IN = 1024
TN_ADA = 1024
HY_COLS = 256
HY_FILT_COLS = 2 * HY_COLS
HY_KF = 256
HY_LEVELS = 3
HY_PHASES = 2 ** HY_LEVELS
HY_ROWS = 512
HY_INV_ROWS = 1024
HY_EMB_PAD = 128
VMEM_LIMIT = 56 * 1024 * 1024

F32 = jnp.float32
BF16 = jnp.bfloat16


def _params(*sem):
    return pltpu.CompilerParams(dimension_semantics=sem, vmem_limit_bytes=VMEM_LIMIT)


def _mod_row(i):
    return jnp.minimum(i // (SEQ // TM), BATCH)


def _silu(a):
    return a / (1.0 + jnp.exp(-a))


def _rms(x, g):
    return x * lax.rsqrt(jnp.mean(x * x, axis=-1, keepdims=True) + NORM_EPS) * g


def _ada_kernel(c_ref, w_ref, b_ref, o_ref):
    s = _silu(c_ref[...]).astype(BF16)
    o_ref[...] = jnp.dot(s, w_ref[...].astype(BF16), preferred_element_type=F32) + b_ref[...]


def _ada_mods(cc, w_ada, b_ada):
    n_out = N_MOD * D_MODEL
    out = pl.pallas_call(
        _ada_kernel,
        grid=(DEPTH, n_out // TN_ADA),
        in_specs=[
            pl.BlockSpec((MOD_ROWS, D_MODEL), lambda l, j: (0, 0)),
            pl.BlockSpec((None, D_MODEL, TN_ADA), lambda l, j: (l, 0, j)),
            pl.BlockSpec((None, 1, TN_ADA), lambda l, j: (l, 0, j)),
        ],
        out_specs=pl.BlockSpec((None, MOD_ROWS, TN_ADA), lambda l, j: (l, 0, j)),
        out_shape=jax.ShapeDtypeStruct((DEPTH, MOD_ROWS, n_out), F32),
        compiler_params=_params("arbitrary", "arbitrary"),
        name="ada_mods",
    )(cc, w_ada, b_ada.reshape(DEPTH, 1, n_out))
    return out.reshape(DEPTH, MOD_ROWS, N_MOD, D_MODEL)


def _ffn_kernel(*refs, k, nf, n_jobs):
    x_ref, mod_ref, g_ref, wg_ref, wu_ref, wd_ref = refs[:6]
    cast_src = refs[6:6 + n_jobs]
    o_ref = refs[6 + n_jobs]
    cast_dst = refs[7 + n_jobs:7 + 2 * n_jobs]
    h_ref, acc_ref, p0_ref, p1_ref = refs[7 + 2 * n_jobs:]
    f = pl.program_id(1)

    def up(p_ref):
        h = h_ref[...]
        a = jnp.dot(h, wg_ref[...], preferred_element_type=F32)
        b = jnp.dot(h, wu_ref[...], preferred_element_type=F32)
        p_ref[...] = (_silu(a) * b).astype(BF16)

    def down(p_ref, first=False):
        d = jnp.dot(p_ref[...], wd_ref[...], preferred_element_type=F32)
        if first:
            acc_ref[...] = d
        else:
            acc_ref[...] += d

    def casts():
        for src, dst in zip(cast_src, cast_dst):
            dst[...] = src[...].astype(BF16)

    @pl.when(f == 0)
    def _():
        g = g_ref[2 * k:2 * k + 1, :]
        m1 = 1.0 + mod_ref[3 * k:3 * k + 1, :]
        m2 = mod_ref[3 * k + 1:3 * k + 2, :]
        for r0 in range(0, TM, FFN_ROWS):
            rs = slice(r0, r0 + FFN_ROWS)
            h_ref[rs, :] = (_rms(x_ref[rs, :], g) * m1 + m2).astype(BF16)
        up(p0_ref)
        casts()

    @pl.when(f == 1)
    def _():
        up(p1_ref)
        down(p0_ref, first=True)
        casts()

    @pl.when((f > 1) & (f < nf) & (lax.rem(f, 2) == 0))
    def _():
        up(p0_ref)
        down(p1_ref)
        casts()

    @pl.when((f > 1) & (f < nf) & (lax.rem(f, 2) == 1))
    def _():
        up(p1_ref)
        down(p0_ref)
        casts()

    @pl.when(f == nf)
    def _():
        down(p1_ref if (nf - 1) % 2 else p0_ref)
        casts()
        g = g_ref[2 * k + 1:2 * k + 2, :]
        gate = 0.5 * mod_ref[3 * k + 2:3 * k + 3, :]
        for r0 in range(0, TM, FFN_ROWS):
            rs = slice(r0, r0 + FFN_ROWS)
            o_ref[rs, :] = x_ref[rs, :] + gate * _rms(acc_ref[rs, :], g)


def _ffn(x, mods, norm_g, wg, wu, wd, cast_jobs, *, layer, k, n_tiles):
    nf = D_FF // TF
    row_steps = nf + 1
    n_steps = n_tiles * row_steps
    in_specs = [
        pl.BlockSpec((TM, D_MODEL), lambda i, f: (i, 0)),
        pl.BlockSpec((None, None, N_MOD, D_MODEL), lambda i, f: (layer, _mod_row(i), 0, 0)),
        pl.BlockSpec((None, 6, D_MODEL), lambda i, f: (layer, 0, 0)),
        pl.BlockSpec((D_MODEL, TF), lambda i, f: (0, jnp.minimum(f, nf - 1))),
        pl.BlockSpec((D_MODEL, TF), lambda i, f: (0, jnp.minimum(f, nf - 1))),
        pl.BlockSpec((TF, D_MODEL), lambda i, f: (jnp.maximum(f - 1, 0), 0)),
    ]
    out_specs = [pl.BlockSpec((TM, D_MODEL), lambda i, f: (i, 0))]
    out_shape = [jax.ShapeDtypeStruct((n_tiles * TM, D_MODEL), F32)]
    for w, src_layer in cast_jobs:
        n_rows, n_cols = w.shape[1:]
        rb = BF16_SUBLANES * pl.cdiv(pl.cdiv(n_rows, n_steps), BF16_SUBLANES)
        assert n_rows % rb == 0
        last_blk = n_rows // rb - 1

        def step_block(i, f, last_blk=last_blk):
            return jnp.minimum(i * row_steps + f, last_blk)

        in_specs.append(pl.BlockSpec((None, rb, n_cols),
                                     lambda i, f, sl=src_layer, sb=step_block: (sl, sb(i, f), 0)))
        out_specs.append(pl.BlockSpec((rb, n_cols), lambda i, f, sb=step_block: (sb(i, f), 0)))
        out_shape.append(jax.ShapeDtypeStruct((n_rows, n_cols), BF16))
    outs = pl.pallas_call(
        functools.partial(_ffn_kernel, k=k, nf=nf, n_jobs=len(cast_jobs)),
        grid=(n_tiles, row_steps),
        in_specs=in_specs,
        out_specs=out_specs,
        out_shape=out_shape,
        scratch_shapes=[pltpu.VMEM((TM, D_MODEL), BF16), pltpu.VMEM((TM, D_MODEL), F32),
                        pltpu.VMEM((TM, TF), BF16), pltpu.VMEM((TM, TF), BF16)],
        compiler_params=_params("arbitrary", "arbitrary"),
        name=f"ffn{k}",
    )(x, mods, norm_g, wg, wu, wd, *[w for w, _ in cast_jobs])
    return outs[0], list(outs[1:])


def _rope(t, cos, sa, sb):
    return t * cos + pltpu.roll(t, 3 * HEAD_DIM // 4, 1) * sa + pltpu.roll(t, HEAD_DIM // 4, 1) * sb


def _inproj_kernel(x_ref, mod_ref, g_ref, w_ref, cos_ref, sa_ref, sb_ref, o_ref):
    y = _rms(x_ref[...], g_ref[2:3, :])
    h = (y * (1.0 + mod_ref[3:4, :]) + mod_ref[4:5, :]).astype(BF16)
    cos, sa, sb = cos_ref[...], sa_ref[...], sb_ref[...]
    n_rope = (K_END - A_END) // HEAD_DIM
    for n in range(IN_COLS // TN_IN):
        c0 = n * TN_IN
        z = jnp.dot(h, w_ref[:, c0:c0 + TN_IN], preferred_element_type=F32)
        for hd in range(TN_IN // HEAD_DIM):
            col = c0 + hd * HEAD_DIM
            zs = z[:, hd * HEAD_DIM:(hd + 1) * HEAD_DIM]
            if A_END <= col < A_END + n_rope * HEAD_DIM:
                zs = _rope(zs, cos, sa, sb)
            o_ref[:, col:col + HEAD_DIM] = zs


def _inproj(x, mods, norm_g, w_in, cos, sa, sb, *, layer):
    n_tiles = N_TOK // TM
    tab = pl.BlockSpec((TM, HEAD_DIM), lambda i: (i, 0))
    return pl.pallas_call(
        _inproj_kernel,
        grid=(n_tiles,),
        in_specs=[
            pl.BlockSpec((TM, D_MODEL), lambda i: (i, 0)),
            pl.BlockSpec((None, None, N_MOD, D_MODEL), lambda i: (layer, _mod_row(i), 0, 0)),
            pl.BlockSpec((None, 6, D_MODEL), lambda i: (layer, 0, 0)),
            pl.BlockSpec((D_MODEL, IN_COLS), lambda i: (0, 0), pipeline_mode=pl.Buffered(1)),
            tab, tab, tab,
        ],
        out_specs=pl.BlockSpec((TM, IN_COLS), lambda i: (i, 0)),
        out_shape=jax.ShapeDtypeStruct((N_TOK, IN_COLS), F32),
        compiler_params=_params("arbitrary"),
        name="inproj",
    )(x, mods, norm_g, w_in, cos, sa, sb)


def _sgu_kernel(z_ref, g_ref, w_ref, b_ref, o_ref, *, rows):
    inv_sqrt2 = 1.0 / math.sqrt(2.0)
    for c in range(rows // CHUNK):
        rs = slice(c * CHUNK, (c + 1) * CHUNK)
        a = z_ref[rs, :]
        a = a * (lax.erf(a * inv_sqrt2) + 1.0) * 0.5
        u, v = a[:, :SGU_WIDTH], a[:, SGU_WIDTH:]
        mu = jnp.mean(v, axis=-1, keepdims=True)
        vc = v - mu
        var = jnp.mean(vc * vc, axis=-1, keepdims=True)
        v = (vc * lax.rsqrt(var + NORM_EPS) * g_ref[...]).astype(BF16)
        for g in range(SGU_GROUPS):
            cs = slice(g * HEAD_DIM, (g + 1) * HEAD_DIM)
            mixed = jnp.dot(w_ref[g].astype(BF16), v[:, cs], preferred_element_type=F32) + b_ref[:, g:g + 1]
            o_ref[rs, cs] = (u[:, cs] * mixed).astype(o_ref.dtype)


def _sgu(z, sgu_g, sgu_w, sgu_b, *, layer, n_rows):
    rows = TM
    return pl.pallas_call(
        functools.partial(_sgu_kernel, rows=rows),
        grid=(n_rows // rows,),
        in_specs=[
            pl.BlockSpec((rows, A_END), lambda i: (i, 0)),
            pl.BlockSpec((None, 1, SGU_WIDTH), lambda i: (layer, 0, 0)),
            pl.BlockSpec((None, SGU_GROUPS, CHUNK, CHUNK), lambda i: (layer, 0, 0, 0)),
            pl.BlockSpec((None, CHUNK, SGU_GROUPS), lambda i: (layer, 0, 0)),
        ],
        out_specs=pl.BlockSpec((rows, SGU_WIDTH), lambda i: (i, 0)),
        out_shape=jax.ShapeDtypeStruct((n_rows, SGU_WIDTH), BF16),
        compiler_params=_params("arbitrary"),
        name="sgu",
    )(z, sgu_g.reshape(DEPTH, 1, SGU_WIDTH), sgu_w, jnp.swapaxes(sgu_b, 1, 2))


def _attn_kernel(sink_ref, q_ref, kp_ref, kc_ref, kn_ref, vp_ref, vc_ref, vn_ref, kx_ref, vx_ref,
                 o_ref, *, layer, nb, n_lat_blocks):
    s_id = pl.program_id(0)
    is_lat = s_id < n_lat_blocks
    n = s_id % nb
    n_keys = CTX_LEN + 3 * BLOCK
    n_rows = Q_GROUP * BLOCK
    row_id = lax.broadcasted_iota(jnp.int32, (n_rows, 1), 0)
    row = lax.broadcasted_iota(jnp.int32, (n_rows, n_keys), 0) & (BLOCK - 1)
    col = lax.broadcasted_iota(jnp.int32, (n_rows, n_keys), 1)
    rel = col - (CTX_LEN + BLOCK)
    lo = jnp.where(is_lat, -n * BLOCK, 4 * BLOCK)
    hi = jnp.where(is_lat, (nb - n) * BLOCK, -4 * BLOCK)
    dist = rel - row
    valid = (col < CTX_LEN) | ((dist >= -WINDOW) & (dist <= WINDOW) & (rel >= lo) & (rel < hi))
    scale = HEAD_DIM ** -0.5
    nt = (((1,), (1,)), ((), ()))
    for kh in range(N_KV_HEADS):
        ks = slice(kh * HEAD_DIM, (kh + 1) * HEAD_DIM)
        keys = jnp.concatenate([kx_ref[:, ks], kp_ref[:, ks], kc_ref[:, ks], kn_ref[:, ks]], axis=0).astype(BF16)
        vals = jnp.concatenate([vx_ref[:, ks], vp_ref[:, ks], vc_ref[:, ks], vn_ref[:, ks]], axis=0).astype(BF16)
        heads = [kh * Q_GROUP + g for g in range(Q_GROUP)]
        q = jnp.concatenate([q_ref[:, hd * HEAD_DIM:(hd + 1) * HEAD_DIM] for hd in heads], axis=0).astype(BF16)
        sink = jnp.full((n_rows, 1), sink_ref[layer, heads[0]], F32)
        for g, hd in enumerate(heads[1:], start=1):
            sink = jnp.where(row_id >= g * BLOCK, sink_ref[layer, hd], sink)
        s = lax.dot_general(q, keys, nt, preferred_element_type=F32) * scale
        s = jnp.where(valid, s, NEG_INF)
        m = jnp.maximum(jnp.max(s, axis=-1, keepdims=True), sink)
        p = jnp.exp(s - m)
        denom = jnp.sum(p, axis=-1, keepdims=True) + jnp.exp(sink - m)
        probs = (p / denom).astype(BF16)
        out = jnp.dot(probs, vals, preferred_element_type=F32).astype(o_ref.dtype)
        for g, hd in enumerate(heads):
            o_ref[:, hd * HEAD_DIM:(hd + 1) * HEAD_DIM] = out[g * BLOCK:(g + 1) * BLOCK]


def _attention(z, attn_sink, *, layer, n_rows):
    nb = SEQ // BLOCK
    n_lat_blocks = N_LAT // BLOCK
    n_blocks = n_rows // BLOCK
    ctx_blocks = CTX_LEN // BLOCK
    q_col = A_END // ATTN_WIDTH
    k_col = Q_END // KV_WIDTH
    v_col = K_END // KV_WIDTH

    def lat(s):
        return s < n_lat_blocks

    def band(off, col):
        def index(s):
            b = s // nb
            nbr = jnp.clip(s % nb + off, 0, nb - 1) + b * nb
            return (jnp.where(lat(s), nbr, s), col)
        return pl.BlockSpec((BLOCK, KV_WIDTH), index)

    def ctx(col):
        def index(s):
            b = jnp.where(lat(s), s // nb, (s - n_lat_blocks) // ctx_blocks)
            return (N_LAT // CTX_LEN + b, col)
        return pl.BlockSpec((CTX_LEN, KV_WIDTH), index)

    return pl.pallas_call(
        functools.partial(_attn_kernel, layer=layer, nb=nb, n_lat_blocks=n_lat_blocks),
        grid=(n_blocks,),
        in_specs=[
            pl.BlockSpec(memory_space=pltpu.SMEM),
            pl.BlockSpec((BLOCK, ATTN_WIDTH), lambda s: (s, q_col)),
            band(-1, k_col), band(0, k_col), band(1, k_col),
            band(-1, v_col), band(0, v_col), band(1, v_col),
            ctx(k_col), ctx(v_col),
        ],
        out_specs=pl.BlockSpec((BLOCK, ATTN_WIDTH), lambda s: (s, 0)),
        out_shape=jax.ShapeDtypeStruct((n_rows, ATTN_WIDTH), BF16),
        compiler_params=_params("arbitrary"),
        name="attn",
    )(attn_sink, z, z, z, z, z, z, z, z, z)


def _short_conv_phases(ref, w_ref, b_ref, m0, rows, n_m):
    ph = HY_PHASES
    z = [ref[pl.ds(ph * m0 + r, rows, stride=ph), :] for r in range(ph)]
    m = lax.broadcasted_iota(jnp.int32, (rows, 1), 0) + m0
    if m0 == 0:
        z_prev = jnp.where(m == 0, 0.0, pltpu.roll(z[ph - 1], 1, 0))
    else:
        z_prev = ref[pl.ds(ph * m0 - 1, rows, stride=ph), :]
    if m0 + rows == n_m:
        z_next = jnp.where(m == n_m - 1, 0.0, pltpu.roll(z[0], rows - 1, 0))
    else:
        z_next = ref[pl.ds(ph * m0 + ph, rows, stride=ph), :]
    ext = [z_prev] + z + [z_next]
    w0, w1, w2, b = w_ref[0:1, :], w_ref[1:2, :], w_ref[2:3, :], b_ref[...]
    return [ext[r] * w0 + ext[r + 1] * w1 + ext[r + 2] * w2 + b for r in range(ph)]


def _cmul(a, b):
    return a[0] * b[0] - a[1] * b[1], a[0] * b[1] + a[1] * b[0]


def _cadd(a, b):
    return a[0] + b[0], a[1] + b[1]


def _csub(a, b):
    return a[0] - b[0], a[1] - b[1]


def _conj(a):
    return a[0], -a[1]


def _split_fwd(e, o, w):
    t = _cmul(w, o)
    return _cadd(e, t), _conj(_csub(e, t))


def _split_inv(pp, pm, w):
    cm = _conj(pm)
    return _cadd(pp, cm), _cmul(_csub(pp, cm), _conj(w))


def _alt_sign(r0, rows):
    t = lax.broadcasted_iota(jnp.int32, (rows, 1), 0) + r0
    return jnp.where((t & 1) == 0, 1.0, -1.0)


def _hy_chunks(seq, chunk=None):
    rows = min(chunk or HY_ROWS, seq)
    return [(r * rows, rows) for r in range(seq // rows)]


def _filt_kernel(feat_ref, w1_ref, b1_ref, w2_ref, b2_ref, w3_ref, fr_ref, dl_ref,
                 hb_ref, asum_ref, hnyq_ref, hd_ref, *, seq):
    hp = lax.Precision.HIGHEST

    @pl.when(pl.program_id(1) == 0)
    def _():
        fr = fr_ref[...]
        hd = jnp.sin(fr * (jnp.dot(feat_ref[...], w1_ref[...], precision=hp, preferred_element_type=F32)
                           + b1_ref[...]))
        hd_ref[...] = jnp.sin(fr * (jnp.dot(hd, w2_ref[...], precision=hp, preferred_element_type=F32)
                                    + b2_ref[...]))

    q = seq // HY_PHASES
    asum = jnp.zeros((1, HY_FILT_COLS), F32)
    alt = [jnp.zeros((1, HY_FILT_COLS), F32) for _ in range(HY_PHASES)]
    for r0, rows in _hy_chunks(seq, min(HY_ROWS, q)):
        rs = slice(r0, r0 + rows)
        h = jnp.dot(hd_ref[rs, :].astype(BF16), w3_ref[...].astype(BF16), preferred_element_type=F32)
        h = h * jnp.exp(-feat_ref[rs, 0:1] * dl_ref[...])
        asum = asum + jnp.sum(jnp.abs(h), axis=0, keepdims=True)
        alt[r0 // q] = alt[r0 // q] + jnp.sum(h * _alt_sign(r0 % q, rows), axis=0, keepdims=True)
        hb_ref[rs, :] = h.astype(BF16)
    asum_ref[...] = asum
    for r in range(HY_PHASES):
        hnyq_ref[r:r + 1, :] = alt[r]


def _hyena_filters(seq, w1, b1, w2, b2, w3, freq):
    n_tiles = HY_WIDTH // HY_COLS
    w3 = w3.reshape(DEPTH, HY_HIDDEN, HY_ORDER, 2, n_tiles, HY_COLS)
    w3 = jnp.swapaxes(w3, 3, 4).reshape(DEPTH, HY_HIDDEN, HY_ORDER * 2 * HY_WIDTH)
    t = jnp.linspace(0.0, 1.0, seq, dtype=F32)[:, None]
    w = (2.0 * math.pi / seq) * jnp.arange(seq, dtype=F32)[:, None]
    bands = jnp.linspace(1e-4, HY_BANDS - 1, HY_BANDS, dtype=F32)[None, :]
    feat = jnp.concatenate([t, jnp.cos(bands * w), -jnp.sin(bands * w)], axis=-1)
    feat = jnp.pad(feat, ((0, 0), (0, HY_EMB_PAD - HY_EMB)))
    feat = jnp.concatenate([feat[r::HY_PHASES] for r in range(HY_PHASES)], axis=0)
    w1p = jnp.pad(w1, ((0, 0), (0, HY_EMB_PAD - HY_EMB), (0, 0)))
    deltas = jnp.abs(jnp.linspace(math.log(HY_TARGET) / HY_FAST_DECAY,
                                  math.log(HY_TARGET) / HY_SLOW_DECAY, HY_WIDTH, dtype=F32))
    deltas = jnp.broadcast_to(deltas.reshape(1, n_tiles, 1, HY_COLS), (HY_ORDER, n_tiles, 2, HY_COLS))
    deltas = deltas.reshape(1, HY_ORDER * 2 * HY_WIDTH)
    n_cols = HY_ORDER * 2 * HY_WIDTH

    def row(a):
        return a.reshape(DEPTH, 1, HY_HIDDEN)

    vec = pl.BlockSpec((None, 1, HY_HIDDEN), lambda l, j: (l, 0, 0))
    stat = pl.BlockSpec((None, 1, HY_FILT_COLS), lambda l, j: (l, 0, j))
    return pl.pallas_call(
        functools.partial(_filt_kernel, seq=seq),
        grid=(DEPTH, n_cols // HY_FILT_COLS),
        in_specs=[
            pl.BlockSpec((seq, HY_EMB_PAD), lambda l, j: (0, 0)),
            pl.BlockSpec((None, HY_EMB_PAD, HY_HIDDEN), lambda l, j: (l, 0, 0)),
            vec,
            pl.BlockSpec((None, HY_HIDDEN, HY_HIDDEN), lambda l, j: (l, 0, 0)),
            vec,
            pl.BlockSpec((None, HY_HIDDEN, HY_FILT_COLS), lambda l, j: (l, 0, j)),
            vec,
            pl.BlockSpec((1, HY_FILT_COLS), lambda l, j: (0, j)),
        ],
        out_specs=[pl.BlockSpec((None, seq, HY_FILT_COLS), lambda l, j: (l, 0, j)), stat,
                   pl.BlockSpec((None, HY_PHASES, HY_FILT_COLS), lambda l, j: (l, 0, j))],
        out_shape=[jax.ShapeDtypeStruct((DEPTH, seq, n_cols), BF16),
                   jax.ShapeDtypeStruct((DEPTH, 1, n_cols), F32),
                   jax.ShapeDtypeStruct((DEPTH, HY_PHASES, n_cols), F32)],
        scratch_shapes=[pltpu.VMEM((seq, HY_HIDDEN), F32)],
        compiler_params=_params("arbitrary", "arbitrary"),
        name=f"hy_filt_l{seq}",
    )(feat, w1p, row(b1), w2, row(b2), w3, row(freq), deltas)


def _dft_matrices(seq, kf):
    n_fft = 2 * seq
    n_ch = seq // kf
    th = 2.0 * math.pi / n_fft
    c = jnp.arange(n_ch, dtype=jnp.int32)[:, None]
    kl = jnp.arange(kf, dtype=jnp.int32)[:, None]
    n = jnp.arange(seq, dtype=jnp.int32)[None, :]
    a1 = th * ((c * kf * n) % n_fft).astype(F32)
    a2 = th * ((kl * n) % n_fft).astype(F32)
    c1, s1, c2, s2 = jnp.cos(a1), jnp.sin(a1), jnp.cos(a2), jnp.sin(a2)
    f_cos = c1[:, None, :] * c2[None, :, :] - s1[:, None, :] * s2[None, :, :]
    f_sin = s1[:, None, :] * c2[None, :, :] + c1[:, None, :] * s2[None, :, :]
    fmat = jnp.concatenate([f_cos, -f_sin], axis=1).astype(BF16)
    c2t, s2t = c2.T, s2.T
    g_cos = c1[:, :, None] * c2t[None, :, :] - s1[:, :, None] * s2t[None, :, :]
    g_sin = s1[:, :, None] * c2t[None, :, :] + c1[:, :, None] * s2t[None, :, :]
    gmat = jnp.concatenate([g_cos, -g_sin], axis=2).astype(BF16)
    return fmat, gmat


def _twiddle_plan(seq):
    freqs = [(0, 1)]
    plan = [None] * HY_LEVELS
    for d in reversed(range(HY_LEVELS)):
        l_d = seq >> d
        plan[d] = [(off, sgn, 2 * l_d) for off, sgn in freqs]
        freqs = [f for off, sgn in freqs for f in ((off, sgn), (l_d - off, -sgn))]
    index = 0
    for d in range(HY_LEVELS):
        plan[d] = [(index + j,) + t for j, t in enumerate(plan[d])]
        index += len(plan[d])
    return plan


def _twiddles(seq):
    k = jnp.arange(seq // HY_PHASES, dtype=F32)[:, None]
    tabs = []
    for level in _twiddle_plan(seq):
        for _, off, sgn, mod in level:
            ang = jnp.broadcast_to((2.0 * math.pi / mod) * (off + sgn * k), (seq // HY_PHASES, HEAD_DIM))
            tabs += [jnp.cos(ang), jnp.sin(ang)]
    return jnp.stack(tabs, axis=0)


def _spectrum_tree(leaves, twiddle, depth=0):
    if len(leaves) == 1:
        return leaves
    even = _spectrum_tree(leaves[0::2], twiddle, depth + 1)
    odd = _spectrum_tree(leaves[1::2], twiddle, depth + 1)
    out = []
    for j, (e, o) in enumerate(zip(even, odd)):
        out += list(_split_fwd(e, o, twiddle(depth, j)))
    return out


def _synthesis_tree(vals, twiddle, depth=0):
    if len(vals) == 1:
        return vals
    even, odd = [], []
    for j in range(len(vals) // 2):
        q0, q1 = _split_inv(vals[2 * j], vals[2 * j + 1], twiddle(depth, j))
        even.append(q0)
        odd.append(q1)
    out = [None] * len(vals)
    out[0::2] = _synthesis_tree(even, twiddle, depth + 1)
    out[1::2] = _synthesis_tree(odd, twiddle, depth + 1)
    return out


def _hy_pre_kernel(z_ref, w_ref, b_ref, ub_ref, ualt_ref, *, seq):
    q = seq // HY_PHASES
    alt = [jnp.zeros((1, HEAD_DIM), F32) for _ in range(HY_PHASES)]
    for m0, rows in _hy_chunks(q):
        v = _short_conv_phases(z_ref, w_ref, b_ref, m0, rows, q)
        sign = _alt_sign(m0, rows)
        for r in range(HY_PHASES):
            ub_ref[r * q + m0:r * q + m0 + rows, :] = v[r].astype(BF16)
            alt[r] = alt[r] + jnp.sum(v[r] * sign, axis=0, keepdims=True)
    for r in range(HY_PHASES):
        ualt_ref[r:r + 1, :] = alt[r]


def _hy_spec_kernel(u_ref, h_ref, a_ref, f_ref, g_ref, tw_ref, y_ref, *, seq, kf):
    c = pl.program_id(1)
    cw = HY_COLS
    q = seq // HY_PHASES
    n_fft = 2 * seq

    @pl.when(c == 0)
    def _():
        y_ref[...] = jnp.zeros_like(y_ref)

    f = f_ref[...]
    plan = _twiddle_plan(seq)

    def twiddles(width):
        rep = width // HEAD_DIM
        cache = {}

        def get(depth, j):
            idx = plan[depth][j][0]
            if idx not in cache:
                cache[idx] = (jnp.concatenate([tw_ref[2 * idx]] * rep, axis=1),
                              -jnp.concatenate([tw_ref[2 * idx + 1]] * rep, axis=1))
            return cache[idx]
        return get

    def spectrum(ref, width):
        leaves = []
        for r in range(HY_PHASES):
            d = jnp.dot(f, ref[r * q:(r + 1) * q, :], preferred_element_type=F32)
            leaves.append((d[:kf], d[kf:]))
        return _spectrum_tree(leaves, twiddles(width))

    row = lax.broadcasted_iota(jnp.int32, (kf, 1), 0)
    a = a_ref[...]
    wgt = jnp.where((row == 0) & (c == 0), 1.0 / n_fft, 2.0 / n_fft) / (a[:, :cw] + a[:, cw:] + 1e-6)
    filt = [((x[0][:, :cw] + x[0][:, cw:]) * wgt, (x[1][:, :cw] - x[1][:, cw:]) * wgt)
            for x in spectrum(h_ref, 2 * cw)]
    sig = spectrum(u_ref, BATCH * cw)
    tw_c = twiddles(cw)
    out = [[] for _ in range(HY_PHASES)]
    for b in range(BATCH):
        bs = slice(b * cw, (b + 1) * cw)
        p = [_cmul((x[0][:, bs], x[1][:, bs]), s) for x, s in zip(sig, filt)]
        for r, val in enumerate(_synthesis_tree(p, tw_c)):
            out[r].append(val)
    for r in range(HY_PHASES):
        spec = jnp.concatenate([jnp.concatenate([v[0] for v in out[r]], axis=1),
                                jnp.concatenate([v[1] for v in out[r]], axis=1)], axis=0).astype(BF16)
        for r0, rows in _hy_chunks(q, HY_INV_ROWS):
            y_ref[r * q + r0:r * q + r0 + rows, :] += jnp.dot(g_ref[r0:r0 + rows, :], spec,
                                                              preferred_element_type=F32)


def _edge_terms(ualt_ref, hf_ref, hb_ref, af_ref, ab_ref, seq):
    k_edge = seq // HY_PHASES
    plan = _twiddle_plan(seq)

    def twiddle(depth, j):
        _, off, sgn, mod = plan[depth][j]
        ang = 2.0 * math.pi * (off + sgn * k_edge) / mod
        return math.cos(ang), -math.sin(ang)

    def spectrum(ref):
        return _spectrum_tree([(ref[r:r + 1, :], 0.0) for r in range(HY_PHASES)], twiddle)

    sig, fwd, bwd = spectrum(ualt_ref), spectrum(hf_ref), spectrum(hb_ref)
    p = [_cmul(u, _cadd(f, _conj(b))) for u, f, b in zip(sig, fwd, bwd)]
    scale = (1.0 / (2 * seq)) / (af_ref[...] + ab_ref[...] + 1e-6)
    return [leaf[0] * scale for leaf in _synthesis_tree(p, twiddle)]


def _hy_mid_kernel(zv_ref, zx_ref, y_ref, ualt_ref, hf_ref, hb_ref, af_ref, ab_ref, wv_ref, bv_ref,
                   wx_ref, bx_ref, skip_ref, s_ref, sb_ref, salt_ref, *, seq):
    q = seq // HY_PHASES
    amp = _edge_terms(ualt_ref, hf_ref, hb_ref, af_ref, ab_ref, seq)
    alt = [jnp.zeros((1, HEAD_DIM), F32) for _ in range(HY_PHASES)]
    skip = skip_ref[...]
    for m0, rows in _hy_chunks(q):
        sign = _alt_sign(m0, rows)
        v = _short_conv_phases(zv_ref, wv_ref, bv_ref, m0, rows, q)
        x1 = _short_conv_phases(zx_ref, wx_ref, bx_ref, m0, rows, q)
        for r in range(HY_PHASES):
            rs = slice(r * q + m0, r * q + m0 + rows)
            s = x1[r] * (y_ref[rs, :] + sign * amp[r] + v[r] * skip)
            s_ref[rs, :] = s
            sb_ref[rs, :] = s.astype(BF16)
            alt[r] = alt[r] + jnp.sum(s * sign, axis=0, keepdims=True)
    for r in range(HY_PHASES):
        salt_ref[r:r + 1, :] = alt[r]


def _hy_post_kernel(zx_ref, y_ref, s_ref, salt_ref, hf_ref, hb_ref, af_ref, ab_ref, wx_ref, bx_ref,
                    skip_ref, *rest, seq):
    o_ref = rest[-1]
    q = seq // HY_PHASES
    amp = _edge_terms(salt_ref, hf_ref, hb_ref, af_ref, ab_ref, seq)
    skip = skip_ref[...]
    for m0, rows in _hy_chunks(q):
        sign = _alt_sign(m0, rows)
        x2 = _short_conv_phases(zx_ref, wx_ref, bx_ref, m0, rows, q)
        for r in range(HY_PHASES):
            rs = slice(r * q + m0, r * q + m0 + rows)
            y = y_ref[rs, :] + sign * amp[r] + s_ref[rs, :] * skip
            o_ref[pl.ds(HY_PHASES * m0 + r, rows, stride=HY_PHASES), :] = x2[r] * y


def _hyena(z, z_rb0, conv_w, conv_b, skip, filt, fmat, gmat, tw, *, layer, seq, out_rows, out_rb0,
           alias_out=None):
    hb, asum, halt = filt
    q = seq // HY_PHASES
    kf = min(HY_KF, q)
    n_ch = q // kf
    n_cols = HY_WIDTH // HY_COLS
    ew = HEAD_DIM
    sub = HY_COLS // ew
    n_ew = HY_WIDTH // ew
    hy0 = V_END // ew

    def zcols(which):
        return pl.BlockSpec((seq, ew), lambda b, e: (z_rb0 + b, hy0 + which * n_ew + e))

    def cw(which):
        return pl.BlockSpec((None, HY_SHORT, ew), lambda b, e: (layer, 0, which * n_ew + e))

    def cb(which):
        return pl.BlockSpec((None, 1, ew), lambda b, e: (layer, 0, which * n_ew + e))

    def fstat(order, direction, rows):
        return pl.BlockSpec((None, rows, ew),
                            lambda b, e: (layer, 0, ((order * n_cols + e // sub) * 2 + direction) * sub + e % sub))

    def skip_spec(order):
        return pl.BlockSpec((None, 1, ew), lambda b, e: (layer * HY_ORDER + order, 0, e))

    def act_col(b, e):
        return ((e // sub) * BATCH + b) * sub + e % sub

    seq_blk = pl.BlockSpec((seq, ew), lambda b, e: (0, act_col(b, e)))
    alt_spec = pl.BlockSpec((HY_PHASES, ew), lambda b, e: (0, act_col(b, e)))
    ew_params = _params("arbitrary", "arbitrary")
    act = jax.ShapeDtypeStruct((seq, BATCH * HY_WIDTH), F32)
    act_b = jax.ShapeDtypeStruct((seq, BATCH * HY_WIDTH), BF16)
    alt_shape = jax.ShapeDtypeStruct((HY_PHASES, BATCH * HY_WIDTH), F32)

    def spectral(ub, order):
        return pl.pallas_call(
            functools.partial(_hy_spec_kernel, seq=seq, kf=kf),
            grid=(n_cols, n_ch),
            in_specs=[
                pl.BlockSpec((seq, BATCH * HY_COLS), lambda j, c: (0, j), pipeline_mode=pl.Buffered(1)),
                pl.BlockSpec((None, seq, 2 * HY_COLS), lambda j, c: (layer, 0, order * n_cols + j),
                             pipeline_mode=pl.Buffered(1)),
                pl.BlockSpec((None, 1, 2 * HY_COLS), lambda j, c: (layer, 0, order * n_cols + j)),
                pl.BlockSpec((None, 2 * kf, q), lambda j, c: (c, 0, 0)),
                pl.BlockSpec((None, q, 2 * kf), lambda j, c: (c, 0, 0)),
                pl.BlockSpec((tw.shape[0], kf, HEAD_DIM), lambda j, c: (0, c, 0)),
            ],
            out_specs=pl.BlockSpec((seq, BATCH * HY_COLS), lambda j, c: (0, j)),
            out_shape=act,
            compiler_params=_params("arbitrary", "arbitrary"),
            name=f"hy_spec{order}_l{seq}",
        )(ub, hb, asum, fmat, gmat, tw)

    ub, ualt = pl.pallas_call(
        functools.partial(_hy_pre_kernel, seq=seq),
        grid=(BATCH, n_ew),
        in_specs=[zcols(0), cw(0), cb(0)],
        out_specs=[seq_blk, alt_spec],
        out_shape=[act_b, alt_shape],
        compiler_params=ew_params,
        name=f"hy_pre_l{seq}",
    )(z, conv_w, conv_b)
    y0 = spectral(ub, 0)
    s, sb, salt = pl.pallas_call(
        functools.partial(_hy_mid_kernel, seq=seq),
        grid=(BATCH, n_ew),
        in_specs=[zcols(0), zcols(1), seq_blk, alt_spec, fstat(0, 0, HY_PHASES), fstat(0, 1, HY_PHASES), fstat(0, 0, 1),
                  fstat(0, 1, 1), cw(0), cb(0), cw(1), cb(1), skip_spec(0)],
        out_specs=[seq_blk, seq_blk, alt_spec],
        out_shape=[act, act_b, alt_shape],
        compiler_params=ew_params,
        name=f"hy_mid_l{seq}",
    )(z, z, y0, ualt, halt, halt, asum, asum, conv_w, conv_b, conv_w, conv_b, skip)
    y1 = spectral(sb, 1)
    in_specs = [zcols(2), seq_blk, seq_blk, alt_spec, fstat(1, 0, HY_PHASES), fstat(1, 1, HY_PHASES), fstat(1, 0, 1),
                fstat(1, 1, 1), cw(2), cb(2), skip_spec(1)]
    args = [z, y1, s, salt, halt, halt, asum, asum, conv_w, conv_b, skip]
    aliases = {}
    if alias_out is not None:
        in_specs.append(pl.BlockSpec(memory_space=pl.ANY))
        args.append(alias_out)
        aliases = {len(args) - 1: 0}
    return pl.pallas_call(
        functools.partial(_hy_post_kernel, seq=seq),
        grid=(BATCH, n_ew),
        in_specs=in_specs,
        out_specs=pl.BlockSpec((seq, ew), lambda b, e: (out_rb0 + b, e)),
        out_shape=jax.ShapeDtypeStruct((out_rows, HY_WIDTH), F32),
        input_output_aliases=aliases,
        compiler_params=ew_params,
        name=f"hy_post_l{seq}",
    )(*args)


def _outproj_kernel(a_ref, b_ref, c_ref, w_ref, x_ref, mod_ref, g_ref, o_ref):
    y = jnp.dot(a_ref[...], w_ref[:SGU_WIDTH, :], preferred_element_type=F32)
    y += jnp.dot(b_ref[...], w_ref[SGU_WIDTH:SGU_WIDTH + ATTN_WIDTH, :], preferred_element_type=F32)
    y += jnp.dot(c_ref[...].astype(BF16), w_ref[SGU_WIDTH + ATTN_WIDTH:, :], preferred_element_type=F32)
    o_ref[...] = x_ref[...] + mod_ref[5:6, :] * _rms(y, g_ref[3:4, :])


def _outproj(a, b, c, w_out, x, mods, norm_g, *, layer, n_tiles):
    return pl.pallas_call(
        _outproj_kernel,
        grid=(n_tiles,),
        in_specs=[
            pl.BlockSpec((TM, SGU_WIDTH), lambda i: (i, 0)),
            pl.BlockSpec((TM, ATTN_WIDTH), lambda i: (i, 0)),
            pl.BlockSpec((TM, HY_WIDTH), lambda i: (i, 0)),
            pl.BlockSpec((D_MIX, D_MODEL), lambda i: (0, 0), pipeline_mode=pl.Buffered(1)),
            pl.BlockSpec((TM, D_MODEL), lambda i: (i, 0)),
            pl.BlockSpec((None, None, N_MOD, D_MODEL), lambda i: (layer, _mod_row(i), 0, 0)),
            pl.BlockSpec((None, 6, D_MODEL), lambda i: (layer, 0, 0)),
        ],
        out_specs=pl.BlockSpec((TM, D_MODEL), lambda i: (i, 0)),
        out_shape=jax.ShapeDtypeStruct((n_tiles * TM, D_MODEL), F32),
        compiler_params=_params("arbitrary"),
        name="outproj",
    )(a, b, c, w_out, x, mods, norm_g)


def _rope_tables():
    rows = SEQ // GRID_W
    row = jnp.broadcast_to(jnp.arange(rows)[:, None], (rows, GRID_W)).reshape(SEQ).astype(F32)
    col = jnp.broadcast_to(jnp.arange(GRID_W)[None, :], (rows, GRID_W)).reshape(SEQ).astype(F32)
    half = HEAD_DIM // 2
    inv = ROPE_BASE ** (-jnp.arange(0, half, 2, dtype=F32) / half)
    ang_r = row[:, None] * inv
    ang_c = col[:, None] * inv
    ang = jnp.concatenate([ang_r, ang_r, ang_c, ang_c], axis=-1)
    cos, sin = jnp.cos(ang), jnp.sin(ang)
    first = (jnp.arange(HEAD_DIM) % half) < (half // 2)
    sa = jnp.where(first, -sin, 0.0)
    sb = jnp.where(first, 0.0, sin)

    def stack(lat, ctx_value):
        return jnp.concatenate([lat] * BATCH + [jnp.full((N_CTX, HEAD_DIM), ctx_value, F32)], axis=0)

    return stack(cos, 1.0), stack(sa, 0.0), stack(sb, 0.0)


def kernel(x, c, ctx, c_ctx, w_ada, b_ada, norm_g, ffn1_gate, ffn1_up, ffn1_down, w_in,
           sgu_g, sgu_w, sgu_b, attn_sink, hy_conv_w, hy_conv_b, hy_f_w1, hy_f_b1, hy_f_w2,
           hy_f_b2, hy_f_w3, hy_freq, hy_skip, w_out, ffn2_gate, ffn2_up, ffn2_down):
    xs = jnp.concatenate([x.reshape(N_LAT, D_MODEL), ctx.reshape(N_CTX, D_MODEL)], axis=0)
    cc = jnp.concatenate([c, c_ctx[None], jnp.zeros((MOD_ROWS - BATCH - 1, D_MODEL), F32)], axis=0)
    mods = _ada_mods(cc, w_ada, b_ada)

    w1 = [w[0].astype(BF16) for w in (ffn1_gate, ffn1_up, ffn1_down)]
    cos, sa, sb = _rope_tables()
    conv_b = hy_conv_b.reshape(DEPTH, 1, (HY_ORDER + 1) * HY_WIDTH)
    skip = hy_skip.reshape(DEPTH * HY_ORDER, 1, HY_WIDTH)
    filt_w = (hy_f_w1, hy_f_b1, hy_f_w2, hy_f_b2, hy_f_w3, hy_freq)
    filt_lat = _hyena_filters(SEQ, *filt_w)
    filt_ctx = _hyena_filters(CTX_LEN, *filt_w)
    f_lat, g_lat = _dft_matrices(SEQ // HY_PHASES, min(HY_KF, SEQ // HY_PHASES))
    f_ctx, g_ctx = _dft_matrices(CTX_LEN // HY_PHASES, min(HY_KF, CTX_LEN // HY_PHASES))
    tw_lat, tw_ctx = _twiddles(SEQ), _twiddles(CTX_LEN)

    all_tiles = N_TOK // TM
    lat_tiles = N_LAT // TM
    ctx_rb = N_LAT // CTX_LEN

    for i in range(DEPTH):
        last = i == DEPTH - 1
        n_tiles = lat_tiles if last else all_tiles
        n_rows = n_tiles * TM

        jobs = [(w, i) for w in (ffn2_gate, ffn2_up, ffn2_down, w_in, w_out)]
        xs, (w2g, w2u, w2d, w_in_b, w_out_b) = _ffn(xs, mods, norm_g, *w1, jobs, layer=i, k=0, n_tiles=all_tiles)
        z = _inproj(xs, mods, norm_g, w_in_b, cos, sa, sb, layer=i)

        a_out = _sgu(z, sgu_g, sgu_w, sgu_b, layer=i, n_rows=n_rows)
        b_out = _attention(z, attn_sink, layer=i, n_rows=n_rows)
        c_out = _hyena(z, 0, hy_conv_w, conv_b, skip, filt_lat, f_lat, g_lat, tw_lat, layer=i, seq=SEQ,
                       out_rows=n_rows, out_rb0=0)
        if not last:
            c_out = _hyena(z, ctx_rb, hy_conv_w, conv_b, skip, filt_ctx, f_ctx, g_ctx, tw_ctx, layer=i,
                           seq=CTX_LEN, out_rows=n_rows, out_rb0=ctx_rb, alias_out=c_out)

        xs = _outproj(a_out, b_out, c_out, w_out_b, xs, mods, norm_g, layer=i, n_tiles=n_tiles)
        jobs = [] if last else [(w, i + 1) for w in (ffn1_gate, ffn1_up, ffn1_down)]
        xs, w1 = _ffn(xs, mods, norm_g, w2g, w2u, w2d, jobs, layer=i, k=2, n_tiles=n_tiles)

    return xs[:N_LAT].reshape(BATCH, SEQ, D_MODEL)
```

```python
import functools
import math

import jax
import jax.numpy as jnp
from jax import lax
from jax.experimental import pallas as pl
from jax.experimental.pallas import tpu as pltpu

D_MODEL = 2048
BATCH = 2
SEQ = 4096
DEPTH = 4
GRID_W = 64
CTX_LEN = 256
HEAD_DIM = 128
NORM_EPS = 1e-6
N_MOD = 9
D_FF = 5632
SGU_GROUPS = 4
SGU_WIDTH = SGU_GROUPS * HEAD_DIM
CHUNK = 128
N_Q_HEADS = 8
N_KV_HEADS = 2
Q_GROUP = N_Q_HEADS // N_KV_HEADS
ATTN_WIDTH = N_Q_HEADS * HEAD_DIM
KV_WIDTH = N_KV_HEADS * HEAD_DIM
WINDOW = 128
BLOCK = 128
ROPE_BASE = 10000.0
NEG_INF = -1e30
HY_GROUPS = 4
HY_WIDTH = HY_GROUPS * HEAD_DIM
HY_ORDER = 2
HY_SHORT = 3
HY_BANDS = 16
HY_EMB = 1 + 2 * HY_BANDS
HY_HIDDEN = 64
HY_TARGET = 1e-2
HY_FAST_DECAY = 0.3
HY_SLOW_DECAY = 1.5
D_MIX = SGU_WIDTH + ATTN_WIDTH + HY_WIDTH
A_END = 2 * SGU_WIDTH
Q_END = A_END + ATTN_WIDTH
K_END = Q_END + KV_WIDTH
V_END = K_END + KV_WIDTH
IN_COLS = V_END + (HY_ORDER + 1) * HY_WIDTH

N_LAT = BATCH * SEQ
N_CTX = BATCH * CTX_LEN
N_TOK = N_LAT + N_CTX
MOD_ROWS = 8

TM = 512
TF = 512
FFN_ROWS = 32
BF16_SUBLANES = 16
TN_IN = 1024
TN_ADA = 1024
HY_COLS = 256
HY_FILT_COLS = 2 * HY_COLS
HY_KF = 256
HY_LEVELS = 3
HY_PHASES = 2 ** HY_LEVELS
HY_ROWS = 512
HY_INV_ROWS = 1024
HY_EMB_PAD = 128
VMEM_LIMIT = 56 * 1024 * 1024

F32 = jnp.float32
BF16 = jnp.bfloat16


def _params(*sem):
    return pltpu.CompilerParams(dimension_semantics=sem, vmem_limit_bytes=VMEM_LIMIT)


def _mod_row(i):
    return jnp.minimum(i // (SEQ // TM), BATCH)


def _silu(a):
    return a / (1.0 + jnp.exp(-a))


def _rms(x, g):
    return x * lax.rsqrt(jnp.mean(x * x, axis=-1, keepdims=True) + NORM_EPS) * g


def _ada_kernel(c_ref, w_ref, b_ref, o_ref):
    s = _silu(c_ref[...]).astype(BF16)
    o_ref[...] = jnp.dot(s, w_ref[...].astype(BF16), preferred_element_type=F32) + b_ref[...]


def _ada_mods(cc, w_ada, b_ada):
    n_out = N_MOD * D_MODEL
    out = pl.pallas_call(
        _ada_kernel,
        grid=(DEPTH, n_out // TN_ADA),
        in_specs=[
            pl.BlockSpec((MOD_ROWS, D_MODEL), lambda l, j: (0, 0)),
            pl.BlockSpec((None, D_MODEL, TN_ADA), lambda l, j: (l, 0, j)),
            pl.BlockSpec((None, 1, TN_ADA), lambda l, j: (l, 0, j)),
        ],
        out_specs=pl.BlockSpec((None, MOD_ROWS, TN_ADA), lambda l, j: (l, 0, j)),
        out_shape=jax.ShapeDtypeStruct((DEPTH, MOD_ROWS, n_out), F32),
        compiler_params=_params("arbitrary", "arbitrary"),
        name="ada_mods",
    )(cc, w_ada, b_ada.reshape(DEPTH, 1, n_out))
    return out.reshape(DEPTH, MOD_ROWS, N_MOD, D_MODEL)


def _ffn_kernel(*refs, k, nf, n_jobs):
    x_ref, mod_ref, g_ref, wg_ref, wu_ref, wd_ref = refs[:6]
    cast_src = refs[6:6 + n_jobs]
    o_ref = refs[6 + n_jobs]
    cast_dst = refs[7 + n_jobs:7 + 2 * n_jobs]
    h_ref, acc_ref = refs[7 + 2 * n_jobs:]
    f = pl.program_id(1)

    for src, dst in zip(cast_src, cast_dst):
        dst[...] = src[...].astype(BF16)

    @pl.when(f == 0)
    def _():
        g = g_ref[2 * k:2 * k + 1, :]
        m1 = 1.0 + mod_ref[3 * k:3 * k + 1, :]
        m2 = mod_ref[3 * k + 1:3 * k + 2, :]
        for r0 in range(0, TM, FFN_ROWS):
            rs = slice(r0, r0 + FFN_ROWS)
            h_ref[rs, :] = (_rms(x_ref[rs, :], g) * m1 + m2).astype(BF16)
        acc_ref[...] = jnp.zeros_like(acc_ref)

    h = h_ref[...]
    a = jnp.dot(h, wg_ref[...], preferred_element_type=F32)
    b = jnp.dot(h, wu_ref[...], preferred_element_type=F32)
    p = (_silu(a) * b).astype(BF16)
    acc_ref[...] += jnp.dot(p, wd_ref[...], preferred_element_type=F32)

    @pl.when(f == nf - 1)
    def _():
        g = g_ref[2 * k + 1:2 * k + 2, :]
        gate = 0.5 * mod_ref[3 * k + 2:3 * k + 3, :]
        for r0 in range(0, TM, FFN_ROWS):
            rs = slice(r0, r0 + FFN_ROWS)
            o_ref[rs, :] = x_ref[rs, :] + gate * _rms(acc_ref[rs, :], g)


def _ffn(x, mods, norm_g, wg, wu, wd, cast_jobs, *, layer, k, n_tiles):
    nf = D_FF // TF
    n_steps = n_tiles * nf
    in_specs = [
        pl.BlockSpec((TM, D_MODEL), lambda i, f: (i, 0)),
        pl.BlockSpec((None, None, N_MOD, D_MODEL), lambda i, f: (layer, _mod_row(i), 0, 0)),
        pl.BlockSpec((None, 6, D_MODEL), lambda i, f: (layer, 0, 0)),
        pl.BlockSpec((D_MODEL, TF), lambda i, f: (0, f)),
        pl.BlockSpec((D_MODEL, TF), lambda i, f: (0, f)),
        pl.BlockSpec((TF, D_MODEL), lambda i, f: (f, 0)),
    ]
    out_specs = [pl.BlockSpec((TM, D_MODEL), lambda i, f: (i, 0))]
    out_shape = [jax.ShapeDtypeStruct((n_tiles * TM, D_MODEL), F32)]
    for w, src_layer in cast_jobs:
        n_rows, n_cols = w.shape[1:]
        rb = BF16_SUBLANES * pl.cdiv(pl.cdiv(n_rows, n_steps), BF16_SUBLANES)
        assert n_rows % rb == 0
        last_blk = n_rows // rb - 1

        def step_block(i, f, last_blk=last_blk):
            return jnp.minimum(i * nf + f, last_blk)

        in_specs.append(pl.BlockSpec((None, rb, n_cols),
                                     lambda i, f, sl=src_layer, sb=step_block: (sl, sb(i, f), 0)))
        out_specs.append(pl.BlockSpec((rb, n_cols), lambda i, f, sb=step_block: (sb(i, f), 0)))
        out_shape.append(jax.ShapeDtypeStruct((n_rows, n_cols), BF16))
    outs = pl.pallas_call(
        functools.partial(_ffn_kernel, k=k, nf=nf, n_jobs=len(cast_jobs)),
        grid=(n_tiles, nf),
        in_specs=in_specs,
        out_specs=out_specs,
        out_shape=out_shape,
        scratch_shapes=[pltpu.VMEM((TM, D_MODEL), BF16), pltpu.VMEM((TM, D_MODEL), F32)],
        compiler_params=_params("arbitrary", "arbitrary"),
        name=f"ffn{k}",
    )(x, mods, norm_g, wg, wu, wd, *[w for w, _ in cast_jobs])
    return outs[0], list(outs[1:])


def _rope(t, cos, sa, sb):
    return t * cos + pltpu.roll(t, 3 * HEAD_DIM // 4, 1) * sa + pltpu.roll(t, HEAD_DIM // 4, 1) * sb


def _inproj_kernel(x_ref, mod_ref, g_ref, w_ref, cos_ref, sa_ref, sb_ref, o_ref):
    y = _rms(x_ref[...], g_ref[2:3, :])
    h = (y * (1.0 + mod_ref[3:4, :]) + mod_ref[4:5, :]).astype(BF16)
    cos, sa, sb = cos_ref[...], sa_ref[...], sb_ref[...]
    n_rope = (K_END - A_END) // HEAD_DIM
    for n in range(IN_COLS // TN_IN):
        c0 = n * TN_IN
        z = jnp.dot(h, w_ref[:, c0:c0 + TN_IN], preferred_element_type=F32)
        for hd in range(TN_IN // HEAD_DIM):
            col = c0 + hd * HEAD_DIM
            zs = z[:, hd * HEAD_DIM:(hd + 1) * HEAD_DIM]
            if A_END <= col < A_END + n_rope * HEAD_DIM:
                zs = _rope(zs, cos, sa, sb)
            o_ref[:, col:col + HEAD_DIM] = zs


def _inproj(x, mods, norm_g, w_in, cos, sa, sb, *, layer):
    n_tiles = N_TOK // TM
    tab = pl.BlockSpec((TM, HEAD_DIM), lambda i: (i, 0))
    return pl.pallas_call(
        _inproj_kernel,
        grid=(n_tiles,),
        in_specs=[
            pl.BlockSpec((TM, D_MODEL), lambda i: (i, 0)),
            pl.BlockSpec((None, None, N_MOD, D_MODEL), lambda i: (layer, _mod_row(i), 0, 0)),
            pl.BlockSpec((None, 6, D_MODEL), lambda i: (layer, 0, 0)),
            pl.BlockSpec((D_MODEL, IN_COLS), lambda i: (0, 0), pipeline_mode=pl.Buffered(1)),
            tab, tab, tab,
        ],
        out_specs=pl.BlockSpec((TM, IN_COLS), lambda i: (i, 0)),
        out_shape=jax.ShapeDtypeStruct((N_TOK, IN_COLS), F32),
        compiler_params=_params("arbitrary"),
        name="inproj",
    )(x, mods, norm_g, w_in, cos, sa, sb)


def _sgu_kernel(z_ref, g_ref, w_ref, b_ref, o_ref, *, rows):
    inv_sqrt2 = 1.0 / math.sqrt(2.0)
    for c in range(rows // CHUNK):
        rs = slice(c * CHUNK, (c + 1) * CHUNK)
        a = z_ref[rs, :]
        a = a * (lax.erf(a * inv_sqrt2) + 1.0) * 0.5
        u, v = a[:, :SGU_WIDTH], a[:, SGU_WIDTH:]
        mu = jnp.mean(v, axis=-1, keepdims=True)
        vc = v - mu
        var = jnp.mean(vc * vc, axis=-1, keepdims=True)
        v = (vc * lax.rsqrt(var + NORM_EPS) * g_ref[...]).astype(BF16)
        for g in range(SGU_GROUPS):
            cs = slice(g * HEAD_DIM, (g + 1) * HEAD_DIM)
            mixed = jnp.dot(w_ref[g].astype(BF16), v[:, cs], preferred_element_type=F32) + b_ref[:, g:g + 1]
            o_ref[rs, cs] = (u[:, cs] * mixed).astype(o_ref.dtype)


def _sgu(z, sgu_g, sgu_w, sgu_b, *, layer, n_rows):
    rows = TM
    return pl.pallas_call(
        functools.partial(_sgu_kernel, rows=rows),
        grid=(n_rows // rows,),
        in_specs=[
            pl.BlockSpec((rows, A_END), lambda i: (i, 0)),
            pl.BlockSpec((None, 1, SGU_WIDTH), lambda i: (layer, 0, 0)),
            pl.BlockSpec((None, SGU_GROUPS, CHUNK, CHUNK), lambda i: (layer, 0, 0, 0)),
            pl.BlockSpec((None, CHUNK, SGU_GROUPS), lambda i: (layer, 0, 0)),
        ],
        out_specs=pl.BlockSpec((rows, SGU_WIDTH), lambda i: (i, 0)),
        out_shape=jax.ShapeDtypeStruct((n_rows, SGU_WIDTH), BF16),
        compiler_params=_params("arbitrary"),
        name="sgu",
    )(z, sgu_g.reshape(DEPTH, 1, SGU_WIDTH), sgu_w, jnp.swapaxes(sgu_b, 1, 2))


def _attn_kernel(sink_ref, q_ref, kp_ref, kc_ref, kn_ref, vp_ref, vc_ref, vn_ref, kx_ref, vx_ref,
                 o_ref, *, layer, nb, n_lat_blocks):
    s_id = pl.program_id(0)
    is_lat = s_id < n_lat_blocks
    n = s_id % nb
    n_keys = CTX_LEN + 3 * BLOCK
    n_rows = Q_GROUP * BLOCK
    row_id = lax.broadcasted_iota(jnp.int32, (n_rows, 1), 0)
    row = lax.broadcasted_iota(jnp.int32, (n_rows, n_keys), 0) & (BLOCK - 1)
    col = lax.broadcasted_iota(jnp.int32, (n_rows, n_keys), 1)
    rel = col - (CTX_LEN + BLOCK)
    lo = jnp.where(is_lat, -n * BLOCK, 4 * BLOCK)
    hi = jnp.where(is_lat, (nb - n) * BLOCK, -4 * BLOCK)
    dist = rel - row
    valid = (col < CTX_LEN) | ((dist >= -WINDOW) & (dist <= WINDOW) & (rel >= lo) & (rel < hi))
    scale = HEAD_DIM ** -0.5
    nt = (((1,), (1,)), ((), ()))
    for kh in range(N_KV_HEADS):
        ks = slice(kh * HEAD_DIM, (kh + 1) * HEAD_DIM)
        keys = jnp.concatenate([kx_ref[:, ks], kp_ref[:, ks], kc_ref[:, ks], kn_ref[:, ks]], axis=0).astype(BF16)
        vals = jnp.concatenate([vx_ref[:, ks], vp_ref[:, ks], vc_ref[:, ks], vn_ref[:, ks]], axis=0).astype(BF16)
        heads = [kh * Q_GROUP + g for g in range(Q_GROUP)]
        q = jnp.concatenate([q_ref[:, hd * HEAD_DIM:(hd + 1) * HEAD_DIM] for hd in heads], axis=0).astype(BF16)
        sink = jnp.full((n_rows, 1), sink_ref[layer, heads[0]], F32)
        for g, hd in enumerate(heads[1:], start=1):
            sink = jnp.where(row_id >= g * BLOCK, sink_ref[layer, hd], sink)
        s = lax.dot_general(q, keys, nt, preferred_element_type=F32) * scale
        s = jnp.where(valid, s, NEG_INF)
        m = jnp.maximum(jnp.max(s, axis=-1, keepdims=True), sink)
        p = jnp.exp(s - m)
        denom = jnp.sum(p, axis=-1, keepdims=True) + jnp.exp(sink - m)
        probs = (p / denom).astype(BF16)
        out = jnp.dot(probs, vals, preferred_element_type=F32).astype(o_ref.dtype)
        for g, hd in enumerate(heads):
            o_ref[:, hd * HEAD_DIM:(hd + 1) * HEAD_DIM] = out[g * BLOCK:(g + 1) * BLOCK]


def _attention(z, attn_sink, *, layer, n_rows):
    nb = SEQ // BLOCK
    n_lat_blocks = N_LAT // BLOCK
    n_blocks = n_rows // BLOCK
    ctx_blocks = CTX_LEN // BLOCK
    q_col = A_END // ATTN_WIDTH
    k_col = Q_END // KV_WIDTH
    v_col = K_END // KV_WIDTH

    def lat(s):
        return s < n_lat_blocks

    def band(off, col):
        def index(s):
            b = s // nb
            nbr = jnp.clip(s % nb + off, 0, nb - 1) + b * nb
            return (jnp.where(lat(s), nbr, s), col)
        return pl.BlockSpec((BLOCK, KV_WIDTH), index)

    def ctx(col):
        def index(s):
            b = jnp.where(lat(s), s // nb, (s - n_lat_blocks) // ctx_blocks)
            return (N_LAT // CTX_LEN + b, col)
        return pl.BlockSpec((CTX_LEN, KV_WIDTH), index)

    return pl.pallas_call(
        functools.partial(_attn_kernel, layer=layer, nb=nb, n_lat_blocks=n_lat_blocks),
        grid=(n_blocks,),
        in_specs=[
            pl.BlockSpec(memory_space=pltpu.SMEM),
            pl.BlockSpec((BLOCK, ATTN_WIDTH), lambda s: (s, q_col)),
            band(-1, k_col), band(0, k_col), band(1, k_col),
            band(-1, v_col), band(0, v_col), band(1, v_col),
            ctx(k_col), ctx(v_col),
        ],
        out_specs=pl.BlockSpec((BLOCK, ATTN_WIDTH), lambda s: (s, 0)),
        out_shape=jax.ShapeDtypeStruct((n_rows, ATTN_WIDTH), BF16),
        compiler_params=_params("arbitrary"),
        name="attn",
    )(attn_sink, z, z, z, z, z, z, z, z, z)


def _short_conv_phases(ref, w_ref, b_ref, m0, rows, n_m):
    ph = HY_PHASES
    z = [ref[pl.ds(ph * m0 + r, rows, stride=ph), :] for r in range(ph)]
    m = lax.broadcasted_iota(jnp.int32, (rows, 1), 0) + m0
    if m0 == 0:
        z_prev = jnp.where(m == 0, 0.0, pltpu.roll(z[ph - 1], 1, 0))
    else:
        z_prev = ref[pl.ds(ph * m0 - 1, rows, stride=ph), :]
    if m0 + rows == n_m:
        z_next = jnp.where(m == n_m - 1, 0.0, pltpu.roll(z[0], rows - 1, 0))
    else:
        z_next = ref[pl.ds(ph * m0 + ph, rows, stride=ph), :]
    ext = [z_prev] + z + [z_next]
    w0, w1, w2, b = w_ref[0:1, :], w_ref[1:2, :], w_ref[2:3, :], b_ref[...]
    return [ext[r] * w0 + ext[r + 1] * w1 + ext[r + 2] * w2 + b for r in range(ph)]


def _cmul(a, b):
    return a[0] * b[0] - a[1] * b[1], a[0] * b[1] + a[1] * b[0]


def _cadd(a, b):
    return a[0] + b[0], a[1] + b[1]


def _csub(a, b):
    return a[0] - b[0], a[1] - b[1]


def _conj(a):
    return a[0], -a[1]


def _split_fwd(e, o, w):
    t = _cmul(w, o)
    return _cadd(e, t), _conj(_csub(e, t))


def _split_inv(pp, pm, w):
    cm = _conj(pm)
    return _cadd(pp, cm), _cmul(_csub(pp, cm), _conj(w))


def _alt_sign(r0, rows):
    t = lax.broadcasted_iota(jnp.int32, (rows, 1), 0) + r0
    return jnp.where((t & 1) == 0, 1.0, -1.0)


def _hy_chunks(seq, chunk=None):
    rows = min(chunk or HY_ROWS, seq)
    return [(r * rows, rows) for r in range(seq // rows)]


def _filt_kernel(feat_ref, w1_ref, b1_ref, w2_ref, b2_ref, w3_ref, fr_ref, dl_ref,
                 hb_ref, asum_ref, hnyq_ref, hd_ref, *, seq):
    hp = lax.Precision.HIGHEST

    @pl.when(pl.program_id(1) == 0)
    def _():
        fr = fr_ref[...]
        hd = jnp.sin(fr * (jnp.dot(feat_ref[...], w1_ref[...], precision=hp, preferred_element_type=F32)
                           + b1_ref[...]))
        hd_ref[...] = jnp.sin(fr * (jnp.dot(hd, w2_ref[...], precision=hp, preferred_element_type=F32)
                                    + b2_ref[...]))

    q = seq // HY_PHASES
    asum = jnp.zeros((1, HY_FILT_COLS), F32)
    alt = [jnp.zeros((1, HY_FILT_COLS), F32) for _ in range(HY_PHASES)]
    for r0, rows in _hy_chunks(seq, min(HY_ROWS, q)):
        rs = slice(r0, r0 + rows)
        h = jnp.dot(hd_ref[rs, :].astype(BF16), w3_ref[...].astype(BF16), preferred_element_type=F32)
        h = h * jnp.exp(-feat_ref[rs, 0:1] * dl_ref[...])
        asum = asum + jnp.sum(jnp.abs(h), axis=0, keepdims=True)
        alt[r0 // q] = alt[r0 // q] + jnp.sum(h * _alt_sign(r0 % q, rows), axis=0, keepdims=True)
        hb_ref[rs, :] = h.astype(BF16)
    asum_ref[...] = asum
    for r in range(HY_PHASES):
        hnyq_ref[r:r + 1, :] = alt[r]


def _hyena_filters(seq, w1, b1, w2, b2, w3, freq):
    n_tiles = HY_WIDTH // HY_COLS
    w3 = w3.reshape(DEPTH, HY_HIDDEN, HY_ORDER, 2, n_tiles, HY_COLS)
    w3 = jnp.swapaxes(w3, 3, 4).reshape(DEPTH, HY_HIDDEN, HY_ORDER * 2 * HY_WIDTH)
    t = jnp.linspace(0.0, 1.0, seq, dtype=F32)[:, None]
    w = (2.0 * math.pi / seq) * jnp.arange(seq, dtype=F32)[:, None]
    bands = jnp.linspace(1e-4, HY_BANDS - 1, HY_BANDS, dtype=F32)[None, :]
    feat = jnp.concatenate([t, jnp.cos(bands * w), -jnp.sin(bands * w)], axis=-1)
    feat = jnp.pad(feat, ((0, 0), (0, HY_EMB_PAD - HY_EMB)))
    feat = jnp.concatenate([feat[r::HY_PHASES] for r in range(HY_PHASES)], axis=0)
    w1p = jnp.pad(w1, ((0, 0), (0, HY_EMB_PAD - HY_EMB), (0, 0)))
    deltas = jnp.abs(jnp.linspace(math.log(HY_TARGET) / HY_FAST_DECAY,
                                  math.log(HY_TARGET) / HY_SLOW_DECAY, HY_WIDTH, dtype=F32))
    deltas = jnp.broadcast_to(deltas.reshape(1, n_tiles, 1, HY_COLS), (HY_ORDER, n_tiles, 2, HY_COLS))
    deltas = deltas.reshape(1, HY_ORDER * 2 * HY_WIDTH)
    n_cols = HY_ORDER * 2 * HY_WIDTH

    def row(a):
        return a.reshape(DEPTH, 1, HY_HIDDEN)

    vec = pl.BlockSpec((None, 1, HY_HIDDEN), lambda l, j: (l, 0, 0))
    stat = pl.BlockSpec((None, 1, HY_FILT_COLS), lambda l, j: (l, 0, j))
    return pl.pallas_call(
        functools.partial(_filt_kernel, seq=seq),
        grid=(DEPTH, n_cols // HY_FILT_COLS),
        in_specs=[
            pl.BlockSpec((seq, HY_EMB_PAD), lambda l, j: (0, 0)),
            pl.BlockSpec((None, HY_EMB_PAD, HY_HIDDEN), lambda l, j: (l, 0, 0)),
            vec,
            pl.BlockSpec((None, HY_HIDDEN, HY_HIDDEN), lambda l, j: (l, 0, 0)),
            vec,
            pl.BlockSpec((None, HY_HIDDEN, HY_FILT_COLS), lambda l, j: (l, 0, j)),
            vec,
            pl.BlockSpec((1, HY_FILT_COLS), lambda l, j: (0, j)),
        ],
        out_specs=[pl.BlockSpec((None, seq, HY_FILT_COLS), lambda l, j: (l, 0, j)), stat,
                   pl.BlockSpec((None, HY_PHASES, HY_FILT_COLS), lambda l, j: (l, 0, j))],
        out_shape=[jax.ShapeDtypeStruct((DEPTH, seq, n_cols), BF16),
                   jax.ShapeDtypeStruct((DEPTH, 1, n_cols), F32),
                   jax.ShapeDtypeStruct((DEPTH, HY_PHASES, n_cols), F32)],
        scratch_shapes=[pltpu.VMEM((seq, HY_HIDDEN), F32)],
        compiler_params=_params("arbitrary", "arbitrary"),
        name=f"hy_filt_l{seq}",
    )(feat, w1p, row(b1), w2, row(b2), w3, row(freq), deltas)


def _dft_matrices(seq, kf):
    n_fft = 2 * seq
    n_ch = seq // kf
    th = 2.0 * math.pi / n_fft
    c = jnp.arange(n_ch, dtype=jnp.int32)[:, None]
    kl = jnp.arange(kf, dtype=jnp.int32)[:, None]
    n = jnp.arange(seq, dtype=jnp.int32)[None, :]
    a1 = th * ((c * kf * n) % n_fft).astype(F32)
    a2 = th * ((kl * n) % n_fft).astype(F32)
    c1, s1, c2, s2 = jnp.cos(a1), jnp.sin(a1), jnp.cos(a2), jnp.sin(a2)
    f_cos = c1[:, None, :] * c2[None, :, :] - s1[:, None, :] * s2[None, :, :]
    f_sin = s1[:, None, :] * c2[None, :, :] + c1[:, None, :] * s2[None, :, :]
    fmat = jnp.concatenate([f_cos, -f_sin], axis=1).astype(BF16)
    c2t, s2t = c2.T, s2.T
    g_cos = c1[:, :, None] * c2t[None, :, :] - s1[:, :, None] * s2t[None, :, :]
    g_sin = s1[:, :, None] * c2t[None, :, :] + c1[:, :, None] * s2t[None, :, :]
    gmat = jnp.concatenate([g_cos, -g_sin], axis=2).astype(BF16)
    return fmat, gmat


def _twiddle_plan(seq):
    freqs = [(0, 1)]
    plan = [None] * HY_LEVELS
    for d in reversed(range(HY_LEVELS)):
        l_d = seq >> d
        plan[d] = [(off, sgn, 2 * l_d) for off, sgn in freqs]
        freqs = [f for off, sgn in freqs for f in ((off, sgn), (l_d - off, -sgn))]
    index = 0
    for d in range(HY_LEVELS):
        plan[d] = [(index + j,) + t for j, t in enumerate(plan[d])]
        index += len(plan[d])
    return plan


def _twiddles(seq):
    k = jnp.arange(seq // HY_PHASES, dtype=F32)[:, None]
    tabs = []
    for level in _twiddle_plan(seq):
        for _, off, sgn, mod in level:
            ang = jnp.broadcast_to((2.0 * math.pi / mod) * (off + sgn * k), (seq // HY_PHASES, HEAD_DIM))
            tabs += [jnp.cos(ang), jnp.sin(ang)]
    return jnp.stack(tabs, axis=0)


def _spectrum_tree(leaves, twiddle, depth=0):
    if len(leaves) == 1:
        return leaves
    even = _spectrum_tree(leaves[0::2], twiddle, depth + 1)
    odd = _spectrum_tree(leaves[1::2], twiddle, depth + 1)
    out = []
    for j, (e, o) in enumerate(zip(even, odd)):
        out += list(_split_fwd(e, o, twiddle(depth, j)))
    return out


def _synthesis_tree(vals, twiddle, depth=0):
    if len(vals) == 1:
        return vals
    even, odd = [], []
    for j in range(len(vals) // 2):
        q0, q1 = _split_inv(vals[2 * j], vals[2 * j + 1], twiddle(depth, j))
        even.append(q0)
        odd.append(q1)
    out = [None] * len(vals)
    out[0::2] = _synthesis_tree(even, twiddle, depth + 1)
    out[1::2] = _synthesis_tree(odd, twiddle, depth + 1)
    return out


def _hy_pre_kernel(z_ref, w_ref, b_ref, ub_ref, ualt_ref, *, seq):
    q = seq // HY_PHASES
    alt = [jnp.zeros((1, HEAD_DIM), F32) for _ in range(HY_PHASES)]
    for m0, rows in _hy_chunks(q):
        v = _short_conv_phases(z_ref, w_ref, b_ref, m0, rows, q)
        sign = _alt_sign(m0, rows)
        for r in range(HY_PHASES):
            ub_ref[r * q + m0:r * q + m0 + rows, :] = v[r].astype(BF16)
            alt[r] = alt[r] + jnp.sum(v[r] * sign, axis=0, keepdims=True)
    for r in range(HY_PHASES):
        ualt_ref[r:r + 1, :] = alt[r]


def _hy_spec_kernel(u_ref, h_ref, a_ref, f_ref, g_ref, tw_ref, y_ref, *, seq, kf):
    c = pl.program_id(1)
    cw = HY_COLS
    q = seq // HY_PHASES
    n_fft = 2 * seq

    @pl.when(c == 0)
    def _():
        y_ref[...] = jnp.zeros_like(y_ref)

    f = f_ref[...]
    plan = _twiddle_plan(seq)

    def twiddles(width):
        rep = width // HEAD_DIM
        cache = {}

        def get(depth, j):
            idx = plan[depth][j][0]
            if idx not in cache:
                cache[idx] = (jnp.concatenate([tw_ref[2 * idx]] * rep, axis=1),
                              -jnp.concatenate([tw_ref[2 * idx + 1]] * rep, axis=1))
            return cache[idx]
        return get

    def spectrum(ref, width):
        leaves = []
        for r in range(HY_PHASES):
            d = jnp.dot(f, ref[r * q:(r + 1) * q, :], preferred_element_type=F32)
            leaves.append((d[:kf], d[kf:]))
        return _spectrum_tree(leaves, twiddles(width))

    row = lax.broadcasted_iota(jnp.int32, (kf, 1), 0)
    a = a_ref[...]
    wgt = jnp.where((row == 0) & (c == 0), 1.0 / n_fft, 2.0 / n_fft) / (a[:, :cw] + a[:, cw:] + 1e-6)
    filt = [((x[0][:, :cw] + x[0][:, cw:]) * wgt, (x[1][:, :cw] - x[1][:, cw:]) * wgt)
            for x in spectrum(h_ref, 2 * cw)]
    sig = spectrum(u_ref, BATCH * cw)
    tw_c = twiddles(cw)
    out = [[] for _ in range(HY_PHASES)]
    for b in range(BATCH):
        bs = slice(b * cw, (b + 1) * cw)
        p = [_cmul((x[0][:, bs], x[1][:, bs]), s) for x, s in zip(sig, filt)]
        for r, val in enumerate(_synthesis_tree(p, tw_c)):
            out[r].append(val)
    for r in range(HY_PHASES):
        spec = jnp.concatenate([jnp.concatenate([v[0] for v in out[r]], axis=1),
                                jnp.concatenate([v[1] for v in out[r]], axis=1)], axis=0).astype(BF16)
        for r0, rows in _hy_chunks(q, HY_INV_ROWS):
            y_ref[r * q + r0:r * q + r0 + rows, :] += jnp.dot(g_ref[r0:r0 + rows, :], spec,
                                                              preferred_element_type=F32)


def _edge_terms(ualt_ref, hf_ref, hb_ref, af_ref, ab_ref, seq):
    k_edge = seq // HY_PHASES
    plan = _twiddle_plan(seq)

    def twiddle(depth, j):
        _, off, sgn, mod = plan[depth][j]
        ang = 2.0 * math.pi * (off + sgn * k_edge) / mod
        return math.cos(ang), -math.sin(ang)

    def spectrum(ref):
        return _spectrum_tree([(ref[r:r + 1, :], 0.0) for r in range(HY_PHASES)], twiddle)

    sig, fwd, bwd = spectrum(ualt_ref), spectrum(hf_ref), spectrum(hb_ref)
    p = [_cmul(u, _cadd(f, _conj(b))) for u, f, b in zip(sig, fwd, bwd)]
    scale = (1.0 / (2 * seq)) / (af_ref[...] + ab_ref[...] + 1e-6)
    return [leaf[0] * scale for leaf in _synthesis_tree(p, twiddle)]


def _hy_mid_kernel(zv_ref, zx_ref, y_ref, ualt_ref, hf_ref, hb_ref, af_ref, ab_ref, wv_ref, bv_ref,
                   wx_ref, bx_ref, skip_ref, s_ref, sb_ref, salt_ref, *, seq):
    q = seq // HY_PHASES
    amp = _edge_terms(ualt_ref, hf_ref, hb_ref, af_ref, ab_ref, seq)
    alt = [jnp.zeros((1, HEAD_DIM), F32) for _ in range(HY_PHASES)]
    skip = skip_ref[...]
    for m0, rows in _hy_chunks(q):
        sign = _alt_sign(m0, rows)
        v = _short_conv_phases(zv_ref, wv_ref, bv_ref, m0, rows, q)
        x1 = _short_conv_phases(zx_ref, wx_ref, bx_ref, m0, rows, q)
        for r in range(HY_PHASES):
            rs = slice(r * q + m0, r * q + m0 + rows)
            s = x1[r] * (y_ref[rs, :] + sign * amp[r] + v[r] * skip)
            s_ref[rs, :] = s
            sb_ref[rs, :] = s.astype(BF16)
            alt[r] = alt[r] + jnp.sum(s * sign, axis=0, keepdims=True)
    for r in range(HY_PHASES):
        salt_ref[r:r + 1, :] = alt[r]


def _hy_post_kernel(zx_ref, y_ref, s_ref, salt_ref, hf_ref, hb_ref, af_ref, ab_ref, wx_ref, bx_ref,
                    skip_ref, *rest, seq):
    o_ref = rest[-1]
    q = seq // HY_PHASES
    amp = _edge_terms(salt_ref, hf_ref, hb_ref, af_ref, ab_ref, seq)
    skip = skip_ref[...]
    for m0, rows in _hy_chunks(q):
        sign = _alt_sign(m0, rows)
        x2 = _short_conv_phases(zx_ref, wx_ref, bx_ref, m0, rows, q)
        for r in range(HY_PHASES):
            rs = slice(r * q + m0, r * q + m0 + rows)
            y = y_ref[rs, :] + sign * amp[r] + s_ref[rs, :] * skip
            o_ref[pl.ds(HY_PHASES * m0 + r, rows, stride=HY_PHASES), :] = x2[r] * y


def _hyena(z, z_rb0, conv_w, conv_b, skip, filt, fmat, gmat, tw, *, layer, seq, out_rows, out_rb0,
           alias_out=None):
    hb, asum, halt = filt
    q = seq // HY_PHASES
    kf = min(HY_KF, q)
    n_ch = q // kf
    n_cols = HY_WIDTH // HY_COLS
    ew = HEAD_DIM
    sub = HY_COLS // ew
    n_ew = HY_WIDTH // ew
    hy0 = V_END // ew

    def zcols(which):
        return pl.BlockSpec((seq, ew), lambda b, e: (z_rb0 + b, hy0 + which * n_ew + e))

    def cw(which):
        return pl.BlockSpec((None, HY_SHORT, ew), lambda b, e: (layer, 0, which * n_ew + e))

    def cb(which):
        return pl.BlockSpec((None, 1, ew), lambda b, e: (layer, 0, which * n_ew + e))

    def fstat(order, direction, rows):
        return pl.BlockSpec((None, rows, ew),
                            lambda b, e: (layer, 0, ((order * n_cols + e // sub) * 2 + direction) * sub + e % sub))

    def skip_spec(order):
        return pl.BlockSpec((None, 1, ew), lambda b, e: (layer * HY_ORDER + order, 0, e))

    def act_col(b, e):
        return ((e // sub) * BATCH + b) * sub + e % sub

    seq_blk = pl.BlockSpec((seq, ew), lambda b, e: (0, act_col(b, e)))
    alt_spec = pl.BlockSpec((HY_PHASES, ew), lambda b, e: (0, act_col(b, e)))
    ew_params = _params("arbitrary", "arbitrary")
    act = jax.ShapeDtypeStruct((seq, BATCH * HY_WIDTH), F32)
    act_b = jax.ShapeDtypeStruct((seq, BATCH * HY_WIDTH), BF16)
    alt_shape = jax.ShapeDtypeStruct((HY_PHASES, BATCH * HY_WIDTH), F32)

    def spectral(ub, order):
        return pl.pallas_call(
            functools.partial(_hy_spec_kernel, seq=seq, kf=kf),
            grid=(n_cols, n_ch),
            in_specs=[
                pl.BlockSpec((seq, BATCH * HY_COLS), lambda j, c: (0, j)),
                pl.BlockSpec((None, seq, 2 * HY_COLS), lambda j, c: (layer, 0, order * n_cols + j),
                             pipeline_mode=pl.Buffered(1)),
                pl.BlockSpec((None, 1, 2 * HY_COLS), lambda j, c: (layer, 0, order * n_cols + j)),
                pl.BlockSpec((None, 2 * kf, q), lambda j, c: (c, 0, 0)),
                pl.BlockSpec((None, q, 2 * kf), lambda j, c: (c, 0, 0)),
                pl.BlockSpec((tw.shape[0], kf, HEAD_DIM), lambda j, c: (0, c, 0)),
            ],
            out_specs=pl.BlockSpec((seq, BATCH * HY_COLS), lambda j, c: (0, j)),
            out_shape=act,
            compiler_params=_params("arbitrary", "arbitrary"),
            name=f"hy_spec{order}_l{seq}",
        )(ub, hb, asum, fmat, gmat, tw)

    ub, ualt = pl.pallas_call(
        functools.partial(_hy_pre_kernel, seq=seq),
        grid=(BATCH, n_ew),
        in_specs=[zcols(0), cw(0), cb(0)],
        out_specs=[seq_blk, alt_spec],
        out_shape=[act_b, alt_shape],
        compiler_params=ew_params,
        name=f"hy_pre_l{seq}",
    )(z, conv_w, conv_b)
    y0 = spectral(ub, 0)
    s, sb, salt = pl.pallas_call(
        functools.partial(_hy_mid_kernel, seq=seq),
        grid=(BATCH, n_ew),
        in_specs=[zcols(0), zcols(1), seq_blk, alt_spec, fstat(0, 0, HY_PHASES), fstat(0, 1, HY_PHASES), fstat(0, 0, 1),
                  fstat(0, 1, 1), cw(0), cb(0), cw(1), cb(1), skip_spec(0)],
        out_specs=[seq_blk, seq_blk, alt_spec],
        out_shape=[act, act_b, alt_shape],
        compiler_params=ew_params,
        name=f"hy_mid_l{seq}",
    )(z, z, y0, ualt, halt, halt, asum, asum, conv_w, conv_b, conv_w, conv_b, skip)
    y1 = spectral(sb, 1)
    in_specs = [zcols(2), seq_blk, seq_blk, alt_spec, fstat(1, 0, HY_PHASES), fstat(1, 1, HY_PHASES), fstat(1, 0, 1),
                fstat(1, 1, 1), cw(2), cb(2), skip_spec(1)]
    args = [z, y1, s, salt, halt, halt, asum, asum, conv_w, conv_b, skip]
    aliases = {}
    if alias_out is not None:
        in_specs.append(pl.BlockSpec(memory_space=pl.ANY))
        args.append(alias_out)
        aliases = {len(args) - 1: 0}
    return pl.pallas_call(
        functools.partial(_hy_post_kernel, seq=seq),
        grid=(BATCH, n_ew),
        in_specs=in_specs,
        out_specs=pl.BlockSpec((seq, ew), lambda b, e: (out_rb0 + b, e)),
        out_shape=jax.ShapeDtypeStruct((out_rows, HY_WIDTH), F32),
        input_output_aliases=aliases,
        compiler_params=ew_params,
        name=f"hy_post_l{seq}",
    )(*args)


def _outproj_kernel(a_ref, b_ref, c_ref, w_ref, x_ref, mod_ref, g_ref, o_ref):
    y = jnp.dot(a_ref[...], w_ref[:SGU_WIDTH, :], preferred_element_type=F32)
    y += jnp.dot(b_ref[...], w_ref[SGU_WIDTH:SGU_WIDTH + ATTN_WIDTH, :], preferred_element_type=F32)
    y += jnp.dot(c_ref[...].astype(BF16), w_ref[SGU_WIDTH + ATTN_WIDTH:, :], preferred_element_type=F32)
    o_ref[...] = x_ref[...] + mod_ref[5:6, :] * _rms(y, g_ref[3:4, :])


def _outproj(a, b, c, w_out, x, mods, norm_g, *, layer, n_tiles):
    return pl.pallas_call(
        _outproj_kernel,
        grid=(n_tiles,),
        in_specs=[
            pl.BlockSpec((TM, SGU_WIDTH), lambda i: (i, 0)),
            pl.BlockSpec((TM, ATTN_WIDTH), lambda i: (i, 0)),
            pl.BlockSpec((TM, HY_WIDTH), lambda i: (i, 0)),
            pl.BlockSpec((D_MIX, D_MODEL), lambda i: (0, 0), pipeline_mode=pl.Buffered(1)),
            pl.BlockSpec((TM, D_MODEL), lambda i: (i, 0)),
            pl.BlockSpec((None, None, N_MOD, D_MODEL), lambda i: (layer, _mod_row(i), 0, 0)),
            pl.BlockSpec((None, 6, D_MODEL), lambda i: (layer, 0, 0)),
        ],
        out_specs=pl.BlockSpec((TM, D_MODEL), lambda i: (i, 0)),
        out_shape=jax.ShapeDtypeStruct((n_tiles * TM, D_MODEL), F32),
        compiler_params=_params("arbitrary"),
        name="outproj",
    )(a, b, c, w_out, x, mods, norm_g)


def _rope_tables():
    rows = SEQ // GRID_W
    row = jnp.broadcast_to(jnp.arange(rows)[:, None], (rows, GRID_W)).reshape(SEQ).astype(F32)
    col = jnp.broadcast_to(jnp.arange(GRID_W)[None, :], (rows, GRID_W)).reshape(SEQ).astype(F32)
    half = HEAD_DIM // 2
    inv = ROPE_BASE ** (-jnp.arange(0, half, 2, dtype=F32) / half)
    ang_r = row[:, None] * inv
    ang_c = col[:, None] * inv
    ang = jnp.concatenate([ang_r, ang_r, ang_c, ang_c], axis=-1)
    cos, sin = jnp.cos(ang), jnp.sin(ang)
    first = (jnp.arange(HEAD_DIM) % half) < (half // 2)
    sa = jnp.where(first, -sin, 0.0)
    sb = jnp.where(first, 0.0, sin)

    def stack(lat, ctx_value):
        return jnp.concatenate([lat] * BATCH + [jnp.full((N_CTX, HEAD_DIM), ctx_value, F32)], axis=0)

    return stack(cos, 1.0), stack(sa, 0.0), stack(sb, 0.0)


def kernel(x, c, ctx, c_ctx, w_ada, b_ada, norm_g, ffn1_gate, ffn1_up, ffn1_down, w_in,
           sgu_g, sgu_w, sgu_b, attn_sink, hy_conv_w, hy_conv_b, hy_f_w1, hy_f_b1, hy_f_w2,
           hy_f_b2, hy_f_w3, hy_freq, hy_skip, w_out, ffn2_gate, ffn2_up, ffn2_down):
    xs = jnp.concatenate([x.reshape(N_LAT, D_MODEL), ctx.reshape(N_CTX, D_MODEL)], axis=0)
    cc = jnp.concatenate([c, c_ctx[None], jnp.zeros((MOD_ROWS - BATCH - 1, D_MODEL), F32)], axis=0)
    mods = _ada_mods(cc, w_ada, b_ada)

    w1 = [w[0].astype(BF16) for w in (ffn1_gate, ffn1_up, ffn1_down)]
    cos, sa, sb = _rope_tables()
    conv_b = hy_conv_b.reshape(DEPTH, 1, (HY_ORDER + 1) * HY_WIDTH)
    skip = hy_skip.reshape(DEPTH * HY_ORDER, 1, HY_WIDTH)
    filt_w = (hy_f_w1, hy_f_b1, hy_f_w2, hy_f_b2, hy_f_w3, hy_freq)
    filt_lat = _hyena_filters(SEQ, *filt_w)
    filt_ctx = _hyena_filters(CTX_LEN, *filt_w)
    f_lat, g_lat = _dft_matrices(SEQ // HY_PHASES, min(HY_KF, SEQ // HY_PHASES))
    f_ctx, g_ctx = _dft_matrices(CTX_LEN // HY_PHASES, min(HY_KF, CTX_LEN // HY_PHASES))
    tw_lat, tw_ctx = _twiddles(SEQ), _twiddles(CTX_LEN)

    all_tiles = N_TOK // TM
    lat_tiles = N_LAT // TM
    ctx_rb = N_LAT // CTX_LEN

    for i in range(DEPTH):
        last = i == DEPTH - 1
        n_tiles = lat_tiles if last else all_tiles
        n_rows = n_tiles * TM

        jobs = [(w, i) for w in (ffn2_gate, ffn2_up, ffn2_down, w_in, w_out)]
        xs, (w2g, w2u, w2d, w_in_b, w_out_b) = _ffn(xs, mods, norm_g, *w1, jobs, layer=i, k=0, n_tiles=all_tiles)
        z = _inproj(xs, mods, norm_g, w_in_b, cos, sa, sb, layer=i)

        a_out = _sgu(z, sgu_g, sgu_w, sgu_b, layer=i, n_rows=n_rows)
        b_out = _attention(z, attn_sink, layer=i, n_rows=n_rows)
        c_out = _hyena(z, 0, hy_conv_w, conv_b, skip, filt_lat, f_lat, g_lat, tw_lat, layer=i, seq=SEQ,
                       out_rows=n_rows, out_rb0=0)
        if not last:
            c_out = _hyena(z, ctx_rb, hy_conv_w, conv_b, skip, filt_ctx, f_ctx, g_ctx, tw_ctx, layer=i,
                           seq=CTX_LEN, out_rows=n_rows, out_rb0=ctx_rb, alias_out=c_out)

        xs = _outproj(a_out, b_out, c_out, w_out_b, xs, mods, norm_g, layer=i, n_tiles=n_tiles)
        jobs = [] if last else [(w, i + 1) for w in (ffn1_gate, ffn1_up, ffn1_down)]
        xs, w1 = _ffn(xs, mods, norm_g, w2g, w2u, w2d, jobs, layer=i, k=2, n_tiles=n_tiles)

    return xs[:N_LAT].reshape(BATCH, SEQ, D_MODEL)
```

```python
import functools
import math

import jax
import jax.numpy as jnp
from jax import lax
from jax.experimental import pallas as pl
from jax.experimental.pallas import tpu as pltpu

D_MODEL = 2048
BATCH = 2
SEQ = 4096
DEPTH = 4
GRID_W = 64
CTX_LEN = 256
HEAD_DIM = 128
NORM_EPS = 1e-6
N_MOD = 9
D_FF = 5632
SGU_GROUPS = 4
SGU_WIDTH = SGU_GROUPS * HEAD_DIM
CHUNK = 128
N_Q_HEADS = 8
N_KV_HEADS = 2
Q_GROUP = N_Q_HEADS // N_KV_HEADS
ATTN_WIDTH = N_Q_HEADS * HEAD_DIM
KV_WIDTH = N_KV_HEADS * HEAD_DIM
WINDOW = 128
BLOCK = 128
ROPE_BASE = 10000.0
NEG_INF = -1e30
HY_GROUPS = 4
HY_WIDTH = HY_GROUPS * HEAD_DIM
HY_ORDER = 2
HY_SHORT = 3
HY_BANDS = 16
HY_EMB = 1 + 2 * HY_BANDS
HY_HIDDEN = 64
HY_TARGET = 1e-2
HY_FAST_DECAY = 0.3
HY_SLOW_DECAY = 1.5
D_MIX = SGU_WIDTH + ATTN_WIDTH + HY_WIDTH
A_END = 2 * SGU_WIDTH
Q_END = A_END + ATTN_WIDTH
K_END = Q_END + KV_WIDTH
V_END = K_END + KV_WIDTH
IN_COLS = V_END + (HY_ORDER + 1) * HY_WIDTH

N_LAT = BATCH * SEQ
N_CTX = BATCH * CTX_LEN
N_TOK = N_LAT + N_CTX
MOD_ROWS = 8

TM = 512
TF = 512
FFN_ROWS = 32
BF16_SUBLANES = 16
TN_IN = 1024
TN_ADA = 1024
HY_COLS = 256
HY_FILT_COLS = 2 * HY_COLS
HY_KF = 256
HY_LEVELS = 3
HY_PHASES = 2 ** HY_LEVELS
HY_ROWS = 512
HY_INV_ROWS = 1024
HY_EMB_PAD = 128
VMEM_LIMIT = 56 * 1024 * 1024

F32 = jnp.float32
BF16 = jnp.bfloat16


def _params(*sem):
    return pltpu.CompilerParams(dimension_semantics=sem, vmem_limit_bytes=VMEM_LIMIT)


def _mod_row(i):
    return jnp.minimum(i // (SEQ // TM), BATCH)


def _silu(a):
    return a / (1.0 + jnp.exp(-a))


def _rms(x, g):
    return x * lax.rsqrt(jnp.mean(x * x, axis=-1, keepdims=True) + NORM_EPS) * g


def _ada_kernel(c_ref, w_ref, b_ref, o_ref):
    s = _silu(c_ref[...]).astype(BF16)
    o_ref[...] = jnp.dot(s, w_ref[...].astype(BF16), preferred_element_type=F32) + b_ref[...]


def _ada_mods(cc, w_ada, b_ada):
    n_out = N_MOD * D_MODEL
    out = pl.pallas_call(
        _ada_kernel,
        grid=(DEPTH, n_out // TN_ADA),
        in_specs=[
            pl.BlockSpec((MOD_ROWS, D_MODEL), lambda l, j: (0, 0)),
            pl.BlockSpec((None, D_MODEL, TN_ADA), lambda l, j: (l, 0, j)),
            pl.BlockSpec((None, 1, TN_ADA), lambda l, j: (l, 0, j)),
        ],
        out_specs=pl.BlockSpec((None, MOD_ROWS, TN_ADA), lambda l, j: (l, 0, j)),
        out_shape=jax.ShapeDtypeStruct((DEPTH, MOD_ROWS, n_out), F32),
        compiler_params=_params("arbitrary", "arbitrary"),
        name="ada_mods",
    )(cc, w_ada, b_ada.reshape(DEPTH, 1, n_out))
    return out.reshape(DEPTH, MOD_ROWS, N_MOD, D_MODEL)


def _ffn_kernel(*refs, k, nf, n_jobs):
    x_ref, mod_ref, g_ref, wg_ref, wu_ref, wd_ref = refs[:6]
    cast_src = refs[6:6 + n_jobs]
    o_ref = refs[6 + n_jobs]
    cast_dst = refs[7 + n_jobs:7 + 2 * n_jobs]
    h_ref, acc_ref = refs[7 + 2 * n_jobs:]
    f = pl.program_id(1)

    for src, dst in zip(cast_src, cast_dst):
        dst[...] = src[...].astype(BF16)

    @pl.when(f == 0)
    def _():
        g = g_ref[2 * k:2 * k + 1, :]
        m1 = 1.0 + mod_ref[3 * k:3 * k + 1, :]
        m2 = mod_ref[3 * k + 1:3 * k + 2, :]
        for r0 in range(0, TM, FFN_ROWS):
            rs = slice(r0, r0 + FFN_ROWS)
            h_ref[rs, :] = (_rms(x_ref[rs, :], g) * m1 + m2).astype(BF16)
        acc_ref[...] = jnp.zeros_like(acc_ref)

    h = h_ref[...]
    a = jnp.dot(h, wg_ref[...], preferred_element_type=F32)
    b = jnp.dot(h, wu_ref[...], preferred_element_type=F32)
    p = (_silu(a) * b).astype(BF16)
    acc_ref[...] += jnp.dot(p, wd_ref[...], preferred_element_type=F32)

    @pl.when(f == nf - 1)
    def _():
        g = g_ref[2 * k + 1:2 * k + 2, :]
        gate = 0.5 * mod_ref[3 * k + 2:3 * k + 3, :]
        for r0 in range(0, TM, FFN_ROWS):
            rs = slice(r0, r0 + FFN_ROWS)
            o_ref[rs, :] = x_ref[rs, :] + gate * _rms(acc_ref[rs, :], g)


def _ffn(x, mods, norm_g, wg, wu, wd, cast_jobs, *, layer, k, n_tiles):
    nf = D_FF // TF
    n_steps = n_tiles * nf
    in_specs = [
        pl.BlockSpec((TM, D_MODEL), lambda i, f: (i, 0)),
        pl.BlockSpec((None, None, N_MOD, D_MODEL), lambda i, f: (layer, _mod_row(i), 0, 0)),
        pl.BlockSpec((None, 6, D_MODEL), lambda i, f: (layer, 0, 0)),
        pl.BlockSpec((D_MODEL, TF), lambda i, f: (0, f)),
        pl.BlockSpec((D_MODEL, TF), lambda i, f: (0, f)),
        pl.BlockSpec((TF, D_MODEL), lambda i, f: (f, 0)),
    ]
    out_specs = [pl.BlockSpec((TM, D_MODEL), lambda i, f: (i, 0))]
    out_shape = [jax.ShapeDtypeStruct((n_tiles * TM, D_MODEL), F32)]
    for w, src_layer in cast_jobs:
        n_rows, n_cols = w.shape[1:]
        rb = BF16_SUBLANES * pl.cdiv(pl.cdiv(n_rows, n_steps), BF16_SUBLANES)
        assert n_rows % rb == 0
        last_blk = n_rows // rb - 1

        def step_block(i, f, last_blk=last_blk):
            return jnp.minimum(i * nf + f, last_blk)

        in_specs.append(pl.BlockSpec((None, rb, n_cols),
                                     lambda i, f, sl=src_layer, sb=step_block: (sl, sb(i, f), 0)))
        out_specs.append(pl.BlockSpec((rb, n_cols), lambda i, f, sb=step_block: (sb(i, f), 0)))
        out_shape.append(jax.ShapeDtypeStruct((n_rows, n_cols), BF16))
    outs = pl.pallas_call(
        functools.partial(_ffn_kernel, k=k, nf=nf, n_jobs=len(cast_jobs)),
        grid=(n_tiles, nf),
        in_specs=in_specs,
        out_specs=out_specs,
        out_shape=out_shape,
        scratch_shapes=[pltpu.VMEM((TM, D_MODEL), BF16), pltpu.VMEM((TM, D_MODEL), F32)],
        compiler_params=_params("arbitrary", "arbitrary"),
        name=f"ffn{k}",
    )(x, mods, norm_g, wg, wu, wd, *[w for w, _ in cast_jobs])
    return outs[0], list(outs[1:])


def _rope(t, cos, sa, sb):
    return t * cos + pltpu.roll(t, 3 * HEAD_DIM // 4, 1) * sa + pltpu.roll(t, HEAD_DIM // 4, 1) * sb


def _inproj_kernel(x_ref, mod_ref, g_ref, w_ref, cos_ref, sa_ref, sb_ref, o_ref):
    y = _rms(x_ref[...], g_ref[2:3, :])
    h = (y * (1.0 + mod_ref[3:4, :]) + mod_ref[4:5, :]).astype(BF16)
    cos, sa, sb = cos_ref[...], sa_ref[...], sb_ref[...]
    n_rope = (K_END - A_END) // HEAD_DIM
    for n in range(IN_COLS // TN_IN):
        c0 = n * TN_IN
        z = jnp.dot(h, w_ref[:, c0:c0 + TN_IN], preferred_element_type=F32)
        for hd in range(TN_IN // HEAD_DIM):
            col = c0 + hd * HEAD_DIM
            zs = z[:, hd * HEAD_DIM:(hd + 1) * HEAD_DIM]
            if A_END <= col < A_END + n_rope * HEAD_DIM:
                zs = _rope(zs, cos, sa, sb)
            o_ref[:, col:col + HEAD_DIM] = zs


def _inproj(x, mods, norm_g, w_in, cos, sa, sb, *, layer):
    n_tiles = N_TOK // TM
    tab = pl.BlockSpec((TM, HEAD_DIM), lambda i: (i, 0))
    return pl.pallas_call(
        _inproj_kernel,
        grid=(n_tiles,),
        in_specs=[
            pl.BlockSpec((TM, D_MODEL), lambda i: (i, 0)),
            pl.BlockSpec((None, None, N_MOD, D_MODEL), lambda i: (layer, _mod_row(i), 0, 0)),
            pl.BlockSpec((None, 6, D_MODEL), lambda i: (layer, 0, 0)),
            pl.BlockSpec((D_MODEL, IN_COLS), lambda i: (0, 0), pipeline_mode=pl.Buffered(1)),
            tab, tab, tab,
        ],
        out_specs=pl.BlockSpec((TM, IN_COLS), lambda i: (i, 0)),
        out_shape=jax.ShapeDtypeStruct((N_TOK, IN_COLS), F32),
        compiler_params=_params("arbitrary"),
        name="inproj",
    )(x, mods, norm_g, w_in, cos, sa, sb)


def _sgu_kernel(z_ref, g_ref, w_ref, b_ref, o_ref, *, rows):
    inv_sqrt2 = 1.0 / math.sqrt(2.0)
    for c in range(rows // CHUNK):
        rs = slice(c * CHUNK, (c + 1) * CHUNK)
        a = z_ref[rs, :]
        a = a * (lax.erf(a * inv_sqrt2) + 1.0) * 0.5
        u, v = a[:, :SGU_WIDTH], a[:, SGU_WIDTH:]
        mu = jnp.mean(v, axis=-1, keepdims=True)
        vc = v - mu
        var = jnp.mean(vc * vc, axis=-1, keepdims=True)
        v = (vc * lax.rsqrt(var + NORM_EPS) * g_ref[...]).astype(BF16)
        for g in range(SGU_GROUPS):
            cs = slice(g * HEAD_DIM, (g + 1) * HEAD_DIM)
            mixed = jnp.dot(w_ref[g].astype(BF16), v[:, cs], preferred_element_type=F32) + b_ref[:, g:g + 1]
            o_ref[rs, cs] = (u[:, cs] * mixed).astype(o_ref.dtype)


def _sgu(z, sgu_g, sgu_w, sgu_b, *, layer, n_rows):
    rows = TM
    return pl.pallas_call(
        functools.partial(_sgu_kernel, rows=rows),
        grid=(n_rows // rows,),
        in_specs=[
            pl.BlockSpec((rows, A_END), lambda i: (i, 0)),
            pl.BlockSpec((None, 1, SGU_WIDTH), lambda i: (layer, 0, 0)),
            pl.BlockSpec((None, SGU_GROUPS, CHUNK, CHUNK), lambda i: (layer, 0, 0, 0)),
            pl.BlockSpec((None, CHUNK, SGU_GROUPS), lambda i: (layer, 0, 0)),
        ],
        out_specs=pl.BlockSpec((rows, SGU_WIDTH), lambda i: (i, 0)),
        out_shape=jax.ShapeDtypeStruct((n_rows, SGU_WIDTH), BF16),
        compiler_params=_params("arbitrary"),
        name="sgu",
    )(z, sgu_g.reshape(DEPTH, 1, SGU_WIDTH), sgu_w, jnp.swapaxes(sgu_b, 1, 2))


def _attn_kernel(sink_ref, q_ref, kp_ref, kc_ref, kn_ref, vp_ref, vc_ref, vn_ref, kx_ref, vx_ref,
                 o_ref, *, layer, nb, n_lat_blocks):
    s_id = pl.program_id(0)
    is_lat = s_id < n_lat_blocks
    n = s_id % nb
    n_keys = CTX_LEN + 3 * BLOCK
    n_rows = Q_GROUP * BLOCK
    row_id = lax.broadcasted_iota(jnp.int32, (n_rows, 1), 0)
    row = lax.broadcasted_iota(jnp.int32, (n_rows, n_keys), 0) & (BLOCK - 1)
    col = lax.broadcasted_iota(jnp.int32, (n_rows, n_keys), 1)
    rel = col - (CTX_LEN + BLOCK)
    lo = jnp.where(is_lat, -n * BLOCK, 4 * BLOCK)
    hi = jnp.where(is_lat, (nb - n) * BLOCK, -4 * BLOCK)
    dist = rel - row
    valid = (col < CTX_LEN) | ((dist >= -WINDOW) & (dist <= WINDOW) & (rel >= lo) & (rel < hi))
    scale = HEAD_DIM ** -0.5
    nt = (((1,), (1,)), ((), ()))
    for kh in range(N_KV_HEADS):
        ks = slice(kh * HEAD_DIM, (kh + 1) * HEAD_DIM)
        keys = jnp.concatenate([kx_ref[:, ks], kp_ref[:, ks], kc_ref[:, ks], kn_ref[:, ks]], axis=0).astype(BF16)
        vals = jnp.concatenate([vx_ref[:, ks], vp_ref[:, ks], vc_ref[:, ks], vn_ref[:, ks]], axis=0).astype(BF16)
        heads = [kh * Q_GROUP + g for g in range(Q_GROUP)]
        q = jnp.concatenate([q_ref[:, hd * HEAD_DIM:(hd + 1) * HEAD_DIM] for hd in heads], axis=0).astype(BF16)
        sink = jnp.full((n_rows, 1), sink_ref[layer, heads[0]], F32)
        for g, hd in enumerate(heads[1:], start=1):
            sink = jnp.where(row_id >= g * BLOCK, sink_ref[layer, hd], sink)
        s = lax.dot_general(q, keys, nt, preferred_element_type=F32) * scale
        s = jnp.where(valid, s, NEG_INF)
        m = jnp.maximum(jnp.max(s, axis=-1, keepdims=True), sink)
        p = jnp.exp(s - m)
        denom = jnp.sum(p, axis=-1, keepdims=True) + jnp.exp(sink - m)
        probs = (p / denom).astype(BF16)
        out = jnp.dot(probs, vals, preferred_element_type=F32).astype(o_ref.dtype)
        for g, hd in enumerate(heads):
            o_ref[:, hd * HEAD_DIM:(hd + 1) * HEAD_DIM] = out[g * BLOCK:(g + 1) * BLOCK]


def _attention(z, attn_sink, *, layer, n_rows):
    nb = SEQ // BLOCK
    n_lat_blocks = N_LAT // BLOCK
    n_blocks = n_rows // BLOCK
    ctx_blocks = CTX_LEN // BLOCK
    q_col = A_END // ATTN_WIDTH
    k_col = Q_END // KV_WIDTH
    v_col = K_END // KV_WIDTH

    def lat(s):
        return s < n_lat_blocks

    def band(off, col):
        def index(s):
            b = s // nb
            nbr = jnp.clip(s % nb + off, 0, nb - 1) + b * nb
            return (jnp.where(lat(s), nbr, s), col)
        return pl.BlockSpec((BLOCK, KV_WIDTH), index)

    def ctx(col):
        def index(s):
            b = jnp.where(lat(s), s // nb, (s - n_lat_blocks) // ctx_blocks)
            return (N_LAT // CTX_LEN + b, col)
        return pl.BlockSpec((CTX_LEN, KV_WIDTH), index)

    return pl.pallas_call(
        functools.partial(_attn_kernel, layer=layer, nb=nb, n_lat_blocks=n_lat_blocks),
        grid=(n_blocks,),
        in_specs=[
            pl.BlockSpec(memory_space=pltpu.SMEM),
            pl.BlockSpec((BLOCK, ATTN_WIDTH), lambda s: (s, q_col)),
            band(-1, k_col), band(0, k_col), band(1, k_col),
            band(-1, v_col), band(0, v_col), band(1, v_col),
            ctx(k_col), ctx(v_col),
        ],
        out_specs=pl.BlockSpec((BLOCK, ATTN_WIDTH), lambda s: (s, 0)),
        out_shape=jax.ShapeDtypeStruct((n_rows, ATTN_WIDTH), BF16),
        compiler_params=_params("arbitrary"),
        name="attn",
    )(attn_sink, z, z, z, z, z, z, z, z, z)


def _short_conv_phases(ref, w_ref, b_ref, m0, rows, n_m):
    ph = HY_PHASES
    z = [ref[pl.ds(ph * m0 + r, rows, stride=ph), :] for r in range(ph)]
    m = lax.broadcasted_iota(jnp.int32, (rows, 1), 0) + m0
    if m0 == 0:
        z_prev = jnp.where(m == 0, 0.0, pltpu.roll(z[ph - 1], 1, 0))
    else:
        z_prev = ref[pl.ds(ph * m0 - 1, rows, stride=ph), :]
    if m0 + rows == n_m:
        z_next = jnp.where(m == n_m - 1, 0.0, pltpu.roll(z[0], rows - 1, 0))
    else:
        z_next = ref[pl.ds(ph * m0 + ph, rows, stride=ph), :]
    ext = [z_prev] + z + [z_next]
    w0, w1, w2, b = w_ref[0:1, :], w_ref[1:2, :], w_ref[2:3, :], b_ref[...]
    return [ext[r] * w0 + ext[r + 1] * w1 + ext[r + 2] * w2 + b for r in range(ph)]


def _cmul(a, b):
    return a[0] * b[0] - a[1] * b[1], a[0] * b[1] + a[1] * b[0]


def _cadd(a, b):
    return a[0] + b[0], a[1] + b[1]


def _csub(a, b):
    return a[0] - b[0], a[1] - b[1]


def _conj(a):
    return a[0], -a[1]


def _split_fwd(e, o, w):
    t = _cmul(w, o)
    return _cadd(e, t), _conj(_csub(e, t))


def _split_inv(pp, pm, w):
    cm = _conj(pm)
    return _cadd(pp, cm), _cmul(_csub(pp, cm), _conj(w))


def _alt_sign(r0, rows):
    t = lax.broadcasted_iota(jnp.int32, (rows, 1), 0) + r0
    return jnp.where((t & 1) == 0, 1.0, -1.0)


def _hy_chunks(seq, chunk=None):
    rows = min(chunk or HY_ROWS, seq)
    return [(r * rows, rows) for r in range(seq // rows)]


def _filt_kernel(feat_ref, w1_ref, b1_ref, w2_ref, b2_ref, w3_ref, fr_ref, dl_ref,
                 hb_ref, asum_ref, hnyq_ref, hd_ref, *, seq):
    hp = lax.Precision.HIGHEST

    @pl.when(pl.program_id(1) == 0)
    def _():
        fr = fr_ref[...]
        hd = jnp.sin(fr * (jnp.dot(feat_ref[...], w1_ref[...], precision=hp, preferred_element_type=F32)
                           + b1_ref[...]))
        hd_ref[...] = jnp.sin(fr * (jnp.dot(hd, w2_ref[...], precision=hp, preferred_element_type=F32)
                                    + b2_ref[...]))

    q = seq // HY_PHASES
    asum = jnp.zeros((1, HY_FILT_COLS), F32)
    alt = [jnp.zeros((1, HY_FILT_COLS), F32) for _ in range(HY_PHASES)]
    for r0, rows in _hy_chunks(seq, min(HY_ROWS, q)):
        rs = slice(r0, r0 + rows)
        h = jnp.dot(hd_ref[rs, :].astype(BF16), w3_ref[...].astype(BF16), preferred_element_type=F32)
        h = h * jnp.exp(-feat_ref[rs, 0:1] * dl_ref[...])
        asum = asum + jnp.sum(jnp.abs(h), axis=0, keepdims=True)
        alt[r0 // q] = alt[r0 // q] + jnp.sum(h * _alt_sign(r0 % q, rows), axis=0, keepdims=True)
        hb_ref[rs, :] = h.astype(BF16)
    asum_ref[...] = asum
    for r in range(HY_PHASES):
        hnyq_ref[r:r + 1, :] = alt[r]


def _hyena_filters(seq, w1, b1, w2, b2, w3, freq):
    n_tiles = HY_WIDTH // HY_COLS
    w3 = w3.reshape(DEPTH, HY_HIDDEN, HY_ORDER, 2, n_tiles, HY_COLS)
    w3 = jnp.swapaxes(w3, 3, 4).reshape(DEPTH, HY_HIDDEN, HY_ORDER * 2 * HY_WIDTH)
    t = jnp.linspace(0.0, 1.0, seq, dtype=F32)[:, None]
    w = (2.0 * math.pi / seq) * jnp.arange(seq, dtype=F32)[:, None]
    bands = jnp.linspace(1e-4, HY_BANDS - 1, HY_BANDS, dtype=F32)[None, :]
    feat = jnp.concatenate([t, jnp.cos(bands * w), -jnp.sin(bands * w)], axis=-1)
    feat = jnp.pad(feat, ((0, 0), (0, HY_EMB_PAD - HY_EMB)))
    feat = jnp.concatenate([feat[r::HY_PHASES] for r in range(HY_PHASES)], axis=0)
    w1p = jnp.pad(w1, ((0, 0), (0, HY_EMB_PAD - HY_EMB), (0, 0)))
    deltas = jnp.abs(jnp.linspace(math.log(HY_TARGET) / HY_FAST_DECAY,
                                  math.log(HY_TARGET) / HY_SLOW_DECAY, HY_WIDTH, dtype=F32))
    deltas = jnp.broadcast_to(deltas.reshape(1, n_tiles, 1, HY_COLS), (HY_ORDER, n_tiles, 2, HY_COLS))
    deltas = deltas.reshape(1, HY_ORDER * 2 * HY_WIDTH)
    n_cols = HY_ORDER * 2 * HY_WIDTH

    def row(a):
        return a.reshape(DEPTH, 1, HY_HIDDEN)

    vec = pl.BlockSpec((None, 1, HY_HIDDEN), lambda l, j: (l, 0, 0))
    stat = pl.BlockSpec((None, 1, HY_FILT_COLS), lambda l, j: (l, 0, j))
    return pl.pallas_call(
        functools.partial(_filt_kernel, seq=seq),
        grid=(DEPTH, n_cols // HY_FILT_COLS),
        in_specs=[
            pl.BlockSpec((seq, HY_EMB_PAD), lambda l, j: (0, 0)),
            pl.BlockSpec((None, HY_EMB_PAD, HY_HIDDEN), lambda l, j: (l, 0, 0)),
            vec,
            pl.BlockSpec((None, HY_HIDDEN, HY_HIDDEN), lambda l, j: (l, 0, 0)),
            vec,
            pl.BlockSpec((None, HY_HIDDEN, HY_FILT_COLS), lambda l, j: (l, 0, j)),
            vec,
            pl.BlockSpec((1, HY_FILT_COLS), lambda l, j: (0, j)),
        ],
        out_specs=[pl.BlockSpec((None, seq, HY_FILT_COLS), lambda l, j: (l, 0, j)), stat,
                   pl.BlockSpec((None, HY_PHASES, HY_FILT_COLS), lambda l, j: (l, 0, j))],
        out_shape=[jax.ShapeDtypeStruct((DEPTH, seq, n_cols), BF16),
                   jax.ShapeDtypeStruct((DEPTH, 1, n_cols), F32),
                   jax.ShapeDtypeStruct((DEPTH, HY_PHASES, n_cols), F32)],
        scratch_shapes=[pltpu.VMEM((seq, HY_HIDDEN), F32)],
        compiler_params=_params("arbitrary", "arbitrary"),
        name=f"hy_filt_l{seq}",
    )(feat, w1p, row(b1), w2, row(b2), w3, row(freq), deltas)


def _dft_matrices(seq, kf):
    n_fft = 2 * seq
    n_ch = seq // kf
    th = 2.0 * math.pi / n_fft
    c = jnp.arange(n_ch, dtype=jnp.int32)[:, None]
    kl = jnp.arange(kf, dtype=jnp.int32)[:, None]
    n = jnp.arange(seq, dtype=jnp.int32)[None, :]
    a1 = th * ((c * kf * n) % n_fft).astype(F32)
    a2 = th * ((kl * n) % n_fft).astype(F32)
    c1, s1, c2, s2 = jnp.cos(a1), jnp.sin(a1), jnp.cos(a2), jnp.sin(a2)
    f_cos = c1[:, None, :] * c2[None, :, :] - s1[:, None, :] * s2[None, :, :]
    f_sin = s1[:, None, :] * c2[None, :, :] + c1[:, None, :] * s2[None, :, :]
    fmat = jnp.concatenate([f_cos, -f_sin], axis=1).astype(BF16)
    c2t, s2t = c2.T, s2.T
    g_cos = c1[:, :, None] * c2t[None, :, :] - s1[:, :, None] * s2t[None, :, :]
    g_sin = s1[:, :, None] * c2t[None, :, :] + c1[:, :, None] * s2t[None, :, :]
    gmat = jnp.concatenate([g_cos, -g_sin], axis=2).astype(BF16)
    return fmat, gmat


def _twiddle_plan(seq):
    freqs = [(0, 1)]
    plan = [None] * HY_LEVELS
    for d in reversed(range(HY_LEVELS)):
        l_d = seq >> d
        plan[d] = [(off, sgn, 2 * l_d) for off, sgn in freqs]
        freqs = [f for off, sgn in freqs for f in ((off, sgn), (l_d - off, -sgn))]
    index = 0
    for d in range(HY_LEVELS):
        plan[d] = [(index + j,) + t for j, t in enumerate(plan[d])]
        index += len(plan[d])
    return plan


def _twiddles(seq):
    k = jnp.arange(seq // HY_PHASES, dtype=F32)[:, None]
    tabs = []
    for level in _twiddle_plan(seq):
        for _, off, sgn, mod in level:
            ang = jnp.broadcast_to((2.0 * math.pi / mod) * (off + sgn * k), (seq // HY_PHASES, HEAD_DIM))
            tabs += [jnp.cos(ang), jnp.sin(ang)]
    return jnp.stack(tabs, axis=0)


def _spectrum_tree(leaves, twiddle, depth=0):
    if len(leaves) == 1:
        return leaves
    even = _spectrum_tree(leaves[0::2], twiddle, depth + 1)
    odd = _spectrum_tree(leaves[1::2], twiddle, depth + 1)
    out = []
    for j, (e, o) in enumerate(zip(even, odd)):
        out += list(_split_fwd(e, o, twiddle(depth, j)))
    return out


def _synthesis_tree(vals, twiddle, depth=0):
    if len(vals) == 1:
        return vals
    even, odd = [], []
    for j in range(len(vals) // 2):
        q0, q1 = _split_inv(vals[2 * j], vals[2 * j + 1], twiddle(depth, j))
        even.append(q0)
        odd.append(q1)
    out = [None] * len(vals)
    out[0::2] = _synthesis_tree(even, twiddle, depth + 1)
    out[1::2] = _synthesis_tree(odd, twiddle, depth + 1)
    return out


def _hy_pre_kernel(z_ref, w_ref, b_ref, ub_ref, ualt_ref, *, seq):
    q = seq // HY_PHASES
    alt = [jnp.zeros((1, HEAD_DIM), F32) for _ in range(HY_PHASES)]
    for m0, rows in _hy_chunks(q):
        v = _short_conv_phases(z_ref, w_ref, b_ref, m0, rows, q)
        sign = _alt_sign(m0, rows)
        for r in range(HY_PHASES):
            ub_ref[r * q + m0:r * q + m0 + rows, :] = v[r].astype(BF16)
            alt[r] = alt[r] + jnp.sum(v[r] * sign, axis=0, keepdims=True)
    for r in range(HY_PHASES):
        ualt_ref[r:r + 1, :] = alt[r]


def _hy_spec_kernel(u_ref, h_ref, a_ref, f_ref, g_ref, tw_ref, y_ref, *, seq, kf):
    c = pl.program_id(1)
    cw = HY_COLS
    q = seq // HY_PHASES
    n_fft = 2 * seq

    @pl.when(c == 0)
    def _():
        y_ref[...] = jnp.zeros_like(y_ref)

    f = f_ref[...]
    plan = _twiddle_plan(seq)

    def twiddles(width):
        rep = width // HEAD_DIM
        cache = {}

        def get(depth, j):
            idx = plan[depth][j][0]
            if idx not in cache:
                cache[idx] = (jnp.concatenate([tw_ref[2 * idx]] * rep, axis=1),
                              -jnp.concatenate([tw_ref[2 * idx + 1]] * rep, axis=1))
            return cache[idx]
        return get

    def spectrum(ref, width):
        leaves = []
        for r in range(HY_PHASES):
            d = jnp.dot(f, ref[r * q:(r + 1) * q, :], preferred_element_type=F32)
            leaves.append((d[:kf], d[kf:]))
        return _spectrum_tree(leaves, twiddles(width))

    row = lax.broadcasted_iota(jnp.int32, (kf, 1), 0)
    a = a_ref[...]
    wgt = jnp.where((row == 0) & (c == 0), 1.0 / n_fft, 2.0 / n_fft) / (a[:, :cw] + a[:, cw:] + 1e-6)
    filt = [((x[0][:, :cw] + x[0][:, cw:]) * wgt, (x[1][:, :cw] - x[1][:, cw:]) * wgt)
            for x in spectrum(h_ref, 2 * cw)]
    sig = spectrum(u_ref, BATCH * cw)
    tw_c = twiddles(cw)
    out = [[] for _ in range(HY_PHASES)]
    for b in range(BATCH):
        bs = slice(b * cw, (b + 1) * cw)
        p = [_cmul((x[0][:, bs], x[1][:, bs]), s) for x, s in zip(sig, filt)]
        for r, val in enumerate(_synthesis_tree(p, tw_c)):
            out[r].append(val)
    for r in range(HY_PHASES):
        spec = jnp.concatenate([jnp.concatenate([v[0] for v in out[r]], axis=1),
                                jnp.concatenate([v[1] for v in out[r]], axis=1)], axis=0).astype(BF16)
        for r0, rows in _hy_chunks(q, HY_INV_ROWS):
            y_ref[r * q + r0:r * q + r0 + rows, :] += jnp.dot(g_ref[r0:r0 + rows, :], spec,
                                                              preferred_element_type=F32)


def _edge_terms(ualt_ref, hf_ref, hb_ref, af_ref, ab_ref, seq):
    k_edge = seq // HY_PHASES
    plan = _twiddle_plan(seq)

    def twiddle(depth, j):
        _, off, sgn, mod = plan[depth][j]
        ang = 2.0 * math.pi * (off + sgn * k_edge) / mod
        return math.cos(ang), -math.sin(ang)

    def spectrum(ref):
        return _spectrum_tree([(ref[r:r + 1, :], 0.0) for r in range(HY_PHASES)], twiddle)

    sig, fwd, bwd = spectrum(ualt_ref), spectrum(hf_ref), spectrum(hb_ref)
    p = [_cmul(u, _cadd(f, _conj(b))) for u, f, b in zip(sig, fwd, bwd)]
    scale = (1.0 / (2 * seq)) / (af_ref[...] + ab_ref[...] + 1e-6)
    return [leaf[0] * scale for leaf in _synthesis_tree(p, twiddle)]


def _hy_mid_kernel(zv_ref, zx_ref, y_ref, ualt_ref, hf_ref, hb_ref, af_ref, ab_ref, wv_ref, bv_ref,
                   wx_ref, bx_ref, skip_ref, s_ref, sb_ref, salt_ref, *, seq):
    q = seq // HY_PHASES
    amp = _edge_terms(ualt_ref, hf_ref, hb_ref, af_ref, ab_ref, seq)
    alt = [jnp.zeros((1, HEAD_DIM), F32) for _ in range(HY_PHASES)]
    skip = skip_ref[...]
    for m0, rows in _hy_chunks(q):
        sign = _alt_sign(m0, rows)
        v = _short_conv_phases(zv_ref, wv_ref, bv_ref, m0, rows, q)
        x1 = _short_conv_phases(zx_ref, wx_ref, bx_ref, m0, rows, q)
        for r in range(HY_PHASES):
            rs = slice(r * q + m0, r * q + m0 + rows)
            s = x1[r] * (y_ref[rs, :] + sign * amp[r] + v[r] * skip)
            s_ref[rs, :] = s
            sb_ref[rs, :] = s.astype(BF16)
            alt[r] = alt[r] + jnp.sum(s * sign, axis=0, keepdims=True)
    for r in range(HY_PHASES):
        salt_ref[r:r + 1, :] = alt[r]


def _hy_post_kernel(zx_ref, y_ref, s_ref, salt_ref, hf_ref, hb_ref, af_ref, ab_ref, wx_ref, bx_ref,
                    skip_ref, o_ref, *, seq):
    q = seq // HY_PHASES
    amp = _edge_terms(salt_ref, hf_ref, hb_ref, af_ref, ab_ref, seq)
    skip = skip_ref[...]
    for m0, rows in _hy_chunks(q):
        sign = _alt_sign(m0, rows)
        x2 = _short_conv_phases(zx_ref, wx_ref, bx_ref, m0, rows, q)
        for r in range(HY_PHASES):
            rs = slice(r * q + m0, r * q + m0 + rows)
            y = y_ref[rs, :] + sign * amp[r] + s_ref[rs, :] * skip
            o_ref[pl.ds(HY_PHASES * m0 + r, rows, stride=HY_PHASES), :] = x2[r] * y


def _hyena(z, z_rb0, conv_w, conv_b, skip, filt, fmat, gmat, tw, *, layer, seq):
    hb, asum, halt = filt
    q = seq // HY_PHASES
    kf = min(HY_KF, q)
    n_ch = q // kf
    n_cols = HY_WIDTH // HY_COLS
    ew = HEAD_DIM
    sub = HY_COLS // ew
    n_ew = HY_WIDTH // ew
    hy0 = V_END // ew

    def zcols(which):
        return pl.BlockSpec((seq, ew), lambda b, e: (z_rb0 + b, hy0 + which * n_ew + e))

    def cw(which):
        return pl.BlockSpec((None, HY_SHORT, ew), lambda b, e: (layer, 0, which * n_ew + e))

    def cb(which):
        return pl.BlockSpec((None, 1, ew), lambda b, e: (layer, 0, which * n_ew + e))

    def fstat(order, direction, rows):
        return pl.BlockSpec((None, rows, ew),
                            lambda b, e: (layer, 0, ((order * n_cols + e // sub) * 2 + direction) * sub + e % sub))

    def skip_spec(order):
        return pl.BlockSpec((None, 1, ew), lambda b, e: (layer * HY_ORDER + order, 0, e))

    def act_col(b, e):
        return ((e // sub) * BATCH + b) * sub + e % sub

    seq_blk = pl.BlockSpec((seq, ew), lambda b, e: (0, act_col(b, e)))
    alt_spec = pl.BlockSpec((HY_PHASES, ew), lambda b, e: (0, act_col(b, e)))
    ew_params = _params("arbitrary", "arbitrary")
    act = jax.ShapeDtypeStruct((seq, BATCH * HY_WIDTH), F32)
    act_b = jax.ShapeDtypeStruct((seq, BATCH * HY_WIDTH), BF16)
    alt_shape = jax.ShapeDtypeStruct((HY_PHASES, BATCH * HY_WIDTH), F32)

    def spectral(ub, order):
        return pl.pallas_call(
            functools.partial(_hy_spec_kernel, seq=seq, kf=kf),
            grid=(n_cols, n_ch),
            in_specs=[
                pl.BlockSpec((seq, BATCH * HY_COLS), lambda j, c: (0, j)),
                pl.BlockSpec((None, seq, 2 * HY_COLS), lambda j, c: (layer, 0, order * n_cols + j),
                             pipeline_mode=pl.Buffered(1)),
                pl.BlockSpec((None, 1, 2 * HY_COLS), lambda j, c: (layer, 0, order * n_cols + j)),
                pl.BlockSpec((None, 2 * kf, q), lambda j, c: (c, 0, 0)),
                pl.BlockSpec((None, q, 2 * kf), lambda j, c: (c, 0, 0)),
                pl.BlockSpec((tw.shape[0], kf, HEAD_DIM), lambda j, c: (0, c, 0)),
            ],
            out_specs=pl.BlockSpec((seq, BATCH * HY_COLS), lambda j, c: (0, j)),
            out_shape=act,
            compiler_params=_params("arbitrary", "arbitrary"),
            name=f"hy_spec{order}_l{seq}",
        )(ub, hb, asum, fmat, gmat, tw)

    ub, ualt = pl.pallas_call(
        functools.partial(_hy_pre_kernel, seq=seq),
        grid=(BATCH, n_ew),
        in_specs=[zcols(0), cw(0), cb(0)],
        out_specs=[seq_blk, alt_spec],
        out_shape=[act_b, alt_shape],
        compiler_params=ew_params,
        name=f"hy_pre_l{seq}",
    )(z, conv_w, conv_b)
    y0 = spectral(ub, 0)
    s, sb, salt = pl.pallas_call(
        functools.partial(_hy_mid_kernel, seq=seq),
        grid=(BATCH, n_ew),
        in_specs=[zcols(0), zcols(1), seq_blk, alt_spec, fstat(0, 0, HY_PHASES), fstat(0, 1, HY_PHASES), fstat(0, 0, 1),
                  fstat(0, 1, 1), cw(0), cb(0), cw(1), cb(1), skip_spec(0)],
        out_specs=[seq_blk, seq_blk, alt_spec],
        out_shape=[act, act_b, alt_shape],
        compiler_params=ew_params,
        name=f"hy_mid_l{seq}",
    )(z, z, y0, ualt, halt, halt, asum, asum, conv_w, conv_b, conv_w, conv_b, skip)
    y1 = spectral(sb, 1)
    return pl.pallas_call(
        functools.partial(_hy_post_kernel, seq=seq),
        grid=(BATCH, n_ew),
        in_specs=[zcols(2), seq_blk, seq_blk, alt_spec, fstat(1, 0, HY_PHASES), fstat(1, 1, HY_PHASES),
                  fstat(1, 0, 1), fstat(1, 1, 1), cw(2), cb(2), skip_spec(1)],
        out_specs=pl.BlockSpec((seq, ew), lambda b, e: (b, e)),
        out_shape=jax.ShapeDtypeStruct((BATCH * seq, HY_WIDTH), F32),
        compiler_params=ew_params,
        name=f"hy_post_l{seq}",
    )(z, y1, s, salt, halt, halt, asum, asum, conv_w, conv_b, skip)


def _outproj_kernel(a_ref, b_ref, c_ref, cc_ref, w_ref, x_ref, mod_ref, g_ref, o_ref, *, lat_tiles):
    c = c_ref[...]
    if cc_ref is not None:
        c = jnp.where(pl.program_id(0) < lat_tiles, c, cc_ref[...])
    y = jnp.dot(a_ref[...], w_ref[:SGU_WIDTH, :], preferred_element_type=F32)
    y += jnp.dot(b_ref[...], w_ref[SGU_WIDTH:SGU_WIDTH + ATTN_WIDTH, :], preferred_element_type=F32)
    y += jnp.dot(c.astype(BF16), w_ref[SGU_WIDTH + ATTN_WIDTH:, :], preferred_element_type=F32)
    o_ref[...] = x_ref[...] + mod_ref[5:6, :] * _rms(y, g_ref[3:4, :])


def _outproj(a, b, c_lat, c_ctx, w_out, x, mods, norm_g, *, layer, n_tiles):
    lat_tiles = N_LAT // TM
    c_specs = [pl.BlockSpec((TM, HY_WIDTH), lambda i: (jnp.minimum(i, lat_tiles - 1), 0))]
    c_args = [c_lat]
    if c_ctx is not None:
        c_specs.append(pl.BlockSpec((TM, HY_WIDTH), lambda i: (jnp.maximum(i - lat_tiles, 0), 0)))
        c_args.append(c_ctx)

    def body(a_ref, b_ref, c_ref, *refs):
        cc_ref, rest = (refs[0], refs[1:]) if c_ctx is not None else (None, refs)
        _outproj_kernel(a_ref, b_ref, c_ref, cc_ref, *rest, lat_tiles=lat_tiles)

    return pl.pallas_call(
        body,
        grid=(n_tiles,),
        in_specs=[
            pl.BlockSpec((TM, SGU_WIDTH), lambda i: (i, 0)),
            pl.BlockSpec((TM, ATTN_WIDTH), lambda i: (i, 0)),
            *c_specs,
            pl.BlockSpec((D_MIX, D_MODEL), lambda i: (0, 0), pipeline_mode=pl.Buffered(1)),
            pl.BlockSpec((TM, D_MODEL), lambda i: (i, 0)),
            pl.BlockSpec((None, None, N_MOD, D_MODEL), lambda i: (layer, _mod_row(i), 0, 0)),
            pl.BlockSpec((None, 6, D_MODEL), lambda i: (layer, 0, 0)),
        ],
        out_specs=pl.BlockSpec((TM, D_MODEL), lambda i: (i, 0)),
        out_shape=jax.ShapeDtypeStruct((n_tiles * TM, D_MODEL), F32),
        compiler_params=_params("arbitrary"),
        name="outproj",
    )(a, b, *c_args, w_out, x, mods, norm_g)


def _rope_tables():
    rows = SEQ // GRID_W
    row = jnp.broadcast_to(jnp.arange(rows)[:, None], (rows, GRID_W)).reshape(SEQ).astype(F32)
    col = jnp.broadcast_to(jnp.arange(GRID_W)[None, :], (rows, GRID_W)).reshape(SEQ).astype(F32)
    half = HEAD_DIM // 2
    inv = ROPE_BASE ** (-jnp.arange(0, half, 2, dtype=F32) / half)
    ang_r = row[:, None] * inv
    ang_c = col[:, None] * inv
    ang = jnp.concatenate([ang_r, ang_r, ang_c, ang_c], axis=-1)
    cos, sin = jnp.cos(ang), jnp.sin(ang)
    first = (jnp.arange(HEAD_DIM) % half) < (half // 2)
    sa = jnp.where(first, -sin, 0.0)
    sb = jnp.where(first, 0.0, sin)

    def stack(lat, ctx_value):
        return jnp.concatenate([lat] * BATCH + [jnp.full((N_CTX, HEAD_DIM), ctx_value, F32)], axis=0)

    return stack(cos, 1.0), stack(sa, 0.0), stack(sb, 0.0)


def kernel(x, c, ctx, c_ctx, w_ada, b_ada, norm_g, ffn1_gate, ffn1_up, ffn1_down, w_in,
           sgu_g, sgu_w, sgu_b, attn_sink, hy_conv_w, hy_conv_b, hy_f_w1, hy_f_b1, hy_f_w2,
           hy_f_b2, hy_f_w3, hy_freq, hy_skip, w_out, ffn2_gate, ffn2_up, ffn2_down):
    xs = jnp.concatenate([x.reshape(N_LAT, D_MODEL), ctx.reshape(N_CTX, D_MODEL)], axis=0)
    cc = jnp.concatenate([c, c_ctx[None], jnp.zeros((MOD_ROWS - BATCH - 1, D_MODEL), F32)], axis=0)
    mods = _ada_mods(cc, w_ada, b_ada)

    w1 = [w[0].astype(BF16) for w in (ffn1_gate, ffn1_up, ffn1_down)]
    cos, sa, sb = _rope_tables()
    conv_b = hy_conv_b.reshape(DEPTH, 1, (HY_ORDER + 1) * HY_WIDTH)
    skip = hy_skip.reshape(DEPTH * HY_ORDER, 1, HY_WIDTH)
    filt_w = (hy_f_w1, hy_f_b1, hy_f_w2, hy_f_b2, hy_f_w3, hy_freq)
    filt_lat = _hyena_filters(SEQ, *filt_w)
    filt_ctx = _hyena_filters(CTX_LEN, *filt_w)
    f_lat, g_lat = _dft_matrices(SEQ // HY_PHASES, min(HY_KF, SEQ // HY_PHASES))
    f_ctx, g_ctx = _dft_matrices(CTX_LEN // HY_PHASES, min(HY_KF, CTX_LEN // HY_PHASES))
    tw_lat, tw_ctx = _twiddles(SEQ), _twiddles(CTX_LEN)

    all_tiles = N_TOK // TM
    lat_tiles = N_LAT // TM
    ctx_rb = N_LAT // CTX_LEN

    for i in range(DEPTH):
        last = i == DEPTH - 1
        n_tiles = lat_tiles if last else all_tiles
        n_rows = n_tiles * TM

        jobs = [(w, i) for w in (ffn2_gate, ffn2_up, ffn2_down, w_in, w_out)]
        xs, (w2g, w2u, w2d, w_in_b, w_out_b) = _ffn(xs, mods, norm_g, *w1, jobs, layer=i, k=0, n_tiles=all_tiles)
        z = _inproj(xs, mods, norm_g, w_in_b, cos, sa, sb, layer=i)

        a_out = _sgu(z, sgu_g, sgu_w, sgu_b, layer=i, n_rows=n_rows)
        b_out = _attention(z, attn_sink, layer=i, n_rows=n_rows)
        c_lat = _hyena(z, 0, hy_conv_w, conv_b, skip, filt_lat, f_lat, g_lat, tw_lat, layer=i, seq=SEQ)
        c_ctx = None
        if not last:
            c_ctx = _hyena(z, ctx_rb, hy_conv_w, conv_b, skip, filt_ctx, f_ctx, g_ctx, tw_ctx, layer=i, seq=CTX_LEN)

        xs = _outproj(a_out, b_out, c_lat, c_ctx, w_out_b, xs, mods, norm_g, layer=i, n_tiles=n_tiles)
        jobs = [] if last else [(w, i + 1) for w in (ffn1_gate, ffn1_up, ffn1_down)]
        xs, w1 = _ffn(xs, mods, norm_g, w2g, w2u, w2d, jobs, layer=i, k=2, n_tiles=n_tiles)

    return xs[:N_LAT].reshape(BATCH, SEQ, D_MODEL)
```

```python
import functools
import math

import jax
import jax.numpy as jnp
from jax import lax
from jax.experimental import pallas as pl
from jax.experimental.pallas import tpu as pltpu

D_MODEL = 2048
BATCH = 2
SEQ = 4096
DEPTH = 4
GRID_W = 64
CTX_LEN = 256
HEAD_DIM = 128
NORM_EPS = 1e-6
N_MOD = 9
D_FF = 5632
SGU_GROUPS = 4
SGU_WIDTH = SGU_GROUPS * HEAD_DIM
CHUNK = 128
N_Q_HEADS = 8
N_KV_HEADS = 2
Q_GROUP = N_Q_HEADS // N_KV_HEADS
ATTN_WIDTH = N_Q_HEADS * HEAD_DIM
KV_WIDTH = N_KV_HEADS * HEAD_DIM
WINDOW = 128
BLOCK = 128
ROPE_BASE = 10000.0
NEG_INF = -1e30
HY_GROUPS = 4
HY_WIDTH = HY_GROUPS * HEAD_DIM
HY_ORDER = 2
HY_SHORT = 3
HY_BANDS = 16
HY_EMB = 1 + 2 * HY_BANDS
HY_HIDDEN = 64
HY_TARGET = 1e-2
HY_FAST_DECAY = 0.3
HY_SLOW_DECAY = 1.5
D_MIX = SGU_WIDTH + ATTN_WIDTH + HY_WIDTH
A_END = 2 * SGU_WIDTH
Q_END = A_END + ATTN_WIDTH
K_END = Q_END + KV_WIDTH
V_END = K_END + KV_WIDTH
IN_COLS = V_END + (HY_ORDER + 1) * HY_WIDTH

N_LAT = BATCH * SEQ
N_CTX = BATCH * CTX_LEN
N_TOK = N_LAT + N_CTX
MOD_ROWS = 8

TM = 512
TF = 512
FFN_ROWS = 16
BF16_SUBLANES = 16
TN_IN = 1024
TN_ADA = 1024
HY_COLS = 256
HY_FILT_COLS = 2 * HY_COLS
HY_KF = 256
HY_LEVELS = 3
HY_PHASES = 2 ** HY_LEVELS
HY_ROWS = 512
HY_INV_ROWS = 1024
HY_EMB_PAD = 128
VMEM_LIMIT = 56 * 1024 * 1024

F32 = jnp.float32
BF16 = jnp.bfloat16


def _params(*sem):
    return pltpu.CompilerParams(dimension_semantics=sem, vmem_limit_bytes=VMEM_LIMIT)


def _mod_row(i):
    return jnp.minimum(i // (SEQ // TM), BATCH)


def _silu(a):
    return a / (1.0 + jnp.exp(-a))


def _rms(x, g):
    return x * lax.rsqrt(jnp.mean(x * x, axis=-1, keepdims=True) + NORM_EPS) * g


def _ada_kernel(c_ref, w_ref, b_ref, o_ref):
    s = _silu(c_ref[...]).astype(BF16)
    o_ref[...] = jnp.dot(s, w_ref[...].astype(BF16), preferred_element_type=F32) + b_ref[...]


def _ada_mods(cc, w_ada, b_ada):
    n_out = N_MOD * D_MODEL
    out = pl.pallas_call(
        _ada_kernel,
        grid=(DEPTH, n_out // TN_ADA),
        in_specs=[
            pl.BlockSpec((MOD_ROWS, D_MODEL), lambda l, j: (0, 0)),
            pl.BlockSpec((None, D_MODEL, TN_ADA), lambda l, j: (l, 0, j)),
            pl.BlockSpec((None, 1, TN_ADA), lambda l, j: (l, 0, j)),
        ],
        out_specs=pl.BlockSpec((None, MOD_ROWS, TN_ADA), lambda l, j: (l, 0, j)),
        out_shape=jax.ShapeDtypeStruct((DEPTH, MOD_ROWS, n_out), F32),
        compiler_params=_params("arbitrary", "arbitrary"),
        name="ada_mods",
    )(cc, w_ada, b_ada.reshape(DEPTH, 1, n_out))
    return out.reshape(DEPTH, MOD_ROWS, N_MOD, D_MODEL)


def _ffn_kernel(*refs, k, nf, n_jobs):
    x_ref, mod_ref, g_ref, wg_ref, wu_ref, wd_ref = refs[:6]
    cast_src = refs[6:6 + n_jobs]
    o_ref = refs[6 + n_jobs]
    cast_dst = refs[7 + n_jobs:7 + 2 * n_jobs]
    h_ref, acc_ref = refs[7 + 2 * n_jobs:]
    f = pl.program_id(1)

    for src, dst in zip(cast_src, cast_dst):
        dst[...] = src[...].astype(BF16)

    @pl.when(f == 0)
    def _():
        g = g_ref[2 * k:2 * k + 1, :]
        m1 = 1.0 + mod_ref[3 * k:3 * k + 1, :]
        m2 = mod_ref[3 * k + 1:3 * k + 2, :]
        for r0 in range(0, TM, FFN_ROWS):
            rs = slice(r0, r0 + FFN_ROWS)
            h_ref[rs, :] = (_rms(x_ref[rs, :], g) * m1 + m2).astype(BF16)
        acc_ref[...] = jnp.zeros_like(acc_ref)

    h = h_ref[...]
    a = jnp.dot(h, wg_ref[...], preferred_element_type=F32)
    b = jnp.dot(h, wu_ref[...], preferred_element_type=F32)
    p = (_silu(a) * b).astype(BF16)
    acc_ref[...] += jnp.dot(p, wd_ref[...], preferred_element_type=F32)

    @pl.when(f == nf - 1)
    def _():
        g = g_ref[2 * k + 1:2 * k + 2, :]
        gate = 0.5 * mod_ref[3 * k + 2:3 * k + 3, :]
        for r0 in range(0, TM, FFN_ROWS):
            rs = slice(r0, r0 + FFN_ROWS)
            o_ref[rs, :] = x_ref[rs, :] + gate * _rms(acc_ref[rs, :], g)


def _ffn(x, mods, norm_g, wg, wu, wd, cast_jobs, *, layer, k, n_tiles):
    nf = D_FF // TF
    n_steps = n_tiles * nf
    in_specs = [
        pl.BlockSpec((TM, D_MODEL), lambda i, f: (i, 0)),
        pl.BlockSpec((None, None, N_MOD, D_MODEL), lambda i, f: (layer, _mod_row(i), 0, 0)),
        pl.BlockSpec((None, 6, D_MODEL), lambda i, f: (layer, 0, 0)),
        pl.BlockSpec((D_MODEL, TF), lambda i, f: (0, f)),
        pl.BlockSpec((D_MODEL, TF), lambda i, f: (0, f)),
        pl.BlockSpec((TF, D_MODEL), lambda i, f: (f, 0)),
    ]
    out_specs = [pl.BlockSpec((TM, D_MODEL), lambda i, f: (i, 0))]
    out_shape = [jax.ShapeDtypeStruct((n_tiles * TM, D_MODEL), F32)]
    for w, src_layer in cast_jobs:
        n_rows, n_cols = w.shape[1:]
        rb = BF16_SUBLANES * pl.cdiv(pl.cdiv(n_rows, n_steps), BF16_SUBLANES)
        assert n_rows % rb == 0
        last_blk = n_rows // rb - 1

        def step_block(i, f, last_blk=last_blk):
            return jnp.minimum(i * nf + f, last_blk)

        in_specs.append(pl.BlockSpec((None, rb, n_cols),
                                     lambda i, f, sl=src_layer, sb=step_block: (sl, sb(i, f), 0)))
        out_specs.append(pl.BlockSpec((rb, n_cols), lambda i, f, sb=step_block: (sb(i, f), 0)))
        out_shape.append(jax.ShapeDtypeStruct((n_rows, n_cols), BF16))
    outs = pl.pallas_call(
        functools.partial(_ffn_kernel, k=k, nf=nf, n_jobs=len(cast_jobs)),
        grid=(n_tiles, nf),
        in_specs=in_specs,
        out_specs=out_specs,
        out_shape=out_shape,
        scratch_shapes=[pltpu.VMEM((TM, D_MODEL), BF16), pltpu.VMEM((TM, D_MODEL), F32)],
        compiler_params=_params("arbitrary", "arbitrary"),
        name=f"ffn{k}",
    )(x, mods, norm_g, wg, wu, wd, *[w for w, _ in cast_jobs])
    return outs[0], list(outs[1:])


def _rope(t, cos, sa, sb):
    return t * cos + pltpu.roll(t, 3 * HEAD_DIM // 4, 1) * sa + pltpu.roll(t, HEAD_DIM // 4, 1) * sb


def _inproj_kernel(x_ref, mod_ref, g_ref, w_ref, cos_ref, sa_ref, sb_ref, o_ref):
    y = _rms(x_ref[...], g_ref[2:3, :])
    h = (y * (1.0 + mod_ref[3:4, :]) + mod_ref[4:5, :]).astype(BF16)
    cos, sa, sb = cos_ref[...], sa_ref[...], sb_ref[...]
    n_rope = (K_END - A_END) // HEAD_DIM
    for n in range(IN_COLS // TN_IN):
        c0 = n * TN_IN
        z = jnp.dot(h, w_ref[:, c0:c0 + TN_IN], preferred_element_type=F32)
        for hd in range(TN_IN // HEAD_DIM):
            col = c0 + hd * HEAD_DIM
            zs = z[:, hd * HEAD_DIM:(hd + 1) * HEAD_DIM]
            if A_END <= col < A_END + n_rope * HEAD_DIM:
                zs = _rope(zs, cos, sa, sb)
            o_ref[:, col:col + HEAD_DIM] = zs


def _inproj(x, mods, norm_g, w_in, cos, sa, sb, *, layer):
    n_tiles = N_TOK // TM
    tab = pl.BlockSpec((TM, HEAD_DIM), lambda i: (i, 0))
    return pl.pallas_call(
        _inproj_kernel,
        grid=(n_tiles,),
        in_specs=[
            pl.BlockSpec((TM, D_MODEL), lambda i: (i, 0)),
            pl.BlockSpec((None, None, N_MOD, D_MODEL), lambda i: (layer, _mod_row(i), 0, 0)),
            pl.BlockSpec((None, 6, D_MODEL), lambda i: (layer, 0, 0)),
            pl.BlockSpec((D_MODEL, IN_COLS), lambda i: (0, 0), pipeline_mode=pl.Buffered(1)),
            tab, tab, tab,
        ],
        out_specs=pl.BlockSpec((TM, IN_COLS), lambda i: (i, 0)),
        out_shape=jax.ShapeDtypeStruct((N_TOK, IN_COLS), F32),
        compiler_params=_params("arbitrary"),
        name="inproj",
    )(x, mods, norm_g, w_in, cos, sa, sb)


def _sgu_kernel(z_ref, g_ref, w_ref, b_ref, o_ref, *, rows):
    inv_sqrt2 = 1.0 / math.sqrt(2.0)
    for c in range(rows // CHUNK):
        rs = slice(c * CHUNK, (c + 1) * CHUNK)
        a = z_ref[rs, :]
        a = a * (lax.erf(a * inv_sqrt2) + 1.0) * 0.5
        u, v = a[:, :SGU_WIDTH], a[:, SGU_WIDTH:]
        mu = jnp.mean(v, axis=-1, keepdims=True)
        vc = v - mu
        var = jnp.mean(vc * vc, axis=-1, keepdims=True)
        v = (vc * lax.rsqrt(var + NORM_EPS) * g_ref[...]).astype(BF16)
        for g in range(SGU_GROUPS):
            cs = slice(g * HEAD_DIM, (g + 1) * HEAD_DIM)
            mixed = jnp.dot(w_ref[g].astype(BF16), v[:, cs], preferred_element_type=F32) + b_ref[:, g:g + 1]
            o_ref[rs, cs] = (u[:, cs] * mixed).astype(o_ref.dtype)


def _sgu(z, sgu_g, sgu_w, sgu_b, *, layer, n_rows):
    rows = TM
    return pl.pallas_call(
        functools.partial(_sgu_kernel, rows=rows),
        grid=(n_rows // rows,),
        in_specs=[
            pl.BlockSpec((rows, A_END), lambda i: (i, 0)),
            pl.BlockSpec((None, 1, SGU_WIDTH), lambda i: (layer, 0, 0)),
            pl.BlockSpec((None, SGU_GROUPS, CHUNK, CHUNK), lambda i: (layer, 0, 0, 0)),
            pl.BlockSpec((None, CHUNK, SGU_GROUPS), lambda i: (layer, 0, 0)),
        ],
        out_specs=pl.BlockSpec((rows, SGU_WIDTH), lambda i: (i, 0)),
        out_shape=jax.ShapeDtypeStruct((n_rows, SGU_WIDTH), BF16),
        compiler_params=_params("arbitrary"),
        name="sgu",
    )(z, sgu_g.reshape(DEPTH, 1, SGU_WIDTH), sgu_w, jnp.swapaxes(sgu_b, 1, 2))


def _attn_kernel(sink_ref, q_ref, kp_ref, kc_ref, kn_ref, vp_ref, vc_ref, vn_ref, kx_ref, vx_ref,
                 o_ref, *, layer, nb, n_lat_blocks):
    s_id = pl.program_id(0)
    is_lat = s_id < n_lat_blocks
    n = s_id % nb
    n_keys = CTX_LEN + 3 * BLOCK
    n_rows = Q_GROUP * BLOCK
    row_id = lax.broadcasted_iota(jnp.int32, (n_rows, 1), 0)
    row = lax.broadcasted_iota(jnp.int32, (n_rows, n_keys), 0) & (BLOCK - 1)
    col = lax.broadcasted_iota(jnp.int32, (n_rows, n_keys), 1)
    rel = col - (CTX_LEN + BLOCK)
    lo = jnp.where(is_lat, -n * BLOCK, 4 * BLOCK)
    hi = jnp.where(is_lat, (nb - n) * BLOCK, -4 * BLOCK)
    dist = rel - row
    valid = (col < CTX_LEN) | ((dist >= -WINDOW) & (dist <= WINDOW) & (rel >= lo) & (rel < hi))
    scale = HEAD_DIM ** -0.5
    nt = (((1,), (1,)), ((), ()))
    for kh in range(N_KV_HEADS):
        ks = slice(kh * HEAD_DIM, (kh + 1) * HEAD_DIM)
        keys = jnp.concatenate([kx_ref[:, ks], kp_ref[:, ks], kc_ref[:, ks], kn_ref[:, ks]], axis=0).astype(BF16)
        vals = jnp.concatenate([vx_ref[:, ks], vp_ref[:, ks], vc_ref[:, ks], vn_ref[:, ks]], axis=0).astype(BF16)
        heads = [kh * Q_GROUP + g for g in range(Q_GROUP)]
        q = jnp.concatenate([q_ref[:, hd * HEAD_DIM:(hd + 1) * HEAD_DIM] for hd in heads], axis=0).astype(BF16)
        sink = jnp.full((n_rows, 1), sink_ref[layer, heads[0]], F32)
        for g, hd in enumerate(heads[1:], start=1):
            sink = jnp.where(row_id >= g * BLOCK, sink_ref[layer, hd], sink)
        s = lax.dot_general(q, keys, nt, preferred_element_type=F32) * scale
        s = jnp.where(valid, s, NEG_INF)
        m = jnp.maximum(jnp.max(s, axis=-1, keepdims=True), sink)
        p = jnp.exp(s - m)
        denom = jnp.sum(p, axis=-1, keepdims=True) + jnp.exp(sink - m)
        probs = (p / denom).astype(BF16)
        out = jnp.dot(probs, vals, preferred_element_type=F32).astype(o_ref.dtype)
        for g, hd in enumerate(heads):
            o_ref[:, hd * HEAD_DIM:(hd + 1) * HEAD_DIM] = out[g * BLOCK:(g + 1) * BLOCK]


def _attention(z, attn_sink, *, layer, n_rows):
    nb = SEQ // BLOCK
    n_lat_blocks = N_LAT // BLOCK
    n_blocks = n_rows // BLOCK
    ctx_blocks = CTX_LEN // BLOCK
    q_col = A_END // ATTN_WIDTH
    k_col = Q_END // KV_WIDTH
    v_col = K_END // KV_WIDTH

    def lat(s):
        return s < n_lat_blocks

    def band(off, col):
        def index(s):
            b = s // nb
            nbr = jnp.clip(s % nb + off, 0, nb - 1) + b * nb
            return (jnp.where(lat(s), nbr, s), col)
        return pl.BlockSpec((BLOCK, KV_WIDTH), index)

    def ctx(col):
        def index(s):
            b = jnp.where(lat(s), s // nb, (s - n_lat_blocks) // ctx_blocks)
            return (N_LAT // CTX_LEN + b, col)
        return pl.BlockSpec((CTX_LEN, KV_WIDTH), index)

    return pl.pallas_call(
        functools.partial(_attn_kernel, layer=layer, nb=nb, n_lat_blocks=n_lat_blocks),
        grid=(n_blocks,),
        in_specs=[
            pl.BlockSpec(memory_space=pltpu.SMEM),
            pl.BlockSpec((BLOCK, ATTN_WIDTH), lambda s: (s, q_col)),
            band(-1, k_col), band(0, k_col), band(1, k_col),
            band(-1, v_col), band(0, v_col), band(1, v_col),
            ctx(k_col), ctx(v_col),
        ],
        out_specs=pl.BlockSpec((BLOCK, ATTN_WIDTH), lambda s: (s, 0)),
        out_shape=jax.ShapeDtypeStruct((n_rows, ATTN_WIDTH), BF16),
        compiler_params=_params("arbitrary"),
        name="attn",
    )(attn_sink, z, z, z, z, z, z, z, z, z)


def _short_conv_phases(ref, w_ref, b_ref, m0, rows, n_m):
    ph = HY_PHASES
    z = [ref[pl.ds(ph * m0 + r, rows, stride=ph), :] for r in range(ph)]
    m = lax.broadcasted_iota(jnp.int32, (rows, 1), 0) + m0
    if m0 == 0:
        z_prev = jnp.where(m == 0, 0.0, pltpu.roll(z[ph - 1], 1, 0))
    else:
        z_prev = ref[pl.ds(ph * m0 - 1, rows, stride=ph), :]
    if m0 + rows == n_m:
        z_next = jnp.where(m == n_m - 1, 0.0, pltpu.roll(z[0], rows - 1, 0))
    else:
        z_next = ref[pl.ds(ph * m0 + ph, rows, stride=ph), :]
    ext = [z_prev] + z + [z_next]
    w0, w1, w2, b = w_ref[0:1, :], w_ref[1:2, :], w_ref[2:3, :], b_ref[...]
    return [ext[r] * w0 + ext[r + 1] * w1 + ext[r + 2] * w2 + b for r in range(ph)]


def _cmul(a, b):
    return a[0] * b[0] - a[1] * b[1], a[0] * b[1] + a[1] * b[0]


def _cadd(a, b):
    return a[0] + b[0], a[1] + b[1]


def _csub(a, b):
    return a[0] - b[0], a[1] - b[1]


def _conj(a):
    return a[0], -a[1]


def _split_fwd(e, o, w):
    t = _cmul(w, o)
    return _cadd(e, t), _conj(_csub(e, t))


def _split_inv(pp, pm, w):
    cm = _conj(pm)
    return _cadd(pp, cm), _cmul(_csub(pp, cm), _conj(w))


def _alt_sign(r0, rows):
    t = lax.broadcasted_iota(jnp.int32, (rows, 1), 0) + r0
    return jnp.where((t & 1) == 0, 1.0, -1.0)


def _hy_chunks(seq, chunk=None):
    rows = min(chunk or HY_ROWS, seq)
    return [(r * rows, rows) for r in range(seq // rows)]


def _filt_kernel(feat_ref, w1_ref, b1_ref, w2_ref, b2_ref, w3_ref, fr_ref, dl_ref,
                 hb_ref, asum_ref, hnyq_ref, hd_ref, *, seq):
    hp = lax.Precision.HIGHEST

    @pl.when(pl.program_id(1) == 0)
    def _():
        fr = fr_ref[...]
        hd = jnp.sin(fr * (jnp.dot(feat_ref[...], w1_ref[...], precision=hp, preferred_element_type=F32)
                           + b1_ref[...]))
        hd_ref[...] = jnp.sin(fr * (jnp.dot(hd, w2_ref[...], precision=hp, preferred_element_type=F32)
                                    + b2_ref[...]))

    q = seq // HY_PHASES
    asum = jnp.zeros((1, HY_FILT_COLS), F32)
    alt = [jnp.zeros((1, HY_FILT_COLS), F32) for _ in range(HY_PHASES)]
    for r0, rows in _hy_chunks(seq, min(HY_ROWS, q)):
        rs = slice(r0, r0 + rows)
        h = jnp.dot(hd_ref[rs, :].astype(BF16), w3_ref[...].astype(BF16), preferred_element_type=F32)
        h = h * jnp.exp(-feat_ref[rs, 0:1] * dl_ref[...])
        asum = asum + jnp.sum(jnp.abs(h), axis=0, keepdims=True)
        alt[r0 // q] = alt[r0 // q] + jnp.sum(h * _alt_sign(r0 % q, rows), axis=0, keepdims=True)
        hb_ref[rs, :] = h.astype(BF16)
    asum_ref[...] = asum
    for r in range(HY_PHASES):
        hnyq_ref[r:r + 1, :] = alt[r]


def _hyena_filters(seq, w1, b1, w2, b2, w3, freq):
    n_tiles = HY_WIDTH // HY_COLS
    w3 = w3.reshape(DEPTH, HY_HIDDEN, HY_ORDER, 2, n_tiles, HY_COLS)
    w3 = jnp.swapaxes(w3, 3, 4).reshape(DEPTH, HY_HIDDEN, HY_ORDER * 2 * HY_WIDTH)
    t = jnp.linspace(0.0, 1.0, seq, dtype=F32)[:, None]
    w = (2.0 * math.pi / seq) * jnp.arange(seq, dtype=F32)[:, None]
    bands = jnp.linspace(1e-4, HY_BANDS - 1, HY_BANDS, dtype=F32)[None, :]
    feat = jnp.concatenate([t, jnp.cos(bands * w), -jnp.sin(bands * w)], axis=-1)
    feat = jnp.pad(feat, ((0, 0), (0, HY_EMB_PAD - HY_EMB)))
    feat = jnp.concatenate([feat[r::HY_PHASES] for r in range(HY_PHASES)], axis=0)
    w1p = jnp.pad(w1, ((0, 0), (0, HY_EMB_PAD - HY_EMB), (0, 0)))
    deltas = jnp.abs(jnp.linspace(math.log(HY_TARGET) / HY_FAST_DECAY,
                                  math.log(HY_TARGET) / HY_SLOW_DECAY, HY_WIDTH, dtype=F32))
    deltas = jnp.broadcast_to(deltas.reshape(1, n_tiles, 1, HY_COLS), (HY_ORDER, n_tiles, 2, HY_COLS))
    deltas = deltas.reshape(1, HY_ORDER * 2 * HY_WIDTH)
    n_cols = HY_ORDER * 2 * HY_WIDTH

    def row(a):
        return a.reshape(DEPTH, 1, HY_HIDDEN)

    vec = pl.BlockSpec((None, 1, HY_HIDDEN), lambda l, j: (l, 0, 0))
    stat = pl.BlockSpec((None, 1, HY_FILT_COLS), lambda l, j: (l, 0, j))
    return pl.pallas_call(
        functools.partial(_filt_kernel, seq=seq),
        grid=(DEPTH, n_cols // HY_FILT_COLS),
        in_specs=[
            pl.BlockSpec((seq, HY_EMB_PAD), lambda l, j: (0, 0)),
            pl.BlockSpec((None, HY_EMB_PAD, HY_HIDDEN), lambda l, j: (l, 0, 0)),
            vec,
            pl.BlockSpec((None, HY_HIDDEN, HY_HIDDEN), lambda l, j: (l, 0, 0)),
            vec,
            pl.BlockSpec((None, HY_HIDDEN, HY_FILT_COLS), lambda l, j: (l, 0, j)),
            vec,
            pl.BlockSpec((1, HY_FILT_COLS), lambda l, j: (0, j)),
        ],
        out_specs=[pl.BlockSpec((None, seq, HY_FILT_COLS), lambda l, j: (l, 0, j)), stat,
                   pl.BlockSpec((None, HY_PHASES, HY_FILT_COLS), lambda l, j: (l, 0, j))],
        out_shape=[jax.ShapeDtypeStruct((DEPTH, seq, n_cols), BF16),
                   jax.ShapeDtypeStruct((DEPTH, 1, n_cols), F32),
                   jax.ShapeDtypeStruct((DEPTH, HY_PHASES, n_cols), F32)],
        scratch_shapes=[pltpu.VMEM((seq, HY_HIDDEN), F32)],
        compiler_params=_params("arbitrary", "arbitrary"),
        name=f"hy_filt_l{seq}",
    )(feat, w1p, row(b1), w2, row(b2), w3, row(freq), deltas)


def _dft_matrices(seq, kf):
    n_fft = 2 * seq
    n_ch = seq // kf
    th = 2.0 * math.pi / n_fft
    c = jnp.arange(n_ch, dtype=jnp.int32)[:, None]
    kl = jnp.arange(kf, dtype=jnp.int32)[:, None]
    n = jnp.arange(seq, dtype=jnp.int32)[None, :]
    a1 = th * ((c * kf * n) % n_fft).astype(F32)
    a2 = th * ((kl * n) % n_fft).astype(F32)
    c1, s1, c2, s2 = jnp.cos(a1), jnp.sin(a1), jnp.cos(a2), jnp.sin(a2)
    f_cos = c1[:, None, :] * c2[None, :, :] - s1[:, None, :] * s2[None, :, :]
    f_sin = s1[:, None, :] * c2[None, :, :] + c1[:, None, :] * s2[None, :, :]
    fmat = jnp.concatenate([f_cos, -f_sin], axis=1).astype(BF16)
    c2t, s2t = c2.T, s2.T
    g_cos = c1[:, :, None] * c2t[None, :, :] - s1[:, :, None] * s2t[None, :, :]
    g_sin = s1[:, :, None] * c2t[None, :, :] + c1[:, :, None] * s2t[None, :, :]
    gmat = jnp.concatenate([g_cos, -g_sin], axis=2).astype(BF16)
    return fmat, gmat


def _twiddle_plan(seq):
    freqs = [(0, 1)]
    plan = [None] * HY_LEVELS
    for d in reversed(range(HY_LEVELS)):
        l_d = seq >> d
        plan[d] = [(off, sgn, 2 * l_d) for off, sgn in freqs]
        freqs = [f for off, sgn in freqs for f in ((off, sgn), (l_d - off, -sgn))]
    index = 0
    for d in range(HY_LEVELS):
        plan[d] = [(index + j,) + t for j, t in enumerate(plan[d])]
        index += len(plan[d])
    return plan


def _twiddles(seq):
    k = jnp.arange(seq // HY_PHASES, dtype=F32)[:, None]
    tabs = []
    for level in _twiddle_plan(seq):
        for _, off, sgn, mod in level:
            ang = jnp.broadcast_to((2.0 * math.pi / mod) * (off + sgn * k), (seq // HY_PHASES, HEAD_DIM))
            tabs += [jnp.cos(ang), jnp.sin(ang)]
    return jnp.stack(tabs, axis=0)


def _spectrum_tree(leaves, twiddle, depth=0):
    if len(leaves) == 1:
        return leaves
    even = _spectrum_tree(leaves[0::2], twiddle, depth + 1)
    odd = _spectrum_tree(leaves[1::2], twiddle, depth + 1)
    out = []
    for j, (e, o) in enumerate(zip(even, odd)):
        out += list(_split_fwd(e, o, twiddle(depth, j)))
    return out


def _synthesis_tree(vals, twiddle, depth=0):
    if len(vals) == 1:
        return vals
    even, odd = [], []
    for j in range(len(vals) // 2):
        q0, q1 = _split_inv(vals[2 * j], vals[2 * j + 1], twiddle(depth, j))
        even.append(q0)
        odd.append(q1)
    out = [None] * len(vals)
    out[0::2] = _synthesis_tree(even, twiddle, depth + 1)
    out[1::2] = _synthesis_tree(odd, twiddle, depth + 1)
    return out


def _hy_pre_kernel(z_ref, w_ref, b_ref, ub_ref, ualt_ref, *, seq):
    q = seq // HY_PHASES
    alt = [jnp.zeros((1, HEAD_DIM), F32) for _ in range(HY_PHASES)]
    for m0, rows in _hy_chunks(q):
        v = _short_conv_phases(z_ref, w_ref, b_ref, m0, rows, q)
        sign = _alt_sign(m0, rows)
        for r in range(HY_PHASES):
            ub_ref[r * q + m0:r * q + m0 + rows, :] = v[r].astype(BF16)
            alt[r] = alt[r] + jnp.sum(v[r] * sign, axis=0, keepdims=True)
    for r in range(HY_PHASES):
        ualt_ref[r:r + 1, :] = alt[r]


def _hy_spec_kernel(u_ref, h_ref, a_ref, f_ref, g_ref, tw_ref, y_ref, *, seq, kf):
    c = pl.program_id(1)
    cw = HY_COLS
    q = seq // HY_PHASES
    n_fft = 2 * seq

    @pl.when(c == 0)
    def _():
        y_ref[...] = jnp.zeros_like(y_ref)

    f = f_ref[...]
    plan = _twiddle_plan(seq)

    def twiddles(width):
        rep = width // HEAD_DIM
        cache = {}

        def get(depth, j):
            idx = plan[depth][j][0]
            if idx not in cache:
                cache[idx] = (jnp.concatenate([tw_ref[2 * idx]] * rep, axis=1),
                              -jnp.concatenate([tw_ref[2 * idx + 1]] * rep, axis=1))
            return cache[idx]
        return get

    def spectrum(ref, width):
        leaves = []
        for r in range(HY_PHASES):
            d = jnp.dot(f, ref[r * q:(r + 1) * q, :], preferred_element_type=F32)
            leaves.append((d[:kf], d[kf:]))
        return _spectrum_tree(leaves, twiddles(width))

    row = lax.broadcasted_iota(jnp.int32, (kf, 1), 0)
    a = a_ref[...]
    wgt = jnp.where((row == 0) & (c == 0), 1.0 / n_fft, 2.0 / n_fft) / (a[:, :cw] + a[:, cw:] + 1e-6)
    filt = [((x[0][:, :cw] + x[0][:, cw:]) * wgt, (x[1][:, :cw] - x[1][:, cw:]) * wgt)
            for x in spectrum(h_ref, 2 * cw)]
    sig = spectrum(u_ref, BATCH * cw)
    tw_c = twiddles(cw)
    out = [[] for _ in range(HY_PHASES)]
    for b in range(BATCH):
        bs = slice(b * cw, (b + 1) * cw)
        p = [_cmul((x[0][:, bs], x[1][:, bs]), s) for x, s in zip(sig, filt)]
        for r, val in enumerate(_synthesis_tree(p, tw_c)):
            out[r].append(val)
    for r in range(HY_PHASES):
        spec = jnp.concatenate([jnp.concatenate([v[0] for v in out[r]], axis=1),
                                jnp.concatenate([v[1] for v in out[r]], axis=1)], axis=0).astype(BF16)
        for r0, rows in _hy_chunks(q, HY_INV_ROWS):
            y_ref[r * q + r0:r * q + r0 + rows, :] += jnp.dot(g_ref[r0:r0 + rows, :], spec,
                                                              preferred_element_type=F32)


def _edge_terms(ualt_ref, hf_ref, hb_ref, af_ref, ab_ref, seq):
    k_edge = seq // HY_PHASES
    plan = _twiddle_plan(seq)

    def twiddle(depth, j):
        _, off, sgn, mod = plan[depth][j]
        ang = 2.0 * math.pi * (off + sgn * k_edge) / mod
        return math.cos(ang), -math.sin(ang)

    def spectrum(ref):
        return _spectrum_tree([(ref[r:r + 1, :], 0.0) for r in range(HY_PHASES)], twiddle)

    sig, fwd, bwd = spectrum(ualt_ref), spectrum(hf_ref), spectrum(hb_ref)
    p = [_cmul(u, _cadd(f, _conj(b))) for u, f, b in zip(sig, fwd, bwd)]
    scale = (1.0 / (2 * seq)) / (af_ref[...] + ab_ref[...] + 1e-6)
    return [leaf[0] * scale for leaf in _synthesis_tree(p, twiddle)]


def _hy_mid_kernel(zv_ref, zx_ref, y_ref, ualt_ref, hf_ref, hb_ref, af_ref, ab_ref, wv_ref, bv_ref,
                   wx_ref, bx_ref, skip_ref, s_ref, sb_ref, salt_ref, *, seq):
    q = seq // HY_PHASES
    amp = _edge_terms(ualt_ref, hf_ref, hb_ref, af_ref, ab_ref, seq)
    alt = [jnp.zeros((1, HEAD_DIM), F32) for _ in range(HY_PHASES)]
    skip = skip_ref[...]
    for m0, rows in _hy_chunks(q):
        sign = _alt_sign(m0, rows)
        v = _short_conv_phases(zv_ref, wv_ref, bv_ref, m0, rows, q)
        x1 = _short_conv_phases(zx_ref, wx_ref, bx_ref, m0, rows, q)
        for r in range(HY_PHASES):
            rs = slice(r * q + m0, r * q + m0 + rows)
            s = x1[r] * (y_ref[rs, :] + sign * amp[r] + v[r] * skip)
            s_ref[rs, :] = s
            sb_ref[rs, :] = s.astype(BF16)
            alt[r] = alt[r] + jnp.sum(s * sign, axis=0, keepdims=True)
    for r in range(HY_PHASES):
        salt_ref[r:r + 1, :] = alt[r]


def _hy_post_kernel(zx_ref, y_ref, s_ref, salt_ref, hf_ref, hb_ref, af_ref, ab_ref, wx_ref, bx_ref,
                    skip_ref, o_ref, *, seq):
    q = seq // HY_PHASES
    amp = _edge_terms(salt_ref, hf_ref, hb_ref, af_ref, ab_ref, seq)
    skip = skip_ref[...]
    for m0, rows in _hy_chunks(q):
        sign = _alt_sign(m0, rows)
        x2 = _short_conv_phases(zx_ref, wx_ref, bx_ref, m0, rows, q)
        for r in range(HY_PHASES):
            rs = slice(r * q + m0, r * q + m0 + rows)
            y = y_ref[rs, :] + sign * amp[r] + s_ref[rs, :] * skip
            o_ref[pl.ds(HY_PHASES * m0 + r, rows, stride=HY_PHASES), :] = x2[r] * y


def _hyena(z, z_rb0, conv_w, conv_b, skip, filt, fmat, gmat, tw, *, layer, seq):
    hb, asum, halt = filt
    q = seq // HY_PHASES
    kf = min(HY_KF, q)
    n_ch = q // kf
    n_cols = HY_WIDTH // HY_COLS
    ew = HEAD_DIM
    sub = HY_COLS // ew
    n_ew = HY_WIDTH // ew
    hy0 = V_END // ew

    def zcols(which):
        return pl.BlockSpec((seq, ew), lambda b, e: (z_rb0 + b, hy0 + which * n_ew + e))

    def cw(which):
        return pl.BlockSpec((None, HY_SHORT, ew), lambda b, e: (layer, 0, which * n_ew + e))

    def cb(which):
        return pl.BlockSpec((None, 1, ew), lambda b, e: (layer, 0, which * n_ew + e))

    def fstat(order, direction, rows):
        return pl.BlockSpec((None, rows, ew),
                            lambda b, e: (layer, 0, ((order * n_cols + e // sub) * 2 + direction) * sub + e % sub))

    def skip_spec(order):
        return pl.BlockSpec((None, 1, ew), lambda b, e: (layer * HY_ORDER + order, 0, e))

    def act_col(b, e):
        return ((e // sub) * BATCH + b) * sub + e % sub

    seq_blk = pl.BlockSpec((seq, ew), lambda b, e: (0, act_col(b, e)))
    alt_spec = pl.BlockSpec((HY_PHASES, ew), lambda b, e: (0, act_col(b, e)))
    ew_params = _params("arbitrary", "arbitrary")
    act = jax.ShapeDtypeStruct((seq, BATCH * HY_WIDTH), F32)
    act_b = jax.ShapeDtypeStruct((seq, BATCH * HY_WIDTH), BF16)
    alt_shape = jax.ShapeDtypeStruct((HY_PHASES, BATCH * HY_WIDTH), F32)

    def spectral(ub, order):
        return pl.pallas_call(
            functools.partial(_hy_spec_kernel, seq=seq, kf=kf),
            grid=(n_cols, n_ch),
            in_specs=[
                pl.BlockSpec((seq, BATCH * HY_COLS), lambda j, c: (0, j)),
                pl.BlockSpec((None, seq, 2 * HY_COLS), lambda j, c: (layer, 0, order * n_cols + j),
                             pipeline_mode=pl.Buffered(1)),
                pl.BlockSpec((None, 1, 2 * HY_COLS), lambda j, c: (layer, 0, order * n_cols + j)),
                pl.BlockSpec((None, 2 * kf, q), lambda j, c: (c, 0, 0)),
                pl.BlockSpec((None, q, 2 * kf), lambda j, c: (c, 0, 0)),
                pl.BlockSpec((tw.shape[0], kf, HEAD_DIM), lambda j, c: (0, c, 0)),
            ],
            out_specs=pl.BlockSpec((seq, BATCH * HY_COLS), lambda j, c: (0, j)),
            out_shape=act,
            compiler_params=_params("arbitrary", "arbitrary"),
            name=f"hy_spec{order}_l{seq}",
        )(ub, hb, asum, fmat, gmat, tw)

    ub, ualt = pl.pallas_call(
        functools.partial(_hy_pre_kernel, seq=seq),
        grid=(BATCH, n_ew),
        in_specs=[zcols(0), cw(0), cb(0)],
        out_specs=[seq_blk, alt_spec],
        out_shape=[act_b, alt_shape],
        compiler_params=ew_params,
        name=f"hy_pre_l{seq}",
    )(z, conv_w, conv_b)
    y0 = spectral(ub, 0)
    s, sb, salt = pl.pallas_call(
        functools.partial(_hy_mid_kernel, seq=seq),
        grid=(BATCH, n_ew),
        in_specs=[zcols(0), zcols(1), seq_blk, alt_spec, fstat(0, 0, HY_PHASES), fstat(0, 1, HY_PHASES), fstat(0, 0, 1),
                  fstat(0, 1, 1), cw(0), cb(0), cw(1), cb(1), skip_spec(0)],
        out_specs=[seq_blk, seq_blk, alt_spec],
        out_shape=[act, act_b, alt_shape],
        compiler_params=ew_params,
        name=f"hy_mid_l{seq}",
    )(z, z, y0, ualt, halt, halt, asum, asum, conv_w, conv_b, conv_w, conv_b, skip)
    y1 = spectral(sb, 1)
    return pl.pallas_call(
        functools.partial(_hy_post_kernel, seq=seq),
        grid=(BATCH, n_ew),
        in_specs=[zcols(2), seq_blk, seq_blk, alt_spec, fstat(1, 0, HY_PHASES), fstat(1, 1, HY_PHASES),
                  fstat(1, 0, 1), fstat(1, 1, 1), cw(2), cb(2), skip_spec(1)],
        out_specs=pl.BlockSpec((seq, ew), lambda b, e: (b, e)),
        out_shape=jax.ShapeDtypeStruct((BATCH * seq, HY_WIDTH), F32),
        compiler_params=ew_params,
        name=f"hy_post_l{seq}",
    )(z, y1, s, salt, halt, halt, asum, asum, conv_w, conv_b, skip)


def _outproj_kernel(a_ref, b_ref, c_ref, cc_ref, w_ref, x_ref, mod_ref, g_ref, o_ref, *, lat_tiles):
    c = c_ref[...]
    if cc_ref is not None:
        c = jnp.where(pl.program_id(0) < lat_tiles, c, cc_ref[...])
    y = jnp.dot(a_ref[...], w_ref[:SGU_WIDTH, :], preferred_element_type=F32)
    y += jnp.dot(b_ref[...], w_ref[SGU_WIDTH:SGU_WIDTH + ATTN_WIDTH, :], preferred_element_type=F32)
    y += jnp.dot(c.astype(BF16), w_ref[SGU_WIDTH + ATTN_WIDTH:, :], preferred_element_type=F32)
    o_ref[...] = x_ref[...] + mod_ref[5:6, :] * _rms(y, g_ref[3:4, :])


def _outproj(a, b, c_lat, c_ctx, w_out, x, mods, norm_g, *, layer, n_tiles):
    lat_tiles = N_LAT // TM
    c_specs = [pl.BlockSpec((TM, HY_WIDTH), lambda i: (jnp.minimum(i, lat_tiles - 1), 0))]
    c_args = [c_lat]
    if c_ctx is not None:
        c_specs.append(pl.BlockSpec((TM, HY_WIDTH), lambda i: (jnp.maximum(i - lat_tiles, 0), 0)))
        c_args.append(c_ctx)

    def body(a_ref, b_ref, c_ref, *refs):
        cc_ref, rest = (refs[0], refs[1:]) if c_ctx is not None else (None, refs)
        _outproj_kernel(a_ref, b_ref, c_ref, cc_ref, *rest, lat_tiles=lat_tiles)

    return pl.pallas_call(
        body,
        grid=(n_tiles,),
        in_specs=[
            pl.BlockSpec((TM, SGU_WIDTH), lambda i: (i, 0)),
            pl.BlockSpec((TM, ATTN_WIDTH), lambda i: (i, 0)),
            *c_specs,
            pl.BlockSpec((D_MIX, D_MODEL), lambda i: (0, 0), pipeline_mode=pl.Buffered(1)),
            pl.BlockSpec((TM, D_MODEL), lambda i: (i, 0)),
            pl.BlockSpec((None, None, N_MOD, D_MODEL), lambda i: (layer, _mod_row(i), 0, 0)),
            pl.BlockSpec((None, 6, D_MODEL), lambda i: (layer, 0, 0)),
        ],
        out_specs=pl.BlockSpec((TM, D_MODEL), lambda i: (i, 0)),
        out_shape=jax.ShapeDtypeStruct((n_tiles * TM, D_MODEL), F32),
        compiler_params=_params("arbitrary"),
        name="outproj",
    )(a, b, *c_args, w_out, x, mods, norm_g)


def _rope_tables():
    rows = SEQ // GRID_W
    row = jnp.broadcast_to(jnp.arange(rows)[:, None], (rows, GRID_W)).reshape(SEQ).astype(F32)
    col = jnp.broadcast_to(jnp.arange(GRID_W)[None, :], (rows, GRID_W)).reshape(SEQ).astype(F32)
    half = HEAD_DIM // 2
    inv = ROPE_BASE ** (-jnp.arange(0, half, 2, dtype=F32) / half)
    ang_r = row[:, None] * inv
    ang_c = col[:, None] * inv
    ang = jnp.concatenate([ang_r, ang_r, ang_c, ang_c], axis=-1)
    cos, sin = jnp.cos(ang), jnp.sin(ang)
    first = (jnp.arange(HEAD_DIM) % half) < (half // 2)
    sa = jnp.where(first, -sin, 0.0)
    sb = jnp.where(first, 0.0, sin)

    def stack(lat, ctx_value):
        return jnp.concatenate([lat] * BATCH + [jnp.full((N_CTX, HEAD_DIM), ctx_value, F32)], axis=0)

    return stack(cos, 1.0), stack(sa, 0.0), stack(sb, 0.0)


def kernel(x, c, ctx, c_ctx, w_ada, b_ada, norm_g, ffn1_gate, ffn1_up, ffn1_down, w_in,
           sgu_g, sgu_w, sgu_b, attn_sink, hy_conv_w, hy_conv_b, hy_f_w1, hy_f_b1, hy_f_w2,
           hy_f_b2, hy_f_w3, hy_freq, hy_skip, w_out, ffn2_gate, ffn2_up, ffn2_down):
    xs = jnp.concatenate([x.reshape(N_LAT, D_MODEL), ctx.reshape(N_CTX, D_MODEL)], axis=0)
    cc = jnp.concatenate([c, c_ctx[None], jnp.zeros((MOD_ROWS - BATCH - 1, D_MODEL), F32)], axis=0)
    mods = _ada_mods(cc, w_ada, b_ada)

    w1 = [w[0].astype(BF16) for w in (ffn1_gate, ffn1_up, ffn1_down)]
    cos, sa, sb = _rope_tables()
    conv_b = hy_conv_b.reshape(DEPTH, 1, (HY_ORDER + 1) * HY_WIDTH)
    skip = hy_skip.reshape(DEPTH * HY_ORDER, 1, HY_WIDTH)
    filt_w = (hy_f_w1, hy_f_b1, hy_f_w2, hy_f_b2, hy_f_w3, hy_freq)
    filt_lat = _hyena_filters(SEQ, *filt_w)
    filt_ctx = _hyena_filters(CTX_LEN, *filt_w)
    f_lat, g_lat = _dft_matrices(SEQ // HY_PHASES, min(HY_KF, SEQ // HY_PHASES))
    f_ctx, g_ctx = _dft_matrices(CTX_LEN // HY_PHASES, min(HY_KF, CTX_LEN // HY_PHASES))
    tw_lat, tw_ctx = _twiddles(SEQ), _twiddles(CTX_LEN)

    all_tiles = N_TOK // TM
    lat_tiles = N_LAT // TM
    ctx_rb = N_LAT // CTX_LEN

    for i in range(DEPTH):
        last = i == DEPTH - 1
        n_tiles = lat_tiles if last else all_tiles
        n_rows = n_tiles * TM

        jobs = [(w, i) for w in (ffn2_gate, ffn2_up, ffn2_down, w_in, w_out)]
        xs, (w2g, w2u, w2d, w_in_b, w_out_b) = _ffn(xs, mods, norm_g, *w1, jobs, layer=i, k=0, n_tiles=all_tiles)
        z = _inproj(xs, mods, norm_g, w_in_b, cos, sa, sb, layer=i)

        a_out = _sgu(z, sgu_g, sgu_w, sgu_b, layer=i, n_rows=n_rows)
        b_out = _attention(z, attn_sink, layer=i, n_rows=n_rows)
        c_lat = _hyena(z, 0, hy_conv_w, conv_b, skip, filt_lat, f_lat, g_lat, tw_lat, layer=i, seq=SEQ)
        c_ctx = None
        if not last:
            c_ctx = _hyena(z, ctx_rb, hy_conv_w, conv_b, skip, filt_ctx, f_ctx, g_ctx, tw_ctx, layer=i, seq=CTX_LEN)

        xs = _outproj(a_out, b_out, c_lat, c_ctx, w_out_b, xs, mods, norm_g, layer=i, n_tiles=n_tiles)
        jobs = [] if last else [(w, i + 1) for w in (ffn1_gate, ffn1_up, ffn1_down)]
        xs, w1 = _ffn(xs, mods, norm_g, w2g, w2u, w2d, jobs, layer=i, k=2, n_tiles=n_tiles)

    return xs[:N_LAT].reshape(BATCH, SEQ, D_MODEL)
```

```python
import functools
import math

import jax
import jax.numpy as jnp
from jax import lax
from jax.experimental import pallas as pl
from jax.experimental.pallas import tpu as pltpu

D_MODEL = 2048
BATCH = 2
SEQ = 4096
DEPTH = 4
GRID_W = 64
CTX_LEN = 256
HEAD_DIM = 128
NORM_EPS = 1e-6
N_MOD = 9
D_FF = 5632
SGU_GROUPS = 4
SGU_WIDTH = SGU_GROUPS * HEAD_DIM
CHUNK = 128
N_Q_HEADS = 8
N_KV_HEADS = 2
Q_GROUP = N_Q_HEADS // N_KV_HEADS
ATTN_WIDTH = N_Q_HEADS * HEAD_DIM
KV_WIDTH = N_KV_HEADS * HEAD_DIM
WINDOW = 128
BLOCK = 128
ROPE_BASE = 10000.0
NEG_INF = -1e30
HY_GROUPS = 4
HY_WIDTH = HY_GROUPS * HEAD_DIM
HY_ORDER = 2
HY_SHORT = 3
HY_BANDS = 16
HY_EMB = 1 + 2 * HY_BANDS
HY_HIDDEN = 64
HY_TARGET = 1e-2
HY_FAST_DECAY = 0.3
HY_SLOW_DECAY = 1.5
D_MIX = SGU_WIDTH + ATTN_WIDTH + HY_WIDTH
A_END = 2 * SGU_WIDTH
Q_END = A_END + ATTN_WIDTH
K_END = Q_END + KV_WIDTH
V_END = K_END + KV_WIDTH
IN_COLS = V_END + (HY_ORDER + 1) * HY_WIDTH

N_LAT = BATCH * SEQ
N_CTX = BATCH * CTX_LEN
N_TOK = N_LAT + N_CTX
MOD_ROWS = 8

TM = 512
TF = 512
FFN_ROWS = 16
BF16_SUBLANES = 16
TN_IN = 1024
TN_ADA = 1024
HY_COLS = 256
HY_FILT_COLS = 2 * HY_COLS
HY_KF = 256
HY_LEVELS = 3
HY_PHASES = 2 ** HY_LEVELS
HY_ROWS = 512
HY_INV_ROWS = 1024
HY_EMB_PAD = 128
VMEM_LIMIT = 56 * 1024 * 1024

F32 = jnp.float32
BF16 = jnp.bfloat16
LOG2_E = math.log2(math.e)


def _params(*sem):
    return pltpu.CompilerParams(dimension_semantics=sem, vmem_limit_bytes=VMEM_LIMIT)


def _mod_row(i):
    return jnp.minimum(i // (SEQ // TM), BATCH)


def _silu(a):
    return a / (1.0 + jnp.exp(-a))


def _rms(x, g):
    return x * lax.rsqrt(jnp.mean(x * x, axis=-1, keepdims=True) + NORM_EPS) * g


def _ada_kernel(c_ref, w_ref, b_ref, o_ref):
    s = _silu(c_ref[...]).astype(BF16)
    o_ref[...] = jnp.dot(s, w_ref[...].astype(BF16), preferred_element_type=F32) + b_ref[...]


def _ada_mods(cc, w_ada, b_ada):
    n_out = N_MOD * D_MODEL
    out = pl.pallas_call(
        _ada_kernel,
        grid=(DEPTH, n_out // TN_ADA),
        in_specs=[
            pl.BlockSpec((MOD_ROWS, D_MODEL), lambda l, j: (0, 0)),
            pl.BlockSpec((None, D_MODEL, TN_ADA), lambda l, j: (l, 0, j)),
            pl.BlockSpec((None, 1, TN_ADA), lambda l, j: (l, 0, j)),
        ],
        out_specs=pl.BlockSpec((None, MOD_ROWS, TN_ADA), lambda l, j: (l, 0, j)),
        out_shape=jax.ShapeDtypeStruct((DEPTH, MOD_ROWS, n_out), F32),
        compiler_params=_params("arbitrary", "arbitrary"),
        name="ada_mods",
    )(cc, w_ada, b_ada.reshape(DEPTH, 1, n_out))
    return out.reshape(DEPTH, MOD_ROWS, N_MOD, D_MODEL)


def _ffn_kernel(*refs, k, nf, n_jobs):
    x_ref, mod_ref, g_ref, wg_ref, wu_ref, wd_ref = refs[:6]
    cast_src = refs[6:6 + n_jobs]
    o_ref = refs[6 + n_jobs]
    cast_dst = refs[7 + n_jobs:7 + 2 * n_jobs]
    h_ref, acc_ref = refs[7 + 2 * n_jobs:]
    f = pl.program_id(1)

    for src, dst in zip(cast_src, cast_dst):
        dst[...] = src[...].astype(BF16)

    @pl.when(f == 0)
    def _():
        g = g_ref[2 * k:2 * k + 1, :]
        m1 = 1.0 + mod_ref[3 * k:3 * k + 1, :]
        m2 = mod_ref[3 * k + 1:3 * k + 2, :]
        for r0 in range(0, TM, FFN_ROWS):
            rs = slice(r0, r0 + FFN_ROWS)
            h_ref[rs, :] = (_rms(x_ref[rs, :], g) * m1 + m2).astype(BF16)
        acc_ref[...] = jnp.zeros_like(acc_ref)

    h = h_ref[...]
    a = jnp.dot(h, wg_ref[...], preferred_element_type=F32)
    b = jnp.dot(h, wu_ref[...], preferred_element_type=F32)
    p = (_silu(a) * b).astype(BF16)
    acc_ref[...] += jnp.dot(p, wd_ref[...], preferred_element_type=F32)

    @pl.when(f == nf - 1)
    def _():
        g = g_ref[2 * k + 1:2 * k + 2, :]
        gate = 0.5 * mod_ref[3 * k + 2:3 * k + 3, :]
        for r0 in range(0, TM, FFN_ROWS):
            rs = slice(r0, r0 + FFN_ROWS)
            o_ref[rs, :] = x_ref[rs, :] + gate * _rms(acc_ref[rs, :], g)


def _ffn(x, mods, norm_g, wg, wu, wd, cast_jobs, *, layer, k, n_tiles):
    nf = D_FF // TF
    n_steps = n_tiles * nf
    in_specs = [
        pl.BlockSpec((TM, D_MODEL), lambda i, f: (i, 0)),
        pl.BlockSpec((None, None, N_MOD, D_MODEL), lambda i, f: (layer, _mod_row(i), 0, 0)),
        pl.BlockSpec((None, 6, D_MODEL), lambda i, f: (layer, 0, 0)),
        pl.BlockSpec((D_MODEL, TF), lambda i, f: (0, f)),
        pl.BlockSpec((D_MODEL, TF), lambda i, f: (0, f)),
        pl.BlockSpec((TF, D_MODEL), lambda i, f: (f, 0)),
    ]
    out_specs = [pl.BlockSpec((TM, D_MODEL), lambda i, f: (i, 0))]
    out_shape = [jax.ShapeDtypeStruct((n_tiles * TM, D_MODEL), F32)]
    for w, src_layer in cast_jobs:
        n_rows, n_cols = w.shape[1:]
        rb = BF16_SUBLANES * pl.cdiv(pl.cdiv(n_rows, n_steps), BF16_SUBLANES)
        assert n_rows % rb == 0
        last_blk = n_rows // rb - 1

        def step_block(i, f, last_blk=last_blk):
            return jnp.minimum(i * nf + f, last_blk)

        in_specs.append(pl.BlockSpec((None, rb, n_cols),
                                     lambda i, f, sl=src_layer, sb=step_block: (sl, sb(i, f), 0)))
        out_specs.append(pl.BlockSpec((rb, n_cols), lambda i, f, sb=step_block: (sb(i, f), 0)))
        out_shape.append(jax.ShapeDtypeStruct((n_rows, n_cols), BF16))
    outs = pl.pallas_call(
        functools.partial(_ffn_kernel, k=k, nf=nf, n_jobs=len(cast_jobs)),
        grid=(n_tiles, nf),
        in_specs=in_specs,
        out_specs=out_specs,
        out_shape=out_shape,
        scratch_shapes=[pltpu.VMEM((TM, D_MODEL), BF16), pltpu.VMEM((TM, D_MODEL), F32)],
        compiler_params=_params("arbitrary", "arbitrary"),
        name=f"ffn{k}",
    )(x, mods, norm_g, wg, wu, wd, *[w for w, _ in cast_jobs])
    return outs[0], list(outs[1:])


def _rope(t, cos, sa, sb):
    return t * cos + pltpu.roll(t, 3 * HEAD_DIM // 4, 1) * sa + pltpu.roll(t, HEAD_DIM // 4, 1) * sb


def _inproj_kernel(x_ref, mod_ref, g_ref, w_ref, cos_ref, sa_ref, sb_ref, o_ref):
    y = _rms(x_ref[...], g_ref[2:3, :])
    h = (y * (1.0 + mod_ref[3:4, :]) + mod_ref[4:5, :]).astype(BF16)
    cos, sa, sb = cos_ref[...], sa_ref[...], sb_ref[...]
    n_rope = (K_END - A_END) // HEAD_DIM
    for n in range(IN_COLS // TN_IN):
        c0 = n * TN_IN
        z = jnp.dot(h, w_ref[:, c0:c0 + TN_IN], preferred_element_type=F32)
        for hd in range(TN_IN // HEAD_DIM):
            col = c0 + hd * HEAD_DIM
            zs = z[:, hd * HEAD_DIM:(hd + 1) * HEAD_DIM]
            if A_END <= col < A_END + n_rope * HEAD_DIM:
                zs = _rope(zs, cos, sa, sb)
            o_ref[:, col:col + HEAD_DIM] = zs


def _inproj(x, mods, norm_g, w_in, cos, sa, sb, *, layer):
    n_tiles = N_TOK // TM
    tab = pl.BlockSpec((TM, HEAD_DIM), lambda i: (i, 0))
    return pl.pallas_call(
        _inproj_kernel,
        grid=(n_tiles,),
        in_specs=[
            pl.BlockSpec((TM, D_MODEL), lambda i: (i, 0)),
            pl.BlockSpec((None, None, N_MOD, D_MODEL), lambda i: (layer, _mod_row(i), 0, 0)),
            pl.BlockSpec((None, 6, D_MODEL), lambda i: (layer, 0, 0)),
            pl.BlockSpec((D_MODEL, IN_COLS), lambda i: (0, 0), pipeline_mode=pl.Buffered(1)),
            tab, tab, tab,
        ],
        out_specs=pl.BlockSpec((TM, IN_COLS), lambda i: (i, 0)),
        out_shape=jax.ShapeDtypeStruct((N_TOK, IN_COLS), F32),
        compiler_params=_params("arbitrary"),
        name="inproj",
    )(x, mods, norm_g, w_in, cos, sa, sb)


def _sgu_kernel(z_ref, g_ref, w_ref, b_ref, o_ref, *, rows):
    inv_sqrt2 = 1.0 / math.sqrt(2.0)
    for c in range(rows // CHUNK):
        rs = slice(c * CHUNK, (c + 1) * CHUNK)
        a = z_ref[rs, :]
        a = a * (lax.erf(a * inv_sqrt2) + 1.0) * 0.5
        u, v = a[:, :SGU_WIDTH], a[:, SGU_WIDTH:]
        mu = jnp.mean(v, axis=-1, keepdims=True)
        vc = v - mu
        var = jnp.mean(vc * vc, axis=-1, keepdims=True)
        v = (vc * lax.rsqrt(var + NORM_EPS) * g_ref[...]).astype(BF16)
        for g in range(SGU_GROUPS):
            cs = slice(g * HEAD_DIM, (g + 1) * HEAD_DIM)
            mixed = jnp.dot(w_ref[g].astype(BF16), v[:, cs], preferred_element_type=F32) + b_ref[:, g:g + 1]
            o_ref[rs, cs] = (u[:, cs] * mixed).astype(o_ref.dtype)


def _sgu(z, sgu_g, sgu_w, sgu_b, *, layer, n_rows):
    rows = TM
    return pl.pallas_call(
        functools.partial(_sgu_kernel, rows=rows),
        grid=(n_rows // rows,),
        in_specs=[
            pl.BlockSpec((rows, A_END), lambda i: (i, 0)),
            pl.BlockSpec((None, 1, SGU_WIDTH), lambda i: (layer, 0, 0)),
            pl.BlockSpec((None, SGU_GROUPS, CHUNK, CHUNK), lambda i: (layer, 0, 0, 0)),
            pl.BlockSpec((None, CHUNK, SGU_GROUPS), lambda i: (layer, 0, 0)),
        ],
        out_specs=pl.BlockSpec((rows, SGU_WIDTH), lambda i: (i, 0)),
        out_shape=jax.ShapeDtypeStruct((n_rows, SGU_WIDTH), BF16),
        compiler_params=_params("arbitrary"),
        name="sgu",
    )(z, sgu_g.reshape(DEPTH, 1, SGU_WIDTH), sgu_w, jnp.swapaxes(sgu_b, 1, 2))


def _attn_kernel(sink_ref, q_ref, kp_ref, kc_ref, kn_ref, vp_ref, vc_ref, vn_ref, kx_ref, vx_ref,
                 o_ref, *, layer, nb, n_lat_blocks):
    s_id = pl.program_id(0)
    is_lat = s_id < n_lat_blocks
    n = s_id % nb
    n_keys = CTX_LEN + 3 * BLOCK
    n_rows = Q_GROUP * BLOCK
    row_id = lax.broadcasted_iota(jnp.int32, (n_rows, 1), 0)
    row = lax.broadcasted_iota(jnp.int32, (n_rows, n_keys), 0) & (BLOCK - 1)
    col = lax.broadcasted_iota(jnp.int32, (n_rows, n_keys), 1)
    rel = col - (CTX_LEN + BLOCK)
    lo = jnp.where(is_lat, -n * BLOCK, 4 * BLOCK)
    hi = jnp.where(is_lat, (nb - n) * BLOCK, -4 * BLOCK)
    dist = rel - row
    valid = (col < CTX_LEN) | ((dist >= -WINDOW) & (dist <= WINDOW) & (rel >= lo) & (rel < hi))
    scale = HEAD_DIM ** -0.5
    nt = (((1,), (1,)), ((), ()))
    for kh in range(N_KV_HEADS):
        ks = slice(kh * HEAD_DIM, (kh + 1) * HEAD_DIM)
        keys = jnp.concatenate([kx_ref[:, ks], kp_ref[:, ks], kc_ref[:, ks], kn_ref[:, ks]], axis=0).astype(BF16)
        vals = jnp.concatenate([vx_ref[:, ks], vp_ref[:, ks], vc_ref[:, ks], vn_ref[:, ks]], axis=0).astype(BF16)
        heads = [kh * Q_GROUP + g for g in range(Q_GROUP)]
        q = jnp.concatenate([q_ref[:, hd * HEAD_DIM:(hd + 1) * HEAD_DIM] for hd in heads], axis=0).astype(BF16)
        sink = jnp.full((n_rows, 1), sink_ref[layer, heads[0]], F32)
        for g, hd in enumerate(heads[1:], start=1):
            sink = jnp.where(row_id >= g * BLOCK, sink_ref[layer, hd], sink)
        s = lax.dot_general(q, keys, nt, preferred_element_type=F32)
        s = jnp.where(valid, s, NEG_INF)
        m = jnp.maximum(jnp.max(s, axis=-1, keepdims=True) * scale, sink)
        p = jnp.exp2(s * (scale * LOG2_E) - m * LOG2_E)
        denom = jnp.sum(p, axis=-1, keepdims=True) + jnp.exp2((sink - m) * LOG2_E)
        out = (jnp.dot(p.astype(BF16), vals, preferred_element_type=F32) / denom).astype(o_ref.dtype)
        for g, hd in enumerate(heads):
            o_ref[:, hd * HEAD_DIM:(hd + 1) * HEAD_DIM] = out[g * BLOCK:(g + 1) * BLOCK]


def _attention(z, attn_sink, *, layer, n_rows):
    nb = SEQ // BLOCK
    n_lat_blocks = N_LAT // BLOCK
    n_blocks = n_rows // BLOCK
    ctx_blocks = CTX_LEN // BLOCK
    q_col = A_END // ATTN_WIDTH
    k_col = Q_END // KV_WIDTH
    v_col = K_END // KV_WIDTH

    def lat(s):
        return s < n_lat_blocks

    def band(off, col):
        def index(s):
            b = s // nb
            nbr = jnp.clip(s % nb + off, 0, nb - 1) + b * nb
            return (jnp.where(lat(s), nbr, s), col)
        return pl.BlockSpec((BLOCK, KV_WIDTH), index)

    def ctx(col):
        def index(s):
            b = jnp.where(lat(s), s // nb, (s - n_lat_blocks) // ctx_blocks)
            return (N_LAT // CTX_LEN + b, col)
        return pl.BlockSpec((CTX_LEN, KV_WIDTH), index)

    return pl.pallas_call(
        functools.partial(_attn_kernel, layer=layer, nb=nb, n_lat_blocks=n_lat_blocks),
        grid=(n_blocks,),
        in_specs=[
            pl.BlockSpec(memory_space=pltpu.SMEM),
            pl.BlockSpec((BLOCK, ATTN_WIDTH), lambda s: (s, q_col)),
            band(-1, k_col), band(0, k_col), band(1, k_col),
            band(-1, v_col), band(0, v_col), band(1, v_col),
            ctx(k_col), ctx(v_col),
        ],
        out_specs=pl.BlockSpec((BLOCK, ATTN_WIDTH), lambda s: (s, 0)),
        out_shape=jax.ShapeDtypeStruct((n_rows, ATTN_WIDTH), BF16),
        compiler_params=_params("arbitrary"),
        name="attn",
    )(attn_sink, z, z, z, z, z, z, z, z, z)


def _short_conv_phases(ref, w_ref, b_ref, m0, rows, n_m):
    ph = HY_PHASES
    z = [ref[pl.ds(ph * m0 + r, rows, stride=ph), :] for r in range(ph)]
    m = lax.broadcasted_iota(jnp.int32, (rows, 1), 0) + m0
    if m0 == 0:
        z_prev = jnp.where(m == 0, 0.0, pltpu.roll(z[ph - 1], 1, 0))
    else:
        z_prev = ref[pl.ds(ph * m0 - 1, rows, stride=ph), :]
    if m0 + rows == n_m:
        z_next = jnp.where(m == n_m - 1, 0.0, pltpu.roll(z[0], rows - 1, 0))
    else:
        z_next = ref[pl.ds(ph * m0 + ph, rows, stride=ph), :]
    ext = [z_prev] + z + [z_next]
    w0, w1, w2, b = w_ref[0:1, :], w_ref[1:2, :], w_ref[2:3, :], b_ref[...]
    return [ext[r] * w0 + ext[r + 1] * w1 + ext[r + 2] * w2 + b for r in range(ph)]


def _cmul(a, b):
    return a[0] * b[0] - a[1] * b[1], a[0] * b[1] + a[1] * b[0]


def _cadd(a, b):
    return a[0] + b[0], a[1] + b[1]


def _csub(a, b):
    return a[0] - b[0], a[1] - b[1]


def _conj(a):
    return a[0], -a[1]


def _split_fwd(e, o, w):
    t = _cmul(w, o)
    return _cadd(e, t), _conj(_csub(e, t))


def _split_inv(pp, pm, w):
    cm = _conj(pm)
    return _cadd(pp, cm), _cmul(_csub(pp, cm), _conj(w))


def _alt_sign(r0, rows):
    t = lax.broadcasted_iota(jnp.int32, (rows, 1), 0) + r0
    return jnp.where((t & 1) == 0, 1.0, -1.0)


def _hy_chunks(seq, chunk=None):
    rows = min(chunk or HY_ROWS, seq)
    return [(r * rows, rows) for r in range(seq // rows)]


def _filt_kernel(feat_ref, w1_ref, b1_ref, w2_ref, b2_ref, w3_ref, fr_ref, dl_ref,
                 hb_ref, asum_ref, hnyq_ref, hd_ref, *, seq):
    hp = lax.Precision.HIGHEST

    @pl.when(pl.program_id(1) == 0)
    def _():
        fr = fr_ref[...]
        hd = jnp.sin(fr * (jnp.dot(feat_ref[...], w1_ref[...], precision=hp, preferred_element_type=F32)
                           + b1_ref[...]))
        hd_ref[...] = jnp.sin(fr * (jnp.dot(hd, w2_ref[...], precision=hp, preferred_element_type=F32)
                                    + b2_ref[...]))

    q = seq // HY_PHASES
    asum = jnp.zeros((1, HY_FILT_COLS), F32)
    alt = [jnp.zeros((1, HY_FILT_COLS), F32) for _ in range(HY_PHASES)]
    for r0, rows in _hy_chunks(seq, min(HY_ROWS, q)):
        rs = slice(r0, r0 + rows)
        h = jnp.dot(hd_ref[rs, :].astype(BF16), w3_ref[...].astype(BF16), preferred_element_type=F32)
        h = h * jnp.exp(-feat_ref[rs, 0:1] * dl_ref[...])
        asum = asum + jnp.sum(jnp.abs(h), axis=0, keepdims=True)
        alt[r0 // q] = alt[r0 // q] + jnp.sum(h * _alt_sign(r0 % q, rows), axis=0, keepdims=True)
        hb_ref[rs, :] = h.astype(BF16)
    asum_ref[...] = asum
    for r in range(HY_PHASES):
        hnyq_ref[r:r + 1, :] = alt[r]


def _hyena_filters(seq, w1, b1, w2, b2, w3, freq):
    n_tiles = HY_WIDTH // HY_COLS
    w3 = w3.reshape(DEPTH, HY_HIDDEN, HY_ORDER, 2, n_tiles, HY_COLS)
    w3 = jnp.swapaxes(w3, 3, 4).reshape(DEPTH, HY_HIDDEN, HY_ORDER * 2 * HY_WIDTH)
    t = jnp.linspace(0.0, 1.0, seq, dtype=F32)[:, None]
    w = (2.0 * math.pi / seq) * jnp.arange(seq, dtype=F32)[:, None]
    bands = jnp.linspace(1e-4, HY_BANDS - 1, HY_BANDS, dtype=F32)[None, :]
    feat = jnp.concatenate([t, jnp.cos(bands * w), -jnp.sin(bands * w)], axis=-1)
    feat = jnp.pad(feat, ((0, 0), (0, HY_EMB_PAD - HY_EMB)))
    feat = jnp.concatenate([feat[r::HY_PHASES] for r in range(HY_PHASES)], axis=0)
    w1p = jnp.pad(w1, ((0, 0), (0, HY_EMB_PAD - HY_EMB), (0, 0)))
    deltas = jnp.abs(jnp.linspace(math.log(HY_TARGET) / HY_FAST_DECAY,
                                  math.log(HY_TARGET) / HY_SLOW_DECAY, HY_WIDTH, dtype=F32))
    deltas = jnp.broadcast_to(deltas.reshape(1, n_tiles, 1, HY_COLS), (HY_ORDER, n_tiles, 2, HY_COLS))
    deltas = deltas.reshape(1, HY_ORDER * 2 * HY_WIDTH)
    n_cols = HY_ORDER * 2 * HY_WIDTH

    def row(a):
        return a.reshape(DEPTH, 1, HY_HIDDEN)

    vec = pl.BlockSpec((None, 1, HY_HIDDEN), lambda l, j: (l, 0, 0))
    stat = pl.BlockSpec((None, 1, HY_FILT_COLS), lambda l, j: (l, 0, j))
    return pl.pallas_call(
        functools.partial(_filt_kernel, seq=seq),
        grid=(DEPTH, n_cols // HY_FILT_COLS),
        in_specs=[
            pl.BlockSpec((seq, HY_EMB_PAD), lambda l, j: (0, 0)),
            pl.BlockSpec((None, HY_EMB_PAD, HY_HIDDEN), lambda l, j: (l, 0, 0)),
            vec,
            pl.BlockSpec((None, HY_HIDDEN, HY_HIDDEN), lambda l, j: (l, 0, 0)),
            vec,
            pl.BlockSpec((None, HY_HIDDEN, HY_FILT_COLS), lambda l, j: (l, 0, j)),
            vec,
            pl.BlockSpec((1, HY_FILT_COLS), lambda l, j: (0, j)),
        ],
        out_specs=[pl.BlockSpec((None, seq, HY_FILT_COLS), lambda l, j: (l, 0, j)), stat,
                   pl.BlockSpec((None, HY_PHASES, HY_FILT_COLS), lambda l, j: (l, 0, j))],
        out_shape=[jax.ShapeDtypeStruct((DEPTH, seq, n_cols), BF16),
                   jax.ShapeDtypeStruct((DEPTH, 1, n_cols), F32),
                   jax.ShapeDtypeStruct((DEPTH, HY_PHASES, n_cols), F32)],
        scratch_shapes=[pltpu.VMEM((seq, HY_HIDDEN), F32)],
        compiler_params=_params("arbitrary", "arbitrary"),
        name=f"hy_filt_l{seq}",
    )(feat, w1p, row(b1), w2, row(b2), w3, row(freq), deltas)


def _dft_matrices(seq, kf):
    n_fft = 2 * seq
    n_ch = seq // kf
    th = 2.0 * math.pi / n_fft
    c = jnp.arange(n_ch, dtype=jnp.int32)[:, None]
    kl = jnp.arange(kf, dtype=jnp.int32)[:, None]
    n = jnp.arange(seq, dtype=jnp.int32)[None, :]
    a1 = th * ((c * kf * n) % n_fft).astype(F32)
    a2 = th * ((kl * n) % n_fft).astype(F32)
    c1, s1, c2, s2 = jnp.cos(a1), jnp.sin(a1), jnp.cos(a2), jnp.sin(a2)
    f_cos = c1[:, None, :] * c2[None, :, :] - s1[:, None, :] * s2[None, :, :]
    f_sin = s1[:, None, :] * c2[None, :, :] + c1[:, None, :] * s2[None, :, :]
    fmat = jnp.concatenate([f_cos, -f_sin], axis=1).astype(BF16)
    c2t, s2t = c2.T, s2.T
    g_cos = c1[:, :, None] * c2t[None, :, :] - s1[:, :, None] * s2t[None, :, :]
    g_sin = s1[:, :, None] * c2t[None, :, :] + c1[:, :, None] * s2t[None, :, :]
    gmat = jnp.concatenate([g_cos, -g_sin], axis=2).astype(BF16)
    return fmat, gmat


def _twiddle_plan(seq):
    freqs = [(0, 1)]
    plan = [None] * HY_LEVELS
    for d in reversed(range(HY_LEVELS)):
        l_d = seq >> d
        plan[d] = [(off, sgn, 2 * l_d) for off, sgn in freqs]
        freqs = [f for off, sgn in freqs for f in ((off, sgn), (l_d - off, -sgn))]
    index = 0
    for d in range(HY_LEVELS):
        plan[d] = [(index + j,) + t for j, t in enumerate(plan[d])]
        index += len(plan[d])
    return plan


def _twiddles(seq):
    k = jnp.arange(seq // HY_PHASES, dtype=F32)[:, None]
    tabs = []
    for level in _twiddle_plan(seq):
        for _, off, sgn, mod in level:
            ang = jnp.broadcast_to((2.0 * math.pi / mod) * (off + sgn * k), (seq // HY_PHASES, HEAD_DIM))
            tabs += [jnp.cos(ang), jnp.sin(ang)]
    return jnp.stack(tabs, axis=0)


def _spectrum_tree(leaves, twiddle, depth=0):
    if len(leaves) == 1:
        return leaves
    even = _spectrum_tree(leaves[0::2], twiddle, depth + 1)
    odd = _spectrum_tree(leaves[1::2], twiddle, depth + 1)
    out = []
    for j, (e, o) in enumerate(zip(even, odd)):
        out += list(_split_fwd(e, o, twiddle(depth, j)))
    return out


def _synthesis_tree(vals, twiddle, depth=0):
    if len(vals) == 1:
        return vals
    even, odd = [], []
    for j in range(len(vals) // 2):
        q0, q1 = _split_inv(vals[2 * j], vals[2 * j + 1], twiddle(depth, j))
        even.append(q0)
        odd.append(q1)
    out = [None] * len(vals)
    out[0::2] = _synthesis_tree(even, twiddle, depth + 1)
    out[1::2] = _synthesis_tree(odd, twiddle, depth + 1)
    return out


def _hy_pre_kernel(z_ref, w_ref, b_ref, ub_ref, ualt_ref, *, seq):
    q = seq // HY_PHASES
    alt = [jnp.zeros((1, HEAD_DIM), F32) for _ in range(HY_PHASES)]
    for m0, rows in _hy_chunks(q):
        v = _short_conv_phases(z_ref, w_ref, b_ref, m0, rows, q)
        sign = _alt_sign(m0, rows)
        for r in range(HY_PHASES):
            ub_ref[r * q + m0:r * q + m0 + rows, :] = v[r].astype(BF16)
            alt[r] = alt[r] + jnp.sum(v[r] * sign, axis=0, keepdims=True)
    for r in range(HY_PHASES):
        ualt_ref[r:r + 1, :] = alt[r]


def _hy_spec_kernel(u_ref, h_ref, a_ref, f_ref, g_ref, tw_ref, y_ref, *, seq, kf):
    c = pl.program_id(1)
    cw = HY_COLS
    q = seq // HY_PHASES
    n_fft = 2 * seq

    @pl.when(c == 0)
    def _():
        y_ref[...] = jnp.zeros_like(y_ref)

    f = f_ref[...]
    plan = _twiddle_plan(seq)

    def twiddles(width):
        rep = width // HEAD_DIM
        cache = {}

        def get(depth, j):
            idx = plan[depth][j][0]
            if idx not in cache:
                cache[idx] = (jnp.concatenate([tw_ref[2 * idx]] * rep, axis=1),
                              -jnp.concatenate([tw_ref[2 * idx + 1]] * rep, axis=1))
            return cache[idx]
        return get

    def spectrum(ref, width):
        leaves = []
        for r in range(HY_PHASES):
            d = jnp.dot(f, ref[r * q:(r + 1) * q, :], preferred_element_type=F32)
            leaves.append((d[:kf], d[kf:]))
        return _spectrum_tree(leaves, twiddles(width))

    row = lax.broadcasted_iota(jnp.int32, (kf, 1), 0)
    a = a_ref[...]
    wgt = jnp.where((row == 0) & (c == 0), 1.0 / n_fft, 2.0 / n_fft) / (a[:, :cw] + a[:, cw:] + 1e-6)
    filt = [((x[0][:, :cw] + x[0][:, cw:]) * wgt, (x[1][:, :cw] - x[1][:, cw:]) * wgt)
            for x in spectrum(h_ref, 2 * cw)]
    sig = spectrum(u_ref, BATCH * cw)
    tw_c = twiddles(cw)
    out = [[] for _ in range(HY_PHASES)]
    for b in range(BATCH):
        bs = slice(b * cw, (b + 1) * cw)
        p = [_cmul((x[0][:, bs], x[1][:, bs]), s) for x, s in zip(sig, filt)]
        for r, val in enumerate(_synthesis_tree(p, tw_c)):
            out[r].append(val)
    for r in range(HY_PHASES):
        spec = jnp.concatenate([jnp.concatenate([v[0] for v in out[r]], axis=1),
                                jnp.concatenate([v[1] for v in out[r]], axis=1)], axis=0).astype(BF16)
        for r0, rows in _hy_chunks(q, HY_INV_ROWS):
            y_ref[r * q + r0:r * q + r0 + rows, :] += jnp.dot(g_ref[r0:r0 + rows, :], spec,
                                                              preferred_element_type=F32)


def _edge_terms(ualt_ref, hf_ref, hb_ref, af_ref, ab_ref, seq):
    k_edge = seq // HY_PHASES
    plan = _twiddle_plan(seq)

    def twiddle(depth, j):
        _, off, sgn, mod = plan[depth][j]
        ang = 2.0 * math.pi * (off + sgn * k_edge) / mod
        return math.cos(ang), -math.sin(ang)

    def spectrum(ref):
        return _spectrum_tree([(ref[r:r + 1, :], 0.0) for r in range(HY_PHASES)], twiddle)

    sig, fwd, bwd = spectrum(ualt_ref), spectrum(hf_ref), spectrum(hb_ref)
    p = [_cmul(u, _cadd(f, _conj(b))) for u, f, b in zip(sig, fwd, bwd)]
    scale = (1.0 / (2 * seq)) / (af_ref[...] + ab_ref[...] + 1e-6)
    return [leaf[0] * scale for leaf in _synthesis_tree(p, twiddle)]


def _hy_mid_kernel(zv_ref, zx_ref, y_ref, ualt_ref, hf_ref, hb_ref, af_ref, ab_ref, wv_ref, bv_ref,
                   wx_ref, bx_ref, skip_ref, s_ref, sb_ref, salt_ref, *, seq):
    q = seq // HY_PHASES
    amp = _edge_terms(ualt_ref, hf_ref, hb_ref, af_ref, ab_ref, seq)
    alt = [jnp.zeros((1, HEAD_DIM), F32) for _ in range(HY_PHASES)]
    skip = skip_ref[...]
    for m0, rows in _hy_chunks(q):
        sign = _alt_sign(m0, rows)
        v = _short_conv_phases(zv_ref, wv_ref, bv_ref, m0, rows, q)
        x1 = _short_conv_phases(zx_ref, wx_ref, bx_ref, m0, rows, q)
        for r in range(HY_PHASES):
            rs = slice(r * q + m0, r * q + m0 + rows)
            s = x1[r] * (y_ref[rs, :] + sign * amp[r] + v[r] * skip)
            s_ref[rs, :] = s
            sb_ref[rs, :] = s.astype(BF16)
            alt[r] = alt[r] + jnp.sum(s * sign, axis=0, keepdims=True)
    for r in range(HY_PHASES):
        salt_ref[r:r + 1, :] = alt[r]


def _hy_post_kernel(zx_ref, y_ref, s_ref, salt_ref, hf_ref, hb_ref, af_ref, ab_ref, wx_ref, bx_ref,
                    skip_ref, o_ref, *, seq):
    q = seq // HY_PHASES
    amp = _edge_terms(salt_ref, hf_ref, hb_ref, af_ref, ab_ref, seq)
    skip = skip_ref[...]
    for m0, rows in _hy_chunks(q):
        sign = _alt_sign(m0, rows)
        x2 = _short_conv_phases(zx_ref, wx_ref, bx_ref, m0, rows, q)
        for r in range(HY_PHASES):
            rs = slice(r * q + m0, r * q + m0 + rows)
            y = y_ref[rs, :] + sign * amp[r] + s_ref[rs, :] * skip
            o_ref[pl.ds(HY_PHASES * m0 + r, rows, stride=HY_PHASES), :] = x2[r] * y


def _hyena(z, z_rb0, conv_w, conv_b, skip, filt, fmat, gmat, tw, *, layer, seq):
    hb, asum, halt = filt
    q = seq // HY_PHASES
    kf = min(HY_KF, q)
    n_ch = q // kf
    n_cols = HY_WIDTH // HY_COLS
    ew = HEAD_DIM
    sub = HY_COLS // ew
    n_ew = HY_WIDTH // ew
    hy0 = V_END // ew

    def zcols(which):
        return pl.BlockSpec((seq, ew), lambda b, e: (z_rb0 + b, hy0 + which * n_ew + e))

    def cw(which):
        return pl.BlockSpec((None, HY_SHORT, ew), lambda b, e: (layer, 0, which * n_ew + e))

    def cb(which):
        return pl.BlockSpec((None, 1, ew), lambda b, e: (layer, 0, which * n_ew + e))

    def fstat(order, direction, rows):
        return pl.BlockSpec((None, rows, ew),
                            lambda b, e: (layer, 0, ((order * n_cols + e // sub) * 2 + direction) * sub + e % sub))

    def skip_spec(order):
        return pl.BlockSpec((None, 1, ew), lambda b, e: (layer * HY_ORDER + order, 0, e))

    def act_col(b, e):
        return ((e // sub) * BATCH + b) * sub + e % sub

    seq_blk = pl.BlockSpec((seq, ew), lambda b, e: (0, act_col(b, e)))
    alt_spec = pl.BlockSpec((HY_PHASES, ew), lambda b, e: (0, act_col(b, e)))
    ew_params = _params("arbitrary", "arbitrary")
    act = jax.ShapeDtypeStruct((seq, BATCH * HY_WIDTH), F32)
    act_b = jax.ShapeDtypeStruct((seq, BATCH * HY_WIDTH), BF16)
    alt_shape = jax.ShapeDtypeStruct((HY_PHASES, BATCH * HY_WIDTH), F32)

    def spectral(ub, order):
        return pl.pallas_call(
            functools.partial(_hy_spec_kernel, seq=seq, kf=kf),
            grid=(n_cols, n_ch),
            in_specs=[
                pl.BlockSpec((seq, BATCH * HY_COLS), lambda j, c: (0, j)),
                pl.BlockSpec((None, seq, 2 * HY_COLS), lambda j, c: (layer, 0, order * n_cols + j),
                             pipeline_mode=pl.Buffered(1)),
                pl.BlockSpec((None, 1, 2 * HY_COLS), lambda j, c: (layer, 0, order * n_cols + j)),
                pl.BlockSpec((None, 2 * kf, q), lambda j, c: (c, 0, 0)),
                pl.BlockSpec((None, q, 2 * kf), lambda j, c: (c, 0, 0)),
                pl.BlockSpec((tw.shape[0], kf, HEAD_DIM), lambda j, c: (0, c, 0)),
            ],
            out_specs=pl.BlockSpec((seq, BATCH * HY_COLS), lambda j, c: (0, j)),
            out_shape=act,
            compiler_params=_params("arbitrary", "arbitrary"),
            name=f"hy_spec{order}_l{seq}",
        )(ub, hb, asum, fmat, gmat, tw)

    ub, ualt = pl.pallas_call(
        functools.partial(_hy_pre_kernel, seq=seq),
        grid=(BATCH, n_ew),
        in_specs=[zcols(0), cw(0), cb(0)],
        out_specs=[seq_blk, alt_spec],
        out_shape=[act_b, alt_shape],
        compiler_params=ew_params,
        name=f"hy_pre_l{seq}",
    )(z, conv_w, conv_b)
    y0 = spectral(ub, 0)
    s, sb, salt = pl.pallas_call(
        functools.partial(_hy_mid_kernel, seq=seq),
        grid=(BATCH, n_ew),
        in_specs=[zcols(0), zcols(1), seq_blk, alt_spec, fstat(0, 0, HY_PHASES), fstat(0, 1, HY_PHASES), fstat(0, 0, 1),
                  fstat(0, 1, 1), cw(0), cb(0), cw(1), cb(1), skip_spec(0)],
        out_specs=[seq_blk, seq_blk, alt_spec],
        out_shape=[act, act_b, alt_shape],
        compiler_params=ew_params,
        name=f"hy_mid_l{seq}",
    )(z, z, y0, ualt, halt, halt, asum, asum, conv_w, conv_b, conv_w, conv_b, skip)
    y1 = spectral(sb, 1)
    return pl.pallas_call(
        functools.partial(_hy_post_kernel, seq=seq),
        grid=(BATCH, n_ew),
        in_specs=[zcols(2), seq_blk, seq_blk, alt_spec, fstat(1, 0, HY_PHASES), fstat(1, 1, HY_PHASES),
                  fstat(1, 0, 1), fstat(1, 1, 1), cw(2), cb(2), skip_spec(1)],
        out_specs=pl.BlockSpec((seq, ew), lambda b, e: (b, e)),
        out_shape=jax.ShapeDtypeStruct((BATCH * seq, HY_WIDTH), F32),
        compiler_params=ew_params,
        name=f"hy_post_l{seq}",
    )(z, y1, s, salt, halt, halt, asum, asum, conv_w, conv_b, skip)


def _outproj_kernel(a_ref, b_ref, c_ref, cc_ref, w_ref, x_ref, mod_ref, g_ref, o_ref, *, lat_tiles):
    c = c_ref[...]
    if cc_ref is not None:
        c = jnp.where(pl.program_id(0) < lat_tiles, c, cc_ref[...])
    y = jnp.dot(a_ref[...], w_ref[:SGU_WIDTH, :], preferred_element_type=F32)
    y += jnp.dot(b_ref[...], w_ref[SGU_WIDTH:SGU_WIDTH + ATTN_WIDTH, :], preferred_element_type=F32)
    y += jnp.dot(c.astype(BF16), w_ref[SGU_WIDTH + ATTN_WIDTH:, :], preferred_element_type=F32)
    o_ref[...] = x_ref[...] + mod_ref[5:6, :] * _rms(y, g_ref[3:4, :])


def _outproj(a, b, c_lat, c_ctx, w_out, x, mods, norm_g, *, layer, n_tiles):
    lat_tiles = N_LAT // TM
    c_specs = [pl.BlockSpec((TM, HY_WIDTH), lambda i: (jnp.minimum(i, lat_tiles - 1), 0))]
    c_args = [c_lat]
    if c_ctx is not None:
        c_specs.append(pl.BlockSpec((TM, HY_WIDTH), lambda i: (jnp.maximum(i - lat_tiles, 0), 0)))
        c_args.append(c_ctx)

    def body(a_ref, b_ref, c_ref, *refs):
        cc_ref, rest = (refs[0], refs[1:]) if c_ctx is not None else (None, refs)
        _outproj_kernel(a_ref, b_ref, c_ref, cc_ref, *rest, lat_tiles=lat_tiles)

    return pl.pallas_call(
        body,
        grid=(n_tiles,),
        in_specs=[
            pl.BlockSpec((TM, SGU_WIDTH), lambda i: (i, 0)),
            pl.BlockSpec((TM, ATTN_WIDTH), lambda i: (i, 0)),
            *c_specs,
            pl.BlockSpec((D_MIX, D_MODEL), lambda i: (0, 0), pipeline_mode=pl.Buffered(1)),
            pl.BlockSpec((TM, D_MODEL), lambda i: (i, 0)),
            pl.BlockSpec((None, None, N_MOD, D_MODEL), lambda i: (layer, _mod_row(i), 0, 0)),
            pl.BlockSpec((None, 6, D_MODEL), lambda i: (layer, 0, 0)),
        ],
        out_specs=pl.BlockSpec((TM, D_MODEL), lambda i: (i, 0)),
        out_shape=jax.ShapeDtypeStruct((n_tiles * TM, D_MODEL), F32),
        compiler_params=_params("arbitrary"),
        name="outproj",
    )(a, b, *c_args, w_out, x, mods, norm_g)


def _rope_tables():
    rows = SEQ // GRID_W
    row = jnp.broadcast_to(jnp.arange(rows)[:, None], (rows, GRID_W)).reshape(SEQ).astype(F32)
    col = jnp.broadcast_to(jnp.arange(GRID_W)[None, :], (rows, GRID_W)).reshape(SEQ).astype(F32)
    half = HEAD_DIM // 2
    inv = ROPE_BASE ** (-jnp.arange(0, half, 2, dtype=F32) / half)
    ang_r = row[:, None] * inv
    ang_c = col[:, None] * inv
    ang = jnp.concatenate([ang_r, ang_r, ang_c, ang_c], axis=-1)
    cos, sin = jnp.cos(ang), jnp.sin(ang)
    first = (jnp.arange(HEAD_DIM) % half) < (half // 2)
    sa = jnp.where(first, -sin, 0.0)
    sb = jnp.where(first, 0.0, sin)

    def stack(lat, ctx_value):
        return jnp.concatenate([lat] * BATCH + [jnp.full((N_CTX, HEAD_DIM), ctx_value, F32)], axis=0)

    return stack(cos, 1.0), stack(sa, 0.0), stack(sb, 0.0)


def kernel(x, c, ctx, c_ctx, w_ada, b_ada, norm_g, ffn1_gate, ffn1_up, ffn1_down, w_in,
           sgu_g, sgu_w, sgu_b, attn_sink, hy_conv_w, hy_conv_b, hy_f_w1, hy_f_b1, hy_f_w2,
           hy_f_b2, hy_f_w3, hy_freq, hy_skip, w_out, ffn2_gate, ffn2_up, ffn2_down):
    xs = jnp.concatenate([x.reshape(N_LAT, D_MODEL), ctx.reshape(N_CTX, D_MODEL)], axis=0)
    cc = jnp.concatenate([c, c_ctx[None], jnp.zeros((MOD_ROWS - BATCH - 1, D_MODEL), F32)], axis=0)
    mods = _ada_mods(cc, w_ada, b_ada)

    w1 = [w[0].astype(BF16) for w in (ffn1_gate, ffn1_up, ffn1_down)]
    cos, sa, sb = _rope_tables()
    conv_b = hy_conv_b.reshape(DEPTH, 1, (HY_ORDER + 1) * HY_WIDTH)
    skip = hy_skip.reshape(DEPTH * HY_ORDER, 1, HY_WIDTH)
    filt_w = (hy_f_w1, hy_f_b1, hy_f_w2, hy_f_b2, hy_f_w3, hy_freq)
    filt_lat = _hyena_filters(SEQ, *filt_w)
    filt_ctx = _hyena_filters(CTX_LEN, *filt_w)
    f_lat, g_lat = _dft_matrices(SEQ // HY_PHASES, min(HY_KF, SEQ // HY_PHASES))
    f_ctx, g_ctx = _dft_matrices(CTX_LEN // HY_PHASES, min(HY_KF, CTX_LEN // HY_PHASES))
    tw_lat, tw_ctx = _twiddles(SEQ), _twiddles(CTX_LEN)

    all_tiles = N_TOK // TM
    lat_tiles = N_LAT // TM
    ctx_rb = N_LAT // CTX_LEN

    for i in range(DEPTH):
        last = i == DEPTH - 1
        n_tiles = lat_tiles if last else all_tiles
        n_rows = n_tiles * TM

        jobs = [(w, i) for w in (ffn2_gate, ffn2_up, ffn2_down, w_in, w_out)]
        xs, (w2g, w2u, w2d, w_in_b, w_out_b) = _ffn(xs, mods, norm_g, *w1, jobs, layer=i, k=0, n_tiles=all_tiles)
        z = _inproj(xs, mods, norm_g, w_in_b, cos, sa, sb, layer=i)

        a_out = _sgu(z, sgu_g, sgu_w, sgu_b, layer=i, n_rows=n_rows)
        b_out = _attention(z, attn_sink, layer=i, n_rows=n_rows)
        c_lat = _hyena(z, 0, hy_conv_w, conv_b, skip, filt_lat, f_lat, g_lat, tw_lat, layer=i, seq=SEQ)
        c_ctx = None
        if not last:
            c_ctx = _hyena(z, ctx_rb, hy_conv_w, conv_b, skip, filt_ctx, f_ctx, g_ctx, tw_ctx, layer=i, seq=CTX_LEN)

        xs = _outproj(a_out, b_out, c_lat, c_ctx, w_out_b, xs, mods, norm_g, layer=i, n_tiles=n_tiles)
        jobs = [] if last else [(w, i + 1) for w in (ffn1_gate, ffn1_up, ffn1_down)]
        xs, w1 = _ffn(xs, mods, norm_g, w2g, w2u, w2d, jobs, layer=i, k=2, n_tiles=n_tiles)

    return xs[:N_LAT].reshape(BATCH, SEQ, D_MODEL)
```
